```python
import math
import jax
import jax.numpy as jnp
from jax import lax
import numpy as np

D_MODEL = 1024
BATCH = 4
SEQ = 8192
DEPTH = 1

D_RNN = D_MODEL
RNN_BLOCKS = 8
RNN_BW = D_RNN // RNN_BLOCKS
CONV_W = 4
LRU_C = 8.0
N_HEADS = 8
HEAD_DIM = 128
D_ATTN = N_HEADS * HEAD_DIM
IDX_HEADS = 16
IDX_DIM = 64
TOPK_MAX = 256
Q_BLOCK = 128
N_BUCKETS = 32
MAX_DIST = 128
N_EXPERTS = 64
TOP_K = 8
N_GROUPS = 8
TOPK_GROUPS = 4
D_EXPERT = 256
D_SHARED = 256
ROUTED_SCALE = 2.5
EXPERT_CHUNK = 128
IN_SIZES = (D_RNN, D_RNN, D_ATTN, D_ATTN, D_ATTN, IDX_HEADS * IDX_DIM, IDX_DIM, IDX_HEADS, D_MODEL, D_MODEL)
W_IN = sum(IN_SIZES)
N_MOD = 6
EPS = 1e-6

kernel_name = 'hybrid_rglru_dsa_moe_block'


def rms_norm(x, g):
    xf = x.astype(jnp.float32)
    xf = xf * lax.rsqrt(jnp.mean(xf * xf, axis=-1, keepdims=True) + EPS)
    return (xf * g.astype(jnp.float32)).astype(x.dtype)


def t5_bucket(dist):
    max_exact = N_BUCKETS // 2
    d = jnp.maximum(dist, 0)
    df = jnp.maximum(d, 1).astype(jnp.float32)
    large = max_exact + (jnp.log(df / max_exact) / math.log(MAX_DIST / max_exact)
                         * (N_BUCKETS - max_exact)).astype(jnp.int32)
    large = jnp.minimum(large, N_BUCKETS - 1)
    return jnp.where(d < max_exact, d, large)


def causal_depthwise_conv(u, w, b):
    out = lax.conv_general_dilated(
        u, w[:, None, :].astype(u.dtype), window_strides=(1,), padding=((CONV_W - 1, 0),),
        dimension_numbers=('NWC', 'WIO', 'NWC'), feature_group_count=u.shape[-1])
    return out + b


def rg_lru(xc, w_a, b_a, w_x, b_x, lam):
    B, S, _ = xc.shape
    xb = xc.reshape(B, S, RNN_BLOCKS, RNN_BW)
    r = jax.nn.sigmoid(jnp.einsum('bsnd,nde->bsne', xb, w_a) + b_a).reshape(B, S, D_RNN)
    i = jax.nn.sigmoid(jnp.einsum('bsnd,nde->bsne', xb, w_x) + b_x).reshape(B, S, D_RNN)
    log_a = -LRU_C * r.astype(jnp.float32) * jax.nn.softplus(-lam.astype(jnp.float32))
    a = jnp.exp(log_a)
    b = jnp.sqrt(-jnp.expm1(2.0 * log_a)) * (i * xc).astype(jnp.float32)

    def combine(lhs, rhs):
        a1, b1 = lhs
        a2, b2 = rhs
        return a1 * a2, a2 * b1 + b2

    _, h = lax.associative_scan(combine, (a, b), axis=1)
    return h.astype(xc.dtype)


def dsa_attention(q, k, v, q_idx, k_idx, w_idx, rel_bias):
    B, S, H, Dh = q.shape
    n_blocks = S // Q_BLOCK
    topk = min(TOPK_MAX, S // 4)
    key_pos = jnp.arange(S, dtype=jnp.int32)
    take = jax.vmap(lambda arr, idx: arr[idx])

    def block(j):
        t0 = j * Q_BLOCK
        qpos = t0 + jnp.arange(Q_BLOCK, dtype=jnp.int32)
        qb = lax.dynamic_slice_in_dim(q, t0, Q_BLOCK, axis=1)
        qi = lax.dynamic_slice_in_dim(q_idx, t0, Q_BLOCK, axis=1)
        wi = lax.dynamic_slice_in_dim(w_idx, t0, Q_BLOCK, axis=1)
        dots = jnp.einsum('bqhd,bsd->bqhs', qi, k_idx) * (IDX_DIM ** -0.5)
        score = jnp.einsum('bqhs,bqh->bqs', jax.nn.relu(dots), wi).astype(jnp.float32)
        causal = key_pos[None, :] <= qpos[:, None]
        score = jnp.where(causal[None], score, -jnp.inf)
        _, sel = lax.top_k(score, topk)
        valid = sel <= qpos[None, :, None]
        k_sel = take(k, sel)
        v_sel = take(v, sel)
        logits = jnp.einsum('bqhd,bqkhd->bhqk', qb, k_sel).astype(jnp.float32) * (Dh ** -0.5)
        bias = rel_bias[t5_bucket(qpos[None, :, None] - sel)]
        logits = logits + jnp.transpose(bias, (0, 3, 1, 2)).astype(jnp.float32)
        logits = jnp.where(valid[:, None], logits, -jnp.inf)
        p = jax.nn.softmax(logits, axis=-1).astype(v.dtype)
        return jnp.einsum('bhqk,bqkhd->bqhd', p, v_sel)

    out = lax.map(block, jnp.arange(n_blocks, dtype=jnp.int32))
    return jnp.transpose(out, (1, 0, 2, 3, 4)).reshape(B, S, H * Dh)


def token_mixer(h, w_in, conv_w, conv_b, w_rg_a, b_rg_a, w_rg_x, b_rg_x, lru_lambda,
                w_br_rnn, w_br_attn, w_out, rel_bias):
    B, S, _ = h.shape
    z = h @ w_in
    splits = [int(o) for o in np.cumsum(IN_SIZES)[:-1]]
    u_rnn, u_gate, q, k, v, qi, ki, wi, gl_rnn, gl_attn = jnp.split(z, splits, axis=-1)
    xc = causal_depthwise_conv(u_rnn, conv_w, conv_b)
    y_rnn = rg_lru(xc, w_rg_a, b_rg_a, w_rg_x, b_rg_x, lru_lambda) * jax.nn.gelu(u_gate)
    y_attn = dsa_attention(
        q.reshape(B, S, N_HEADS, HEAD_DIM), k.reshape(B, S, N_HEADS, HEAD_DIM),
        v.reshape(B, S, N_HEADS, HEAD_DIM), qi.reshape(B, S, IDX_HEADS, IDX_DIM), ki,
        wi * (IDX_HEADS ** -0.5), rel_bias)
    merged = jax.nn.sigmoid(gl_rnn) * (y_rnn @ w_br_rnn) + jax.nn.sigmoid(gl_attn) * (y_attn @ w_br_attn)
    return merged @ w_out


def route(h, w_router, router_bias):
    N = h.shape[0]
    s = jax.nn.sigmoid((h @ w_router).astype(jnp.float32))
    s_sel = s + router_bias.astype(jnp.float32)
    grp = s_sel.reshape(N, N_GROUPS, N_EXPERTS // N_GROUPS)
    grp_score = lax.top_k(grp, 2)[0].sum(-1)
    _, top_g = lax.top_k(grp_score, TOPK_GROUPS)
    gmask = jax.nn.one_hot(top_g, N_GROUPS, dtype=jnp.float32).sum(1) > 0
    emask = jnp.repeat(gmask, N_EXPERTS // N_GROUPS, axis=1)
    _, sel = lax.top_k(jnp.where(emask, s_sel, -jnp.inf), TOP_K)
    w = jnp.take_along_axis(s, sel, axis=-1)
    w = w / jnp.sum(w, axis=-1, keepdims=True) * ROUTED_SCALE
    return sel, w


def routed_experts(h, sel, wts, w_gate, w_up, w_down):
    N, D = h.shape
    A = N * TOP_K
    flat_e = sel.reshape(A)
    flat_tok = jnp.arange(A, dtype=jnp.int32) // TOP_K
    flat_w = wts.reshape(A).astype(h.dtype)
    order = jnp.argsort(flat_e)
    sorted_e = flat_e[order]
    counts = jnp.bincount(flat_e, length=N_EXPERTS)
    padded = (counts + EXPERT_CHUNK - 1) // EXPERT_CHUNK * EXPERT_CHUNK
    start = jnp.cumsum(counts) - counts
    pad_end = jnp.cumsum(padded)
    pad_start = pad_end - padded
    dest = pad_start[sorted_e] + jnp.arange(A, dtype=jnp.int32) - start[sorted_e]
    n_chunks = -(-A // EXPERT_CHUNK) + N_EXPERTS
    P = n_chunks * EXPERT_CHUNK
    buf_tok = jnp.full((P,), N, jnp.int32).at[dest].set(flat_tok[order])
    buf_w = jnp.zeros((P,), h.dtype).at[dest].set(flat_w[order])
    chunk_start = jnp.arange(n_chunks, dtype=jnp.int32) * EXPERT_CHUNK
    chunk_e = jnp.minimum(jnp.searchsorted(pad_end, chunk_start, side='right'), N_EXPERTS - 1)
    h_pad = jnp.concatenate([h, jnp.zeros((1, D), h.dtype)], axis=0)

    def run(args):
        e, toks, w = args
        xs = h_pad[toks]
        y = (jax.nn.silu(xs @ w_gate[e]) * (xs @ w_up[e])) @ w_down[e]
        return y * w[:, None]

    ys = lax.map(run, (chunk_e, buf_tok.reshape(n_chunks, EXPERT_CHUNK),
                       buf_w.reshape(n_chunks, EXPERT_CHUNK)))
    return jax.ops.segment_sum(ys.reshape(P, D), buf_tok, num_segments=N + 1)[:N]


def moe_ffn(h, w_router, router_bias, w_exp_gate, w_exp_up, w_exp_down, w_sh_gate, w_sh_up, w_sh_down):
    sel, wts = route(h, w_router, router_bias)
    routed = routed_experts(h, sel, wts, w_exp_gate, w_exp_up, w_exp_down)
    shared = (jax.nn.silu(h @ w_sh_gate) * (h @ w_sh_up)) @ w_sh_down
    return routed + shared


def setup_inputs(seed: int = 0) -> dict:
    key = jax.random.key(seed)
    ks = jax.random.split(key, 25)
    s_d = D_MODEL ** -0.5

    def nrm(k, shape, scale):
        return scale * jax.random.normal(k, shape, jnp.float32)

    lru_u = jax.random.uniform(ks[12], (DEPTH, D_RNN), jnp.float32, minval=0.9, maxval=0.999)
    a0 = lru_u ** (1.0 / LRU_C)
    return {
        'x': nrm(ks[0], (BATCH, SEQ, D_MODEL), 1.0),
        'c': nrm(ks[1], (BATCH, D_MODEL), 1.0),
        'w_ada': nrm(ks[2], (DEPTH, D_MODEL, N_MOD * D_MODEL), 0.5 * s_d),
        'b_ada': nrm(ks[3], (DEPTH, N_MOD * D_MODEL), 0.02),
        'norm_gain': 1.0 + nrm(ks[4], (DEPTH, 4, D_MODEL), 0.02),
        'w_in': nrm(ks[5], (DEPTH, D_MODEL, W_IN), s_d),
        'conv_w': nrm(ks[6], (DEPTH, CONV_W, D_RNN), CONV_W ** -0.5),
        'conv_b': nrm(ks[7], (DEPTH, D_RNN), 0.02),
        'w_rg_a': nrm(ks[8], (DEPTH, RNN_BLOCKS, RNN_BW, RNN_BW), RNN_BW ** -0.5),
        'b_rg_a': nrm(ks[9], (DEPTH, RNN_BLOCKS, RNN_BW), 0.02),
        'w_rg_x': nrm(ks[10], (DEPTH, RNN_BLOCKS, RNN_BW, RNN_BW), RNN_BW ** -0.5),
        'b_rg_x': nrm(ks[11], (DEPTH, RNN_BLOCKS, RNN_BW), 0.02),
        'lru_lambda': jnp.log(a0) - jnp.log1p(-a0),
        'w_br_rnn': nrm(ks[13], (DEPTH, D_RNN, D_MODEL), D_RNN ** -0.5),
        'w_br_attn': nrm(ks[14], (DEPTH, D_ATTN, D_MODEL), D_ATTN ** -0.5),
        'w_out': nrm(ks[15], (DEPTH, D_MODEL, D_MODEL), s_d),
        'rel_bias': nrm(ks[16], (N_BUCKETS, N_HEADS), 0.5),
        'w_router': nrm(ks[17], (DEPTH, D_MODEL, N_EXPERTS), s_d),
        'router_bias': nrm(ks[18], (DEPTH, N_EXPERTS), 0.01),
        'w_exp_gate': nrm(ks[19], (DEPTH, N_EXPERTS, D_MODEL, D_EXPERT), s_d),
        'w_exp_up': nrm(ks[20], (DEPTH, N_EXPERTS, D_MODEL, D_EXPERT), s_d),
        'w_exp_down': nrm(ks[21], (DEPTH, N_EXPERTS, D_EXPERT, D_MODEL), D_EXPERT ** -0.5),
        'w_sh_gate': nrm(ks[22], (DEPTH, D_MODEL, D_SHARED), s_d),
        'w_sh_up': nrm(ks[23], (DEPTH, D_MODEL, D_SHARED), s_d),
        'w_sh_down': nrm(ks[24], (DEPTH, D_SHARED, D_MODEL), D_SHARED ** -0.5),
    }


def reference(x, c, w_ada, b_ada, norm_gain, w_in, conv_w, conv_b, w_rg_a, b_rg_a, w_rg_x, b_rg_x,
              lru_lambda, w_br_rnn, w_br_attn, w_out, rel_bias, w_router, router_bias,
              w_exp_gate, w_exp_up, w_exp_down, w_sh_gate, w_sh_up, w_sh_down):
    B, S, D = x.shape
    cond = jax.nn.silu(c)
    for l in range(DEPTH):
        mod = (cond @ w_ada[l] + b_ada[l])[:, None, :]
        shift_m, scale_m, gate_m, shift_f, scale_f, gate_f = jnp.split(mod, N_MOD, axis=-1)
        h = rms_norm(x, norm_gain[l, 0]) * (1.0 + scale_m) + shift_m
        y = token_mixer(h, w_in[l], conv_w[l], conv_b[l], w_rg_a[l], b_rg_a[l], w_rg_x[l], b_rg_x[l],
                        lru_lambda[l], w_br_rnn[l], w_br_attn[l], w_out[l], rel_bias)
        x = x + gate_m * rms_norm(y, norm_gain[l, 1])
        h = rms_norm(x, norm_gain[l, 2]) * (1.0 + scale_f) + shift_f
        y = moe_ffn(h.reshape(B * S, D), w_router[l], router_bias[l], w_exp_gate[l], w_exp_up[l],
                    w_exp_down[l], w_sh_gate[l], w_sh_up[l], w_sh_down[l]).reshape(B, S, D)
        x = x + gate_f * rms_norm(y, norm_gain[l, 3])
    return x
```

```python
import functools
import math

import jax
import jax.numpy as jnp
import numpy as np
from jax import lax
from jax.experimental import pallas as pl
from jax.experimental.pallas import tpu as pltpu

F32 = jnp.float32
BF16 = jnp.bfloat16

D_MODEL = 1024
RNN_BLOCKS = 8
RNN_BW = D_MODEL // RNN_BLOCKS
CONV_W = 4
LRU_C = 8.0
N_HEADS = 8
HEAD_DIM = 128
IDX_HEADS = 16
IDX_DIM = 64
TOPK_MAX = 256
N_BUCKETS = 32
MAX_DIST = 128
N_EXPERTS = 64
TOP_K = 8
N_GROUPS = 8
TOPK_GROUPS = 4
D_EXPERT = 256
ROUTED_SCALE = 2.5
N_MOD = 6
EPS = 1e-6

LANES = 128
SUBLANES = 8
VMEM_LIMIT_BYTES = 58 * 1024 * 1024

ROW_TILE = 256
SCAN_TILE = 512
Q_TILE = 256
KEY_SUB = 128
MOE_TILE = 512
ROUTER_TILE = 1024

NEG_INF = float("-inf")


def _cparams(*sem):
    return pltpu.CompilerParams(dimension_semantics=sem, vmem_limit_bytes=VMEM_LIMIT_BYTES)


def _bdot(a, b):
    return jnp.dot(a.astype(BF16), b.astype(BF16), preferred_element_type=F32)


def _bdot_nt(a, b):
    return lax.dot_general(a.astype(BF16), b.astype(BF16), (((1,), (1,)), ((), ())),
                           preferred_element_type=F32)


def _rms(x, g):
    ms = jnp.mean(x * x, axis=-1, keepdims=True)
    return x * lax.rsqrt(ms + EPS) * g


def _ada_kernel(c_ref, w_ref, b_ref, o_ref):
    c = c_ref[...]
    cond = c * jax.nn.sigmoid(c)
    o_ref[...] = _bdot(cond, w_ref[...]) + b_ref[...]


def _ada(c, w_ada, b_ada):
    bsz, d = c.shape
    n = w_ada.shape[1]
    tn = 1024
    return pl.pallas_call(
        _ada_kernel,
        grid=(n // tn,),
        in_specs=[pl.BlockSpec((bsz, d), lambda j: (0, 0)),
                  pl.BlockSpec((d, tn), lambda j: (0, j)),
                  pl.BlockSpec((1, tn), lambda j: (0, j))],
        out_specs=pl.BlockSpec((bsz, tn), lambda j: (0, j)),
        out_shape=jax.ShapeDtypeStruct((bsz, n), F32),
        compiler_params=_cparams("arbitrary"),
        name="ada",
    )(c, w_ada, b_ada.reshape(1, n))


def _inproj_kernel(x_ref, mod_ref, g_ref, wn_ref, wki_ref, wt_ref, wwi_ref,
                   urnn_ref, ugate_ref, k_ref, glr_ref, gla_ref, ki_ref,
                   qt_ref, vt_ref, qit_ref, wit_ref):
    d = D_MODEL
    x = x_ref[0]
    h = _rms(x, g_ref[...]) * (1.0 + mod_ref[0, 1:2, :]) + mod_ref[0, 0:1, :]
    hb = h.astype(BF16)
    urnn_ref[0] = jnp.dot(hb, wn_ref[:, 0 * d:1 * d], preferred_element_type=F32)
    ugate_ref[0] = jnp.dot(hb, wn_ref[:, 1 * d:2 * d], preferred_element_type=F32)
    k_ref[0] = jnp.dot(hb, wn_ref[:, 2 * d:3 * d], preferred_element_type=F32).astype(BF16)
    glr_ref[0] = jnp.dot(hb, wn_ref[:, 3 * d:4 * d], preferred_element_type=F32)
    gla_ref[0] = jnp.dot(hb, wn_ref[:, 4 * d:5 * d], preferred_element_type=F32)
    ki_ref[0] = jnp.dot(hb, wki_ref[...], preferred_element_type=F32).astype(BF16)
    nt = (((1,), (1,)), ((), ()))
    qt_ref[0] = (lax.dot_general(wt_ref[0 * d:1 * d, :], hb, nt, preferred_element_type=F32)
                 * (HEAD_DIM ** -0.5)).astype(BF16)
    vt_ref[0] = lax.dot_general(wt_ref[1 * d:2 * d, :], hb, nt, preferred_element_type=F32).astype(BF16)
    qit_ref[0] = lax.dot_general(wt_ref[2 * d:3 * d, :], hb, nt, preferred_element_type=F32).astype(BF16)
    wit_ref[0] = lax.dot_general(wwi_ref[...], hb, nt, preferred_element_type=F32)


def _in_proj(x, mod3, g0, w_in):
    bsz, s, d = x.shape
    tm = ROW_TILE
    offs = np.cumsum([0, d, d, d, d, d, IDX_HEADS * IDX_DIM, IDX_DIM, IDX_HEADS, d, d])
    seg = lambda i: w_in[:, int(offs[i]):int(offs[i + 1])]
    wn = jnp.concatenate([seg(0), seg(1), seg(3), seg(8), seg(9)], axis=1).astype(BF16)
    wki = seg(6).astype(BF16)
    wt = jnp.concatenate([seg(2), seg(4), seg(5)], axis=1).T.astype(BF16)
    wwi = seg(7).T.astype(BF16)
    const = lambda shape: pl.BlockSpec(shape, lambda b, i: (0,) * len(shape),
                                       pipeline_mode=pl.Buffered(1))
    row = lambda w: pl.BlockSpec((1, tm, w), lambda b, i: (b, i, 0))
    col = lambda r: pl.BlockSpec((1, r, tm), lambda b, i: (b, 0, i))
    out_shape = (
        jax.ShapeDtypeStruct((bsz, s, d), F32),
        jax.ShapeDtypeStruct((bsz, s, d), F32),
        jax.ShapeDtypeStruct((bsz, s, d), BF16),
        jax.ShapeDtypeStruct((bsz, s, d), F32),
        jax.ShapeDtypeStruct((bsz, s, d), F32),
        jax.ShapeDtypeStruct((bsz, s, IDX_DIM), BF16),
        jax.ShapeDtypeStruct((bsz, d, s), BF16),
        jax.ShapeDtypeStruct((bsz, d, s), BF16),
        jax.ShapeDtypeStruct((bsz, d, s), BF16),
        jax.ShapeDtypeStruct((bsz, IDX_HEADS, s), F32),
    )
    return pl.pallas_call(
        _inproj_kernel,
        grid=(bsz, s // tm),
        in_specs=[row(d),
                  pl.BlockSpec((1, N_MOD, d), lambda b, i: (b, 0, 0)),
                  pl.BlockSpec((1, d), lambda b, i: (0, 0)),
                  const(wn.shape), const(wki.shape), const(wt.shape), const(wwi.shape)],
        out_specs=(row(d), row(d), row(d), row(d), row(d), row(IDX_DIM),
                   col(d), col(d), col(d), col(IDX_HEADS)),
        out_shape=out_shape,
        compiler_params=_cparams("arbitrary", "arbitrary"),
        name="in_proj",
    )(x, mod3, g0, wn, wki, wt, wwi)


def _gelu_tanh(x):
    return 0.5 * x * (1.0 + jnp.tanh(math.sqrt(2.0 / math.pi) * (x + 0.044715 * (x * x * x))))


def _rglru_kernel(u_ref, ug_ref, cw_ref, cb_ref, wax_ref, ba_ref, bx_ref, lam_ref, y_ref,
                  ext_ref, a_ref, b_ref, carry_ref):
    ts = u_ref.shape[1]
    d = D_MODEL

    @pl.when(pl.program_id(1) == 0)
    def _():
        ext_ref[0:SUBLANES, :] = jnp.zeros((SUBLANES, d), F32)
        carry_ref[...] = jnp.zeros_like(carry_ref)

    ext_ref[SUBLANES:SUBLANES + ts, :] = u_ref[0]
    xc = cb_ref[...] + cw_ref[CONV_W - 1:CONV_W, :] * ext_ref[SUBLANES:SUBLANES + ts, :]
    for k in range(CONV_W - 1):
        off = SUBLANES - (CONV_W - 1) + k
        xc = xc + cw_ref[k:k + 1, :] * ext_ref[off:off + ts, :]
    ext_ref[0:SUBLANES, :] = ext_ref[ts:ts + SUBLANES, :]

    nl = -lam_ref[...]
    sp = jnp.maximum(nl, 0.0) + jnp.log1p(jnp.exp(-jnp.abs(nl)))
    for n in range(RNN_BLOCKS):
        cs = slice(n * RNN_BW, (n + 1) * RNN_BW)
        xb = xc[:, cs]
        g = _bdot(xb, wax_ref[n])
        r = jax.nn.sigmoid(g[:, :RNN_BW] + ba_ref[:, cs])
        i = jax.nn.sigmoid(g[:, RNN_BW:] + bx_ref[:, cs])
        log_a = (-LRU_C) * r * sp[:, cs]
        a_ref[:, cs] = jnp.exp(log_a)
        th = jnp.tanh(log_a)
        b_ref[:, cs] = jnp.sqrt(-2.0 * th / (1.0 - th)) * (i * xb)

    row = lax.broadcasted_iota(jnp.int32, (SUBLANES, d), 0)

    def group(gi, hprev):
        r0 = pl.multiple_of(gi * SUBLANES, SUBLANES)
        a = a_ref[pl.ds(r0, SUBLANES), :]
        b = b_ref[pl.ds(r0, SUBLANES), :]
        for sh in (1, 2, 4):
            keep = row >= sh
            a_s = jnp.where(keep, pltpu.roll(a, sh, 0), 1.0)
            b_s = jnp.where(keep, pltpu.roll(b, sh, 0), 0.0)
            b = a * b_s + b
            a = a * a_s
        h = b + a * hprev
        b_ref[pl.ds(r0, SUBLANES), :] = h
        return jnp.broadcast_to(h[SUBLANES - 1:SUBLANES, :], (SUBLANES, d))

    carry_ref[...] = lax.fori_loop(0, ts // SUBLANES, group, carry_ref[...])
    y_ref[0] = (b_ref[...] * _gelu_tanh(ug_ref[0])).astype(BF16)


def _rglru(u_rnn, u_gate, conv_w, conv_b, w_rg_a, b_rg_a, w_rg_x, b_rg_x, lam):
    bsz, s, d = u_rnn.shape
    ts = SCAN_TILE
    wax = jnp.concatenate([w_rg_a, w_rg_x], axis=-1).astype(BF16)
    vec = lambda v: v.reshape(1, d)
    c2 = lambda shape: pl.BlockSpec(shape, lambda b, i: (0,) * len(shape))
    row = pl.BlockSpec((1, ts, d), lambda b, i: (b, i, 0))
    return pl.pallas_call(
        _rglru_kernel,
        grid=(bsz, s // ts),
        in_specs=[row, row, c2((CONV_W, d)), c2((1, d)), c2(wax.shape), c2((1, d)), c2((1, d)),
                  c2((1, d))],
        out_specs=row,
        out_shape=jax.ShapeDtypeStruct((bsz, s, d), BF16),
        scratch_shapes=[pltpu.VMEM((ts + SUBLANES, d), F32), pltpu.VMEM((ts, d), F32),
                        pltpu.VMEM((ts, d), F32), pltpu.VMEM((SUBLANES, d), F32)],
        compiler_params=_cparams("arbitrary", "arbitrary"),
        name="rglru",
    )(u_rnn, u_gate, conv_w, vec(conv_b), wax, vec(b_rg_a), vec(b_rg_x), vec(lam))


def _t5_bucket_np(dist):
    max_exact = N_BUCKETS // 2
    dd = np.maximum(dist, 0)
    df = np.maximum(dd, 1).astype(np.float32)
    large = max_exact + (np.log(df / np.float32(max_exact)) / np.float32(math.log(MAX_DIST / max_exact))
                         * np.float32(N_BUCKETS - max_exact)).astype(np.int32)
    large = np.minimum(large, N_BUCKETS - 1)
    return np.where(dd < max_exact, dd, large)


def _near_bias_tables(rel_bias):
    r = np.arange(KEY_SUB)[None, :, None]
    c = np.arange(Q_TILE)[None, None, :]
    o = np.arange(3)[:, None, None]
    dist = c - r - (o - 1) * KEY_SUB
    bucket = _t5_bucket_np(dist)
    tab = rel_bias[bucket] - rel_bias[N_BUCKETS - 1]
    return jnp.transpose(tab, (3, 0, 1, 2)).astype(F32)


def _attn_kernel(qt_ref, qit_ref, wit_ref, k_ref, vt_ref, ki_ref, tab_ref, y_ref,
                 s_ref, acc_ref, m_ref, l_ref):
    tq = Q_TILE
    ks = KEY_SUB
    jq = pl.program_id(1)
    t0 = jq * tq
    nsub = (t0 + tq) // ks
    lane_t = t0 + lax.broadcasted_iota(jnp.int32, (1, tq), 1)
    wi = wit_ref[0] * (IDX_HEADS ** -0.5 * IDX_DIM ** -0.5)

    def score_group(i, masked):
        r0 = pl.multiple_of(i * ks, ks)
        kic = ki_ref[0, pl.ds(r0, ks), :]
        acc = jnp.zeros((ks, tq), F32)
        for h in range(IDX_HEADS):
            dts = jnp.dot(kic, qit_ref[0, h * IDX_DIM:(h + 1) * IDX_DIM, :],
                          preferred_element_type=F32)
            acc = acc + jnp.maximum(dts, 0.0) * wi[h:h + 1, :]
        if masked:
            key_s = r0 + lax.broadcasted_iota(jnp.int32, (ks, tq), 0)
            acc = jnp.where(key_s <= lane_t, acc, NEG_INF)
        s_ref[pl.ds(r0, ks), :] = acc

    def _score_body(i, c):
        score_group(i, False)
        return c

    lax.fori_loop(0, nsub - 2, _score_body, 0)
    score_group(nsub - 2, True)
    score_group(nsub - 1, True)

    k_eff = jnp.minimum(lane_t + 1, TOPK_MAX).astype(F32)

    def col_reduce(fn, init, combine):
        def body(i, c):
            r0 = pl.multiple_of(i * ks, ks)
            blk = fn(s_ref[pl.ds(r0, ks), :])
            return combine(c, blk.reshape(ks // SUBLANES, SUBLANES, tq))
        return lax.fori_loop(0, nsub, body, init)

    def count_ge(thr):
        c8 = col_reduce(lambda blk: jnp.where(blk >= thr, 1.0, 0.0),
                        jnp.zeros((SUBLANES, tq), F32),
                        lambda c, b3: c + jnp.sum(b3, axis=0))
        return jnp.sum(c8, axis=0, keepdims=True)

    mx8 = col_reduce(lambda blk: blk, jnp.full((SUBLANES, tq), NEG_INF, F32),
                     lambda c, b3: jnp.maximum(c, jnp.max(b3, axis=0)))
    mn8 = col_reduce(lambda blk: jnp.where(blk == NEG_INF, jnp.inf, blk),
                     jnp.full((SUBLANES, tq), jnp.inf, F32),
                     lambda c, b3: jnp.minimum(c, jnp.min(b3, axis=0)))
    smax = jnp.max(mx8, axis=0, keepdims=True)
    smin = jnp.min(mn8, axis=0, keepdims=True)
    c_max = count_ge(smax)
    all_sel = (lane_t + 1).astype(F32) <= k_eff
    top_tie = c_max >= k_eff
    lo0 = jnp.where(top_tie & ~all_sel, smax, smin)
    done0 = all_sel | top_tie

    def bis_cond(st):
        return st[3] > 0.0

    def bis_body(st):
        lo, hi, done, _ = st
        mid = 0.5 * lo + 0.5 * hi
        cnt = count_ge(mid)
        collapsed = (mid <= lo) | (mid >= hi)
        ge = cnt >= k_eff
        upd = (done == 0.0) & ~collapsed
        lo_n = jnp.where(upd & ge, mid, lo)
        hi_n = jnp.where(upd & ~ge, mid, hi)
        done_n = jnp.where(collapsed | (cnt == k_eff), 1.0, done)
        return lo_n, hi_n, done_n, jnp.sum(1.0 - done_n)

    done0f = done0.astype(F32)
    thr, _, _, _ = lax.while_loop(bis_cond, bis_body,
                                  (lo0, smax, done0f, jnp.sum(1.0 - done0f)))

    c_ge = count_ge(thr)
    c_gt = col_reduce(lambda blk: jnp.where(blk > thr, 1.0, 0.0),
                      jnp.zeros((SUBLANES, tq), F32),
                      lambda c, b3: c + jnp.sum(b3, axis=0))
    need = k_eff - jnp.sum(c_gt, axis=0, keepdims=True)
    excess = jnp.sum(jnp.where(c_ge > k_eff, 1.0, 0.0))
    tie_hi0 = jnp.full((1, tq), 1.0, F32) * (t0 + tq).astype(F32)

    def tie_limit():
        def count_tie_below(lim):
            def body(i, c):
                r0 = pl.multiple_of(i * ks, ks)
                blk = s_ref[pl.ds(r0, ks), :]
                key_s = (r0 + lax.broadcasted_iota(jnp.int32, (ks, tq), 0)).astype(F32)
                hit = jnp.where((blk == thr) & (key_s < lim), 1.0, 0.0)
                return c + jnp.sum(hit.reshape(ks // SUBLANES, SUBLANES, tq), axis=0)
            c8 = lax.fori_loop(0, nsub, body, jnp.zeros((SUBLANES, tq), F32))
            return jnp.sum(c8, axis=0, keepdims=True)

        def body(_, st):
            lo, hi = st
            mid = jnp.floor(0.5 * (lo + hi))
            ok = count_tie_below(mid) >= need
            return jnp.where(ok, lo, mid), jnp.where(ok, mid, hi)

        n_steps = int(math.ceil(math.log2(s_ref.shape[0]))) + 1
        _, hi = lax.fori_loop(0, n_steps, body, (jnp.zeros((1, tq), F32), tie_hi0))
        return hi

    tie_lim = lax.cond(excess > 0.0, tie_limit, lambda: tie_hi0)

    def mask_body(i, c):
        r0 = pl.multiple_of(i * ks, ks)
        blk = s_ref[pl.ds(r0, ks), :]
        key_s = (r0 + lax.broadcasted_iota(jnp.int32, (ks, tq), 0)).astype(F32)
        sel = (blk > thr) | ((blk == thr) & (key_s < tie_lim))
        s_ref[pl.ds(r0, ks), :] = jnp.where(sel, 0.0, NEG_INF)
        return c

    lax.fori_loop(0, nsub, mask_body, 0)

    m_ref[...] = jnp.full_like(m_ref, NEG_INF)
    l_ref[...] = jnp.zeros_like(l_ref)
    acc_ref[...] = jnp.zeros_like(acc_ref)

    def attend(i, near):
        r0 = pl.multiple_of(i * ks, ks)
        msk = s_ref[pl.ds(r0, ks), :]
        for h in range(N_HEADS):
            hs = slice(h * HEAD_DIM, (h + 1) * HEAD_DIM)
            kh = k_ref[0, pl.ds(r0, ks), hs]
            lg = jnp.dot(kh, qt_ref[0, hs, :], preferred_element_type=F32) + msk
            if near is not None:
                lg = lg + tab_ref[h, near]
            m_old = m_ref[h:h + 1, :]
            m_new = jnp.maximum(m_old, jnp.max(lg, axis=0, keepdims=True))
            m_safe = jnp.where(m_new == NEG_INF, 0.0, m_new)
            p = jnp.exp(lg - m_safe)
            alpha = jnp.exp(m_old - m_safe)
            l_ref[h:h + 1, :] = alpha * l_ref[h:h + 1, :] + jnp.sum(p, axis=0, keepdims=True)
            m_ref[h:h + 1, :] = m_new
            vh = vt_ref[0, hs, pl.ds(r0, ks)]
            acc_ref[h] = alpha * acc_ref[h] + jnp.dot(vh, p.astype(BF16), preferred_element_type=F32)

    def _attend_body(i, c):
        attend(i, None)
        return c

    lax.fori_loop(0, jnp.maximum(nsub - 3, 0), _attend_body, 0)

    @pl.when(jq > 0)
    def _():
        attend(nsub - 3, 0)

    attend(nsub - 2, 1)
    attend(nsub - 1, 2)

    for h in range(N_HEADS):
        o = acc_ref[h] / l_ref[h:h + 1, :]
        y_ref[0, :, h * HEAD_DIM:(h + 1) * HEAD_DIM] = o.T.astype(BF16)


def _attention(qt, qit, wit, k, vt, ki, tab):
    bsz, d, s = qt.shape
    tq = Q_TILE
    once = lambda shape: pl.BlockSpec(shape, lambda b, j: (b,) + (0,) * (len(shape) - 1),
                                      pipeline_mode=pl.Buffered(1))
    col = lambda r: pl.BlockSpec((1, r, tq), lambda b, j: (b, 0, j))
    return pl.pallas_call(
        _attn_kernel,
        grid=(bsz, s // tq),
        in_specs=[col(d), col(d), col(IDX_HEADS),
                  once((1, s, d)), once((1, d, s)), once((1, s, IDX_DIM)),
                  pl.BlockSpec(tab.shape, lambda b, j: (0, 0, 0, 0))],
        out_specs=pl.BlockSpec((1, tq, d), lambda b, j: (b, j, 0)),
        out_shape=jax.ShapeDtypeStruct((bsz, s, d), BF16),
        scratch_shapes=[pltpu.VMEM((s, tq), F32),
                        pltpu.VMEM((N_HEADS, HEAD_DIM, tq), F32),
                        pltpu.VMEM((N_HEADS, tq), F32),
                        pltpu.VMEM((N_HEADS, tq), F32)],
        compiler_params=_cparams("arbitrary", "arbitrary"),
        name="attn",
    )(qt, qit, wit, k, vt, ki, tab)


def _merge_kernel(yr_ref, ya_ref, glr_ref, gla_ref, x_ref, mod_ref, g_ref,
                  wr_ref, wa_ref, wo_ref, x1_ref, h2_ref):
    merged = (jax.nn.sigmoid(glr_ref[0]) * jnp.dot(yr_ref[0], wr_ref[...], preferred_element_type=F32)
              + jax.nn.sigmoid(gla_ref[0]) * jnp.dot(ya_ref[0], wa_ref[...], preferred_element_type=F32))
    y = _bdot(merged, wo_ref[...])
    x1 = x_ref[0] + mod_ref[0, 2:3, :] * _rms(y, g_ref[1:2, :])
    x1_ref[0] = x1
    h2 = _rms(x1, g_ref[2:3, :]) * (1.0 + mod_ref[0, 4:5, :]) + mod_ref[0, 3:4, :]
    h2_ref[0] = h2.astype(BF16)


def _merge(y_rnn, y_attn, gl_rnn, gl_attn, x, mod3, gains, w_br_rnn, w_br_attn, w_out):
    bsz, s, d = x.shape
    tm = ROW_TILE
    row = pl.BlockSpec((1, tm, d), lambda b, i: (b, i, 0))
    c2 = lambda shape: pl.BlockSpec(shape, lambda b, i: (0,) * len(shape))
    return pl.pallas_call(
        _merge_kernel,
        grid=(bsz, s // tm),
        in_specs=[row, row, row, row, row,
                  pl.BlockSpec((1, N_MOD, d), lambda b, i: (b, 0, 0)),
                  c2(gains.shape), c2((d, d)), c2((d, d)), c2((d, d))],
        out_specs=(row, row),
        out_shape=(jax.ShapeDtypeStruct((bsz, s, d), F32), jax.ShapeDtypeStruct((bsz, s, d), BF16)),
        compiler_params=_cparams("arbitrary", "arbitrary"),
        name="merge",
    )(y_rnn, y_attn, gl_rnn, gl_attn, x, mod3, gains,
      w_br_rnn.astype(BF16), w_br_attn.astype(BF16), w_out.astype(BF16))


def _router_kernel(h_ref, wr_ref, rb_ref, o_ref):
    t = h_ref.shape[0]
    gsz = N_EXPERTS // N_GROUPS
    s = jax.nn.sigmoid(_bdot_nt(wr_ref[...], h_ref[...]))
    s_sel = s + rb_ref[...]
    g3 = s_sel.reshape(N_GROUPS, gsz, t)
    e_in_g = lax.broadcasted_iota(jnp.int32, (N_GROUPS, gsz, t), 1)
    top1 = jnp.max(g3, axis=1, keepdims=True)
    first = jnp.min(jnp.where(g3 == top1, e_in_g, gsz), axis=1, keepdims=True)
    top2 = jnp.max(jnp.where(e_in_g == first, NEG_INF, g3), axis=1, keepdims=True)
    gscore = jnp.broadcast_to(top1 + top2, (N_GROUPS, gsz, t))
    gi = lax.broadcasted_iota(jnp.int32, (N_GROUPS, gsz, t), 0)
    gmask = jnp.zeros((N_GROUPS, gsz, t), F32)
    for _ in range(TOPK_GROUPS):
        mx = jnp.max(gscore, axis=0, keepdims=True)
        pick = jnp.min(jnp.where(gscore == mx, gi, N_GROUPS), axis=0, keepdims=True)
        hit = gi == pick
        gmask = jnp.where(hit, 1.0, gmask)
        gscore = jnp.where(hit, NEG_INF, gscore)
    cand = jnp.where(gmask.reshape(N_EXPERTS, t) > 0.0, s_sel, NEG_INF)
    ei = lax.broadcasted_iota(jnp.int32, (N_EXPERTS, t), 0)
    sel = jnp.zeros((N_EXPERTS, t), F32)
    for _ in range(TOP_K):
        mx = jnp.max(cand, axis=0, keepdims=True)
        pick = jnp.min(jnp.where(cand == mx, ei, N_EXPERTS), axis=0, keepdims=True)
        hit = ei == pick
        sel = jnp.where(hit, 1.0, sel)
        cand = jnp.where(hit, NEG_INF, cand)
    w = s * sel
    w = w / jnp.sum(w, axis=0, keepdims=True) * ROUTED_SCALE
    o_ref[...] = w.T


def _router(h2, w_router, router_bias):
    n, d = h2.shape
    t = ROUTER_TILE
    return pl.pallas_call(
        _router_kernel,
        grid=(n // t,),
        in_specs=[pl.BlockSpec((t, d), lambda i: (i, 0)),
                  pl.BlockSpec((N_EXPERTS, d), lambda i: (0, 0)),
                  pl.BlockSpec((N_EXPERTS, 1), lambda i: (0, 0))],
        out_specs=pl.BlockSpec((t, N_EXPERTS), lambda i: (i, 0)),
        out_shape=jax.ShapeDtypeStruct((n, N_EXPERTS), F32),
        compiler_params=_cparams("arbitrary"),
        name="router",
    )(h2, w_router.T.astype(BF16), router_bias.reshape(N_EXPERTS, 1))


def _moe_kernel(h_ref, w_ref, wgu_ref, wd_ref, x1_ref, mod_ref, g_ref, o_ref, acc_ref):
    e = pl.program_id(1)
    n_e = pl.num_programs(1)

    @pl.when(e == 0)
    def _():
        acc_ref[...] = jnp.zeros_like(acc_ref)

    gu = jnp.dot(h_ref[...], wgu_ref[0], preferred_element_type=F32)
    g = gu[:, :D_EXPERT]
    a = (g * jax.nn.sigmoid(g)) * gu[:, D_EXPERT:]
    y = _bdot(a, wd_ref[0])
    lane = lax.broadcasted_iota(jnp.int32, w_ref.shape, 1)
    wcol = jnp.sum(jnp.where(lane == e, w_ref[...], 0.0), axis=1, keepdims=True)
    wcol = jnp.where(e == n_e - 1, 1.0, wcol)
    acc_ref[...] += y * wcol

    @pl.when(e == n_e - 1)
    def _():
        o_ref[...] = x1_ref[...] + mod_ref[0, 5:6, :] * _rms(acc_ref[...], g_ref[...])


def _moe(h2, wdense, wgu, wd, x1, mod3, g3, seq):
    n, d = h2.shape
    t = MOE_TILE
    n_e = wgu.shape[0]
    tiles_per_seq = seq // t
    tok = lambda w: pl.BlockSpec((t, w), lambda i, e: (i, 0))
    return pl.pallas_call(
        _moe_kernel,
        grid=(n // t, n_e),
        in_specs=[tok(d), tok(N_EXPERTS),
                  pl.BlockSpec((1, d, 2 * D_EXPERT), lambda i, e: (e, 0, 0)),
                  pl.BlockSpec((1, D_EXPERT, d), lambda i, e: (e, 0, 0)),
                  tok(d),
                  pl.BlockSpec((1, N_MOD, d), lambda i, e: (i // tiles_per_seq, 0, 0)),
                  pl.BlockSpec((1, d), lambda i, e: (0, 0))],
        out_specs=tok(d),
        out_shape=jax.ShapeDtypeStruct((n, d), F32),
        scratch_shapes=[pltpu.VMEM((t, d), F32)],
        compiler_params=_cparams("arbitrary", "arbitrary"),
        name="moe",
    )(h2, wdense, wgu, wd, x1, mod3, g3)


def kernel(x, c, w_ada, b_ada, norm_gain, w_in, conv_w, conv_b, w_rg_a, b_rg_a, w_rg_x, b_rg_x,
           lru_lambda, w_br_rnn, w_br_attn, w_out, rel_bias, w_router, router_bias,
           w_exp_gate, w_exp_up, w_exp_down, w_sh_gate, w_sh_up, w_sh_down):
    bsz, s, d = x.shape
    depth = w_ada.shape[0]
    tab = _near_bias_tables(rel_bias)
    for l in range(depth):
        mod = _ada(c, w_ada[l], b_ada[l])
        mod3 = mod.reshape(bsz, N_MOD, d)
        gains = norm_gain[l]
        (u_rnn, u_gate, k, gl_rnn, gl_attn, ki, qt, vt, qit, wit) = _in_proj(x, mod3, gains[0:1], w_in[l])
        y_rnn = _rglru(u_rnn, u_gate, conv_w[l], conv_b[l], w_rg_a[l], b_rg_a[l], w_rg_x[l],
                       b_rg_x[l], lru_lambda[l])
        y_attn = _attention(qt, qit, wit, k, vt, ki, tab)
        x1, h2 = _merge(y_rnn, y_attn, gl_rnn, gl_attn, x, mod3, gains, w_br_rnn[l], w_br_attn[l],
                        w_out[l])
        h2f = h2.reshape(bsz * s, d)
        wdense = _router(h2f, w_router[l], router_bias[l])
        wgu = jnp.concatenate(
            [jnp.concatenate([w_exp_gate[l], w_exp_up[l]], axis=-1),
             jnp.concatenate([w_sh_gate[l], w_sh_up[l]], axis=-1)[None]], axis=0).astype(BF16)
        wd = jnp.concatenate([w_exp_down[l], w_sh_down[l][None]], axis=0).astype(BF16)
        x = _moe(h2f, wdense, wgu, wd, x1.reshape(bsz * s, d), mod3, gains[3:4], s).reshape(bsz, s, d)
    return x
```

```python
import functools
import math

import jax
import jax.numpy as jnp
import numpy as np
from jax import lax
from jax.experimental import pallas as pl
from jax.experimental.pallas import tpu as pltpu

F32 = jnp.float32
BF16 = jnp.bfloat16

D_MODEL = 1024
RNN_BLOCKS = 8
RNN_BW = D_MODEL // RNN_BLOCKS
CONV_W = 4
LRU_C = 8.0
N_HEADS = 8
HEAD_DIM = 128
IDX_HEADS = 16
IDX_DIM = 64
TOPK_MAX = 256
N_BUCKETS = 32
MAX_DIST = 128
N_EXPERTS = 64
TOP_K = 8
N_GROUPS = 8
TOPK_GROUPS = 4
D_EXPERT = 256
ROUTED_SCALE = 2.5
N_MOD = 6
EPS = 1e-6

LANES = 128
SUBLANES = 8
VMEM_LIMIT_BYTES = 58 * 1024 * 1024

ROW_TILE = 256
SCAN_TILE = 512
Q_TILE = 256
KEY_SUB = 128
KEY_GROUP = 256
DEN_ROWS = 16
MOE_TILE = 512
ROUTER_TILE = 1024

NEG_INF = float("-inf")
LOG2E = math.log2(math.e)


def _cparams(*sem):
    return pltpu.CompilerParams(dimension_semantics=sem, vmem_limit_bytes=VMEM_LIMIT_BYTES)


def _bdot(a, b):
    return jnp.dot(a.astype(BF16), b.astype(BF16), preferred_element_type=F32)


def _bdot_nt(a, b):
    return lax.dot_general(a.astype(BF16), b.astype(BF16), (((1,), (1,)), ((), ())),
                           preferred_element_type=F32)


def _rms(x, g):
    ms = jnp.mean(x * x, axis=-1, keepdims=True)
    return x * lax.rsqrt(ms + EPS) * g


def _ada_kernel(c_ref, w_ref, b_ref, o_ref):
    c = c_ref[...]
    cond = c * jax.nn.sigmoid(c)
    o_ref[...] = _bdot(cond, w_ref[...]) + b_ref[...]


def _ada(c, w_ada, b_ada):
    bsz, d = c.shape
    n = w_ada.shape[1]
    tn = 1024
    return pl.pallas_call(
        _ada_kernel,
        grid=(n // tn,),
        in_specs=[pl.BlockSpec((bsz, d), lambda j: (0, 0)),
                  pl.BlockSpec((d, tn), lambda j: (0, j)),
                  pl.BlockSpec((1, tn), lambda j: (0, j))],
        out_specs=pl.BlockSpec((bsz, tn), lambda j: (0, j)),
        out_shape=jax.ShapeDtypeStruct((bsz, n), F32),
        compiler_params=_cparams("arbitrary"),
        name="ada",
    )(c, w_ada, b_ada.reshape(1, n))


def _inproj_kernel(x_ref, mod_ref, g_ref, wn_ref, wki_ref, wt_ref, wwi_ref,
                   urnn_ref, ugate_ref, k_ref, glr_ref, gla_ref, ki_ref,
                   qt_ref, vt_ref, qit_ref, wit_ref):
    d = D_MODEL
    x = x_ref[0]
    h = _rms(x, g_ref[...]) * (1.0 + mod_ref[0, 1:2, :]) + mod_ref[0, 0:1, :]
    hb = h.astype(BF16)
    urnn_ref[0] = jnp.dot(hb, wn_ref[:, 0 * d:1 * d], preferred_element_type=F32)
    ugate_ref[0] = jnp.dot(hb, wn_ref[:, 1 * d:2 * d], preferred_element_type=F32)
    k_ref[0] = jnp.dot(hb, wn_ref[:, 2 * d:3 * d], preferred_element_type=F32).astype(BF16)
    glr_ref[0] = jnp.dot(hb, wn_ref[:, 3 * d:4 * d], preferred_element_type=F32)
    gla_ref[0] = jnp.dot(hb, wn_ref[:, 4 * d:5 * d], preferred_element_type=F32)
    ki_ref[0] = jnp.dot(hb, wki_ref[...], preferred_element_type=F32).astype(BF16)
    nt = (((1,), (1,)), ((), ()))
    qt_ref[0] = (lax.dot_general(wt_ref[0 * d:1 * d, :], hb, nt, preferred_element_type=F32)
                 * (HEAD_DIM ** -0.5 * LOG2E)).astype(BF16)
    vt_ref[0] = lax.dot_general(wt_ref[1 * d:2 * d, :], hb, nt, preferred_element_type=F32).astype(BF16)
    qit_ref[0] = lax.dot_general(wt_ref[2 * d:3 * d, :], hb, nt, preferred_element_type=F32).astype(BF16)
    wit_ref[0] = lax.dot_general(wwi_ref[...], hb, nt, preferred_element_type=F32)


def _in_proj(x, mod3, g0, w_in):
    bsz, s, d = x.shape
    tm = ROW_TILE
    offs = np.cumsum([0, d, d, d, d, d, IDX_HEADS * IDX_DIM, IDX_DIM, IDX_HEADS, d, d])
    seg = lambda i: w_in[:, int(offs[i]):int(offs[i + 1])]
    wn = jnp.concatenate([seg(0), seg(1), seg(3), seg(8), seg(9)], axis=1).astype(BF16)
    wki = seg(6).astype(BF16)
    wt = jnp.concatenate([seg(2), seg(4), seg(5)], axis=1).T.astype(BF16)
    wwi = seg(7).T.astype(BF16)
    const = lambda shape: pl.BlockSpec(shape, lambda b, i: (0,) * len(shape),
                                       pipeline_mode=pl.Buffered(1))
    row = lambda w: pl.BlockSpec((1, tm, w), lambda b, i: (b, i, 0))
    col = lambda r: pl.BlockSpec((1, r, tm), lambda b, i: (b, 0, i))
    out_shape = (
        jax.ShapeDtypeStruct((bsz, s, d), F32),
        jax.ShapeDtypeStruct((bsz, s, d), F32),
        jax.ShapeDtypeStruct((bsz, s, d), BF16),
        jax.ShapeDtypeStruct((bsz, s, d), F32),
        jax.ShapeDtypeStruct((bsz, s, d), F32),
        jax.ShapeDtypeStruct((bsz, s, IDX_DIM), BF16),
        jax.ShapeDtypeStruct((bsz, d, s), BF16),
        jax.ShapeDtypeStruct((bsz, d, s), BF16),
        jax.ShapeDtypeStruct((bsz, d, s), BF16),
        jax.ShapeDtypeStruct((bsz, IDX_HEADS, s), F32),
    )
    return pl.pallas_call(
        _inproj_kernel,
        grid=(bsz, s // tm),
        in_specs=[row(d),
                  pl.BlockSpec((1, N_MOD, d), lambda b, i: (b, 0, 0)),
                  pl.BlockSpec((1, d), lambda b, i: (0, 0)),
                  const(wn.shape), const(wki.shape), const(wt.shape), const(wwi.shape)],
        out_specs=(row(d), row(d), row(d), row(d), row(d), row(IDX_DIM),
                   col(d), col(d), col(d), col(IDX_HEADS)),
        out_shape=out_shape,
        compiler_params=_cparams("arbitrary", "arbitrary"),
        name="in_proj",
    )(x, mod3, g0, wn, wki, wt, wwi)


def _gelu_tanh(x):
    return 0.5 * x * (1.0 + jnp.tanh(math.sqrt(2.0 / math.pi) * (x + 0.044715 * (x * x * x))))


def _rglru_kernel(u_ref, ug_ref, cw_ref, cb_ref, wax_ref, ba_ref, bx_ref, lam_ref, y_ref,
                  ext_ref, a_ref, b_ref, carry_ref):
    ts = u_ref.shape[1]
    d = D_MODEL

    @pl.when(pl.program_id(1) == 0)
    def _():
        ext_ref[0:SUBLANES, :] = jnp.zeros((SUBLANES, d), F32)
        carry_ref[...] = jnp.zeros_like(carry_ref)

    ext_ref[SUBLANES:SUBLANES + ts, :] = u_ref[0]
    xc = cb_ref[...] + cw_ref[CONV_W - 1:CONV_W, :] * ext_ref[SUBLANES:SUBLANES + ts, :]
    for k in range(CONV_W - 1):
        off = SUBLANES - (CONV_W - 1) + k
        xc = xc + cw_ref[k:k + 1, :] * ext_ref[off:off + ts, :]
    ext_ref[0:SUBLANES, :] = ext_ref[ts:ts + SUBLANES, :]

    nl = -lam_ref[...]
    sp = jnp.maximum(nl, 0.0) + jnp.log1p(jnp.exp(-jnp.abs(nl)))
    for n in range(RNN_BLOCKS):
        cs = slice(n * RNN_BW, (n + 1) * RNN_BW)
        xb = xc[:, cs]
        g = _bdot(xb, wax_ref[n])
        r = jax.nn.sigmoid(g[:, :RNN_BW] + ba_ref[:, cs])
        i = jax.nn.sigmoid(g[:, RNN_BW:] + bx_ref[:, cs])
        log_a = (-LRU_C) * r * sp[:, cs]
        a_ref[:, cs] = jnp.exp(log_a)
        th = jnp.tanh(log_a)
        b_ref[:, cs] = jnp.sqrt(-2.0 * th / (1.0 - th)) * (i * xb)

    row = lax.broadcasted_iota(jnp.int32, (SUBLANES, d), 0)

    def group(gi, hprev):
        r0 = pl.multiple_of(gi * SUBLANES, SUBLANES)
        a = a_ref[pl.ds(r0, SUBLANES), :]
        b = b_ref[pl.ds(r0, SUBLANES), :]
        for sh in (1, 2, 4):
            keep = row >= sh
            a_s = jnp.where(keep, pltpu.roll(a, sh, 0), 1.0)
            b_s = jnp.where(keep, pltpu.roll(b, sh, 0), 0.0)
            b = a * b_s + b
            a = a * a_s
        h = b + a * hprev
        b_ref[pl.ds(r0, SUBLANES), :] = h
        return jnp.broadcast_to(h[SUBLANES - 1:SUBLANES, :], (SUBLANES, d))

    carry_ref[...] = lax.fori_loop(0, ts // SUBLANES, group, carry_ref[...])
    y_ref[0] = (b_ref[...] * _gelu_tanh(ug_ref[0])).astype(BF16)


def _rglru(u_rnn, u_gate, conv_w, conv_b, w_rg_a, b_rg_a, w_rg_x, b_rg_x, lam):
    bsz, s, d = u_rnn.shape
    ts = SCAN_TILE
    wax = jnp.concatenate([w_rg_a, w_rg_x], axis=-1).astype(BF16)
    vec = lambda v: v.reshape(1, d)
    c2 = lambda shape: pl.BlockSpec(shape, lambda b, i: (0,) * len(shape))
    row = pl.BlockSpec((1, ts, d), lambda b, i: (b, i, 0))
    return pl.pallas_call(
        _rglru_kernel,
        grid=(bsz, s // ts),
        in_specs=[row, row, c2((CONV_W, d)), c2((1, d)), c2(wax.shape), c2((1, d)), c2((1, d)),
                  c2((1, d))],
        out_specs=row,
        out_shape=jax.ShapeDtypeStruct((bsz, s, d), BF16),
        scratch_shapes=[pltpu.VMEM((ts + SUBLANES, d), F32), pltpu.VMEM((ts, d), F32),
                        pltpu.VMEM((ts, d), F32), pltpu.VMEM((SUBLANES, d), F32)],
        compiler_params=_cparams("arbitrary", "arbitrary"),
        name="rglru",
    )(u_rnn, u_gate, conv_w, vec(conv_b), wax, vec(b_rg_a), vec(b_rg_x), vec(lam))


def _t5_bucket_np(dist):
    max_exact = N_BUCKETS // 2
    dd = np.maximum(dist, 0)
    df = np.maximum(dd, 1).astype(np.float32)
    large = max_exact + (np.log(df / np.float32(max_exact)) / np.float32(math.log(MAX_DIST / max_exact))
                         * np.float32(N_BUCKETS - max_exact)).astype(np.int32)
    large = np.minimum(large, N_BUCKETS - 1)
    return np.where(dd < max_exact, dd, large)


def _near_bucket_ids():
    r = np.arange(KEY_GROUP)[None, :, None]
    c = np.arange(Q_TILE)[None, None, :]
    o = np.arange(2)[:, None, None]
    return _t5_bucket_np(c - r - (o - 1) * KEY_GROUP).astype(np.int32)


def _attn_kernel(rb_ref, bkt_ref, qt_ref, qit_ref, wit_ref, k_ref, vt_ref, ki_ref, y_ref,
                 s_ref, tab_ref, acc_ref, m_ref, lg_ref):
    tq = Q_TILE
    ks = KEY_SUB
    kg = KEY_GROUP
    jq = pl.program_id(1)
    t0 = jq * tq
    ngrp = jq + 1
    lane_t = t0 + lax.broadcasted_iota(jnp.int32, (1, tq), 1)

    @pl.when((pl.program_id(0) == 0) & (jq == 0))
    def _():
        for o in range(2):
            for h in range(N_HEADS):
                tab_ref[h, o] = jnp.zeros((kg, tq), F32)

            def fill(b, c):
                hit = bkt_ref[o] == b
                for h in range(N_HEADS):
                    val = (rb_ref[b, h] - rb_ref[N_BUCKETS - 1, h]) * LOG2E
                    tab_ref[h, o] = jnp.where(hit, val, tab_ref[h, o])
                return c

            lax.fori_loop(0, N_BUCKETS - 1, fill, 0)

    wi = wit_ref[0] * (IDX_HEADS ** -0.5 * IDX_DIM ** -0.5)

    def score_sub(i, mnmx, masked):
        r0 = pl.multiple_of(i * ks, ks)
        kic = ki_ref[0, pl.ds(r0, ks), :]
        acc = jnp.zeros((ks, tq), F32)
        for h in range(IDX_HEADS):
            dts = jnp.dot(kic, qit_ref[0, h * IDX_DIM:(h + 1) * IDX_DIM, :],
                          preferred_element_type=F32)
            acc = acc + jnp.maximum(dts, 0.0) * wi[h:h + 1, :]
        lo_src = acc
        if masked:
            key_s = r0 + lax.broadcasted_iota(jnp.int32, (ks, tq), 0)
            causal = key_s <= lane_t
            acc = jnp.where(causal, acc, NEG_INF)
            lo_src = jnp.where(causal, acc, jnp.inf)
        s_ref[pl.ds(r0, ks), :] = acc
        mn, mx = mnmx
        mn = jnp.minimum(mn, jnp.min(lo_src.reshape(ks // SUBLANES, SUBLANES, tq), axis=0))
        mx = jnp.maximum(mx, jnp.max(acc.reshape(ks // SUBLANES, SUBLANES, tq), axis=0))
        return mn, mx

    mnmx = (jnp.full((SUBLANES, tq), jnp.inf, F32), jnp.full((SUBLANES, tq), NEG_INF, F32))
    mnmx = lax.fori_loop(0, 2 * ngrp - 2, lambda i, c: score_sub(i, c, False), mnmx)
    mnmx = score_sub(2 * ngrp - 2, mnmx, True)
    mn8, mx8 = score_sub(2 * ngrp - 1, mnmx, True)
    smin = jnp.min(mn8, axis=0, keepdims=True)
    smax = jnp.max(mx8, axis=0, keepdims=True)

    n_causal = (lane_t + 1).astype(F32)
    k_eff = jnp.minimum(n_causal, float(TOPK_MAX))

    def count_rows(pred):
        part = kg // 8

        def body(g, c):
            r0 = pl.multiple_of(g * kg, kg)
            key_s = (r0 + lax.broadcasted_iota(jnp.int32, (kg, tq), 0)).astype(F32)
            ind = jnp.where(pred(s_ref[pl.ds(r0, kg), :], key_s), 1.0, 0.0)
            return c + jnp.sum(ind.reshape(8, part, tq), axis=0)

        c = lax.fori_loop(0, ngrp, body, jnp.zeros((part, tq), F32))
        return jnp.sum(c, axis=0, keepdims=True)

    def bis_body(st):
        it, lo, hi, c_lo, done, _ = st
        first = (jnp.zeros((1, tq), F32) + jnp.where(it == 0, 1.0, 0.0)) > 0.0
        probe = jnp.where(first, smax, 0.5 * lo + 0.5 * hi)
        collapsed = ~first & ((probe <= lo) | (probe >= hi))
        cnt = count_rows(lambda blk, _: blk >= probe)
        ge = cnt >= k_eff
        upd = (done == 0.0) & ~collapsed
        lo_n = jnp.where(upd & ge, probe, lo)
        c_lo_n = jnp.where(upd & ge, cnt, c_lo)
        hi_n = jnp.where(upd & ~ge, probe, hi)
        fin = collapsed | (cnt == k_eff) | (first & ge)
        done_n = jnp.where(fin, 1.0, done)
        return it + 1, lo_n, hi_n, c_lo_n, done_n, jnp.sum(1.0 - done_n)

    done0 = jnp.where(n_causal <= k_eff, 1.0, 0.0)
    _, thr, _, c_thr, _, _ = lax.while_loop(
        lambda st: st[5] > 0.0, bis_body,
        (jnp.int32(0), smin, smax, n_causal, done0, jnp.sum(1.0 - done0)))

    tie_all = jnp.zeros((1, tq), F32) + (t0 + tq).astype(F32)

    def tie_limit():
        need = k_eff - count_rows(lambda blk, _: blk > thr)

        def body(_, st):
            lo, hi = st
            mid = jnp.floor(0.5 * (lo + hi))
            ok = count_rows(lambda blk, key_s: (blk == thr) & (key_s < mid)) >= need
            return jnp.where(ok, lo, mid), jnp.where(ok, mid, hi)

        n_steps = int(math.ceil(math.log2(s_ref.shape[0]))) + 1
        return lax.fori_loop(0, n_steps, body, (jnp.zeros((1, tq), F32), tie_all))[1]

    excess = jnp.sum(jnp.where(c_thr > k_eff, 1.0, 0.0))
    tie_lim = lax.cond(excess > 0.0, tie_limit, lambda: tie_all)

    def mask_body(g, c):
        r0 = pl.multiple_of(g * kg, kg)
        blk = s_ref[pl.ds(r0, kg), :]
        key_s = (r0 + lax.broadcasted_iota(jnp.int32, (kg, tq), 0)).astype(F32)
        sel = (blk > thr) | ((blk == thr) & (key_s < tie_lim))
        s_ref[pl.ds(r0, kg), :] = jnp.where(sel, 0.0, NEG_INF)
        return c

    lax.fori_loop(0, ngrp, mask_body, 0)

    m_ref[...] = jnp.full_like(m_ref, NEG_INF)
    acc_ref[...] = jnp.zeros_like(acc_ref)
    ones_rows = jnp.ones((DEN_ROWS, kg), BF16)

    def attend(g, near):
        r0 = pl.multiple_of(g * kg, kg)
        msk = s_ref[pl.ds(r0, kg), :]
        col_max = []
        for h in range(N_HEADS):
            hs = slice(h * HEAD_DIM, (h + 1) * HEAD_DIM)
            kh = k_ref[0, pl.ds(r0, kg), hs]
            lg = jnp.dot(kh, qt_ref[0, hs, :], preferred_element_type=F32) + msk
            if near is not None:
                lg = lg + tab_ref[h, near]
            lg_ref[h] = lg
            col_max.append(jnp.max(lg, axis=0, keepdims=True))
        for h in range(N_HEADS):
            hs = slice(h * HEAD_DIM, (h + 1) * HEAD_DIM)
            m_old = m_ref[h:h + 1, :]
            m_new = jnp.maximum(m_old, col_max[h])
            m_safe = jnp.where(m_new == NEG_INF, 0.0, m_new)
            p = jnp.exp2(lg_ref[h] - m_safe).astype(BF16)
            alpha = jnp.exp2(m_old - m_safe)
            m_ref[h:h + 1, :] = m_new
            vh = jnp.concatenate([vt_ref[0, hs, pl.ds(r0, kg)], ones_rows], axis=0)
            acc_ref[h] = alpha * acc_ref[h] + jnp.dot(vh, p, preferred_element_type=F32)

    def _attend_body(g, c):
        attend(g, None)
        return c

    lax.fori_loop(0, jnp.maximum(ngrp - 2, 0), _attend_body, 0)

    @pl.when(jq > 0)
    def _():
        attend(ngrp - 2, 0)

    attend(ngrp - 1, 1)

    for h in range(N_HEADS):
        o = acc_ref[h, 0:HEAD_DIM, :] / acc_ref[h, HEAD_DIM:HEAD_DIM + 1, :]
        y_ref[0, :, h * HEAD_DIM:(h + 1) * HEAD_DIM] = o.T.astype(BF16)


def _attention(qt, qit, wit, k, vt, ki, rel_bias):
    bsz, d, s = qt.shape
    tq = Q_TILE
    bkt = jnp.asarray(_near_bucket_ids())
    once = lambda shape: pl.BlockSpec(shape, lambda b, j: (b,) + (0,) * (len(shape) - 1),
                                      pipeline_mode=pl.Buffered(1))
    col = lambda r: pl.BlockSpec((1, r, tq), lambda b, j: (b, 0, j))
    return pl.pallas_call(
        _attn_kernel,
        grid=(bsz, s // tq),
        in_specs=[pl.BlockSpec(memory_space=pltpu.SMEM),
                  pl.BlockSpec(bkt.shape, lambda b, j: (0, 0, 0)),
                  col(d), col(d), col(IDX_HEADS),
                  once((1, s, d)), once((1, d, s)), once((1, s, IDX_DIM))],
        out_specs=pl.BlockSpec((1, tq, d), lambda b, j: (b, j, 0)),
        out_shape=jax.ShapeDtypeStruct((bsz, s, d), BF16),
        scratch_shapes=[pltpu.VMEM((s, tq), F32),
                        pltpu.VMEM((N_HEADS, 2, KEY_GROUP, tq), F32),
                        pltpu.VMEM((N_HEADS, HEAD_DIM + DEN_ROWS, tq), F32),
                        pltpu.VMEM((N_HEADS, tq), F32),
                        pltpu.VMEM((N_HEADS, KEY_GROUP, tq), F32)],
        compiler_params=_cparams("arbitrary", "arbitrary"),
        name="attn",
    )(rel_bias, bkt, qt, qit, wit, k, vt, ki)


def _merge_kernel(yr_ref, ya_ref, glr_ref, gla_ref, x_ref, mod_ref, g_ref,
                  wr_ref, wa_ref, wo_ref, x1_ref, h2_ref):
    merged = (jax.nn.sigmoid(glr_ref[0]) * jnp.dot(yr_ref[0], wr_ref[...], preferred_element_type=F32)
              + jax.nn.sigmoid(gla_ref[0]) * jnp.dot(ya_ref[0], wa_ref[...], preferred_element_type=F32))
    y = _bdot(merged, wo_ref[...])
    x1 = x_ref[0] + mod_ref[0, 2:3, :] * _rms(y, g_ref[1:2, :])
    x1_ref[0] = x1
    h2 = _rms(x1, g_ref[2:3, :]) * (1.0 + mod_ref[0, 4:5, :]) + mod_ref[0, 3:4, :]
    h2_ref[0] = h2.astype(BF16)


def _merge(y_rnn, y_attn, gl_rnn, gl_attn, x, mod3, gains, w_br_rnn, w_br_attn, w_out):
    bsz, s, d = x.shape
    tm = ROW_TILE
    row = pl.BlockSpec((1, tm, d), lambda b, i: (b, i, 0))
    c2 = lambda shape: pl.BlockSpec(shape, lambda b, i: (0,) * len(shape))
    return pl.pallas_call(
        _merge_kernel,
        grid=(bsz, s // tm),
        in_specs=[row, row, row, row, row,
                  pl.BlockSpec((1, N_MOD, d), lambda b, i: (b, 0, 0)),
                  c2(gains.shape), c2((d, d)), c2((d, d)), c2((d, d))],
        out_specs=(row, row),
        out_shape=(jax.ShapeDtypeStruct((bsz, s, d), F32), jax.ShapeDtypeStruct((bsz, s, d), BF16)),
        compiler_params=_cparams("arbitrary", "arbitrary"),
        name="merge",
    )(y_rnn, y_attn, gl_rnn, gl_attn, x, mod3, gains,
      w_br_rnn.astype(BF16), w_br_attn.astype(BF16), w_out.astype(BF16))


def _router_kernel(h_ref, wr_ref, rb_ref, o_ref):
    t = h_ref.shape[0]
    gsz = N_EXPERTS // N_GROUPS
    s = jax.nn.sigmoid(_bdot_nt(wr_ref[...], h_ref[...]))
    s_sel = s + rb_ref[...]
    g3 = s_sel.reshape(N_GROUPS, gsz, t)
    e_in_g = lax.broadcasted_iota(jnp.int32, (N_GROUPS, gsz, t), 1)
    top1 = jnp.max(g3, axis=1, keepdims=True)
    first = jnp.min(jnp.where(g3 == top1, e_in_g, gsz), axis=1, keepdims=True)
    top2 = jnp.max(jnp.where(e_in_g == first, NEG_INF, g3), axis=1, keepdims=True)
    gscore = jnp.broadcast_to(top1 + top2, (N_GROUPS, gsz, t))
    gi = lax.broadcasted_iota(jnp.int32, (N_GROUPS, gsz, t), 0)
    gmask = jnp.zeros((N_GROUPS, gsz, t), F32)
    for _ in range(TOPK_GROUPS):
        mx = jnp.max(gscore, axis=0, keepdims=True)
        pick = jnp.min(jnp.where(gscore == mx, gi, N_GROUPS), axis=0, keepdims=True)
        hit = gi == pick
        gmask = jnp.where(hit, 1.0, gmask)
        gscore = jnp.where(hit, NEG_INF, gscore)
    cand = jnp.where(gmask.reshape(N_EXPERTS, t) > 0.0, s_sel, NEG_INF)
    ei = lax.broadcasted_iota(jnp.int32, (N_EXPERTS, t), 0)
    sel = jnp.zeros((N_EXPERTS, t), F32)
    for _ in range(TOP_K):
        mx = jnp.max(cand, axis=0, keepdims=True)
        pick = jnp.min(jnp.where(cand == mx, ei, N_EXPERTS), axis=0, keepdims=True)
        hit = ei == pick
        sel = jnp.where(hit, 1.0, sel)
        cand = jnp.where(hit, NEG_INF, cand)
    w = s * sel
    w = w / jnp.sum(w, axis=0, keepdims=True) * ROUTED_SCALE
    o_ref[...] = w.T


def _router(h2, w_router, router_bias):
    n, d = h2.shape
    t = ROUTER_TILE
    return pl.pallas_call(
        _router_kernel,
        grid=(n // t,),
        in_specs=[pl.BlockSpec((t, d), lambda i: (i, 0)),
                  pl.BlockSpec((N_EXPERTS, d), lambda i: (0, 0)),
                  pl.BlockSpec((N_EXPERTS, 1), lambda i: (0, 0))],
        out_specs=pl.BlockSpec((t, N_EXPERTS), lambda i: (i, 0)),
        out_shape=jax.ShapeDtypeStruct((n, N_EXPERTS), F32),
        compiler_params=_cparams("arbitrary"),
        name="router",
    )(h2, w_router.T.astype(BF16), router_bias.reshape(N_EXPERTS, 1))


def _moe_kernel(h_ref, w_ref, wgu_ref, wd_ref, x1_ref, mod_ref, g_ref, o_ref, acc_ref):
    e = pl.program_id(1)
    n_e = pl.num_programs(1)

    @pl.when(e == 0)
    def _():
        acc_ref[...] = jnp.zeros_like(acc_ref)

    gu = jnp.dot(h_ref[...], wgu_ref[0], preferred_element_type=F32)
    g = gu[:, :D_EXPERT]
    a = (g * jax.nn.sigmoid(g)) * gu[:, D_EXPERT:]
    y = _bdot(a, wd_ref[0])
    lane = lax.broadcasted_iota(jnp.int32, w_ref.shape, 1)
    wcol = jnp.sum(jnp.where(lane == e, w_ref[...], 0.0), axis=1, keepdims=True)
    wcol = jnp.where(e == n_e - 1, 1.0, wcol)
    acc_ref[...] += y * wcol

    @pl.when(e == n_e - 1)
    def _():
        o_ref[...] = x1_ref[...] + mod_ref[0, 5:6, :] * _rms(acc_ref[...], g_ref[...])


def _moe(h2, wdense, wgu, wd, x1, mod3, g3, seq):
    n, d = h2.shape
    t = MOE_TILE
    n_e = wgu.shape[0]
    tiles_per_seq = seq // t
    tok = lambda w: pl.BlockSpec((t, w), lambda i, e: (i, 0))
    return pl.pallas_call(
        _moe_kernel,
        grid=(n // t, n_e),
        in_specs=[tok(d), tok(N_EXPERTS),
                  pl.BlockSpec((1, d, 2 * D_EXPERT), lambda i, e: (e, 0, 0)),
                  pl.BlockSpec((1, D_EXPERT, d), lambda i, e: (e, 0, 0)),
                  tok(d),
                  pl.BlockSpec((1, N_MOD, d), lambda i, e: (i // tiles_per_seq, 0, 0)),
                  pl.BlockSpec((1, d), lambda i, e: (0, 0))],
        out_specs=tok(d),
        out_shape=jax.ShapeDtypeStruct((n, d), F32),
        scratch_shapes=[pltpu.VMEM((t, d), F32)],
        compiler_params=_cparams("arbitrary", "arbitrary"),
        name="moe",
    )(h2, wdense, wgu, wd, x1, mod3, g3)


def kernel(x, c, w_ada, b_ada, norm_gain, w_in, conv_w, conv_b, w_rg_a, b_rg_a, w_rg_x, b_rg_x,
           lru_lambda, w_br_rnn, w_br_attn, w_out, rel_bias, w_router, router_bias,
           w_exp_gate, w_exp_up, w_exp_down, w_sh_gate, w_sh_up, w_sh_down):
    bsz, s, d = x.shape
    depth = w_ada.shape[0]
    for l in range(depth):
        mod = _ada(c, w_ada[l], b_ada[l])
        mod3 = mod.reshape(bsz, N_MOD, d)
        gains = norm_gain[l]
        (u_rnn, u_gate, k, gl_rnn, gl_attn, ki, qt, vt, qit, wit) = _in_proj(x, mod3, gains[0:1], w_in[l])
        y_rnn = _rglru(u_rnn, u_gate, conv_w[l], conv_b[l], w_rg_a[l], b_rg_a[l], w_rg_x[l],
                       b_rg_x[l], lru_lambda[l])
        y_attn = _attention(qt, qit, wit, k, vt, ki, rel_bias)
        x1, h2 = _merge(y_rnn, y_attn, gl_rnn, gl_attn, x, mod3, gains, w_br_rnn[l], w_br_attn[l],
                        w_out[l])
        h2f = h2.reshape(bsz * s, d)
        wdense = _router(h2f, w_router[l], router_bias[l])
        wgu = jnp.concatenate(
            [jnp.concatenate([w_exp_gate[l], w_exp_up[l]], axis=-1),
             jnp.concatenate([w_sh_gate[l], w_sh_up[l]], axis=-1)[None]], axis=0).astype(BF16)
        wd = jnp.concatenate([w_exp_down[l], w_sh_down[l][None]], axis=0).astype(BF16)
        x = _moe(h2f, wdense, wgu, wd, x1.reshape(bsz * s, d), mod3, gains[3:4], s).reshape(bsz, s, d)
    return x
```

```python
import functools
import math

import jax
import jax.numpy as jnp
import numpy as np
from jax import lax
from jax.experimental import pallas as pl
from jax.experimental.pallas import tpu as pltpu

F32 = jnp.float32
BF16 = jnp.bfloat16

D_MODEL = 1024
RNN_BLOCKS = 8
RNN_BW = D_MODEL // RNN_BLOCKS
CONV_W = 4
LRU_C = 8.0
N_HEADS = 8
HEAD_DIM = 128
IDX_HEADS = 16
IDX_DIM = 64
TOPK_MAX = 256
N_BUCKETS = 32
MAX_DIST = 128
N_EXPERTS = 64
TOP_K = 8
N_GROUPS = 8
TOPK_GROUPS = 4
D_EXPERT = 256
ROUTED_SCALE = 2.5
N_MOD = 6
EPS = 1e-6

LANES = 128
SUBLANES = 8
VMEM_LIMIT_BYTES = 58 * 1024 * 1024

ROW_TILE = 256
SCAN_TILE = 512
Q_TILE = 256
KEY_SUB = 128
KEY_GROUP = 256
DEN_ROWS = 16
MOE_TILE = 1024
EXPERTS_PER_STEP = 2
FFN_ROWS = 128

NEG_INF = float("-inf")
LOG2E = math.log2(math.e)


def _cparams(*sem):
    return pltpu.CompilerParams(dimension_semantics=sem, vmem_limit_bytes=VMEM_LIMIT_BYTES)


def _bdot(a, b):
    return jnp.dot(a.astype(BF16), b.astype(BF16), preferred_element_type=F32)


def _bdot_nt(a, b):
    return lax.dot_general(a.astype(BF16), b.astype(BF16), (((1,), (1,)), ((), ())),
                           preferred_element_type=F32)


def _rms(x, g):
    ms = jnp.mean(x * x, axis=-1, keepdims=True)
    return x * lax.rsqrt(ms + EPS) * g


def _ada_kernel(c_ref, w_ref, b_ref, o_ref):
    c = c_ref[...]
    cond = c * jax.nn.sigmoid(c)
    o_ref[...] = _bdot(cond, w_ref[...]) + b_ref[...]


def _ada(c, w_ada, b_ada):
    bsz, d = c.shape
    n = w_ada.shape[1]
    tn = 1024
    return pl.pallas_call(
        _ada_kernel,
        grid=(n // tn,),
        in_specs=[pl.BlockSpec((bsz, d), lambda j: (0, 0)),
                  pl.BlockSpec((d, tn), lambda j: (0, j)),
                  pl.BlockSpec((1, tn), lambda j: (0, j))],
        out_specs=pl.BlockSpec((bsz, tn), lambda j: (0, j)),
        out_shape=jax.ShapeDtypeStruct((bsz, n), F32),
        compiler_params=_cparams("arbitrary"),
        name="ada",
    )(c, w_ada, b_ada.reshape(1, n))


def _inproj_kernel(x_ref, mod_ref, g_ref, wn_ref, wki_ref, wt_ref, wwi_ref,
                   urnn_ref, ugate_ref, k_ref, glr_ref, gla_ref, ki_ref,
                   qt_ref, vt_ref, qit_ref, wit_ref):
    d = D_MODEL
    x = x_ref[0]
    h = _rms(x, g_ref[...]) * (1.0 + mod_ref[0, 1:2, :]) + mod_ref[0, 0:1, :]
    hb = h.astype(BF16)
    urnn_ref[0] = jnp.dot(hb, wn_ref[:, 0 * d:1 * d], preferred_element_type=F32)
    ugate_ref[0] = jnp.dot(hb, wn_ref[:, 1 * d:2 * d], preferred_element_type=F32)
    k_ref[0] = jnp.dot(hb, wn_ref[:, 2 * d:3 * d], preferred_element_type=F32).astype(BF16)
    glr_ref[0] = jnp.dot(hb, wn_ref[:, 3 * d:4 * d], preferred_element_type=F32)
    gla_ref[0] = jnp.dot(hb, wn_ref[:, 4 * d:5 * d], preferred_element_type=F32)
    ki_ref[0] = jnp.dot(hb, wki_ref[...], preferred_element_type=F32).astype(BF16)
    nt = (((1,), (1,)), ((), ()))
    qt_ref[0] = (lax.dot_general(wt_ref[0 * d:1 * d, :], hb, nt, preferred_element_type=F32)
                 * (HEAD_DIM ** -0.5 * LOG2E)).astype(BF16)
    vt_ref[0] = lax.dot_general(wt_ref[1 * d:2 * d, :], hb, nt, preferred_element_type=F32).astype(BF16)
    qit_ref[0] = lax.dot_general(wt_ref[2 * d:3 * d, :], hb, nt, preferred_element_type=F32).astype(BF16)
    wit_ref[0] = lax.dot_general(wwi_ref[...], hb, nt, preferred_element_type=F32)


def _in_proj(x, mod3, g0, w_in):
    bsz, s, d = x.shape
    tm = ROW_TILE
    offs = np.cumsum([0, d, d, d, d, d, IDX_HEADS * IDX_DIM, IDX_DIM, IDX_HEADS, d, d])
    seg = lambda i: w_in[:, int(offs[i]):int(offs[i + 1])]
    wn = jnp.concatenate([seg(0), seg(1), seg(3), seg(8), seg(9)], axis=1).astype(BF16)
    wki = seg(6).astype(BF16)
    wt = jnp.concatenate([seg(2), seg(4), seg(5)], axis=1).T.astype(BF16)
    wwi = seg(7).T.astype(BF16)
    const = lambda shape: pl.BlockSpec(shape, lambda b, i: (0,) * len(shape),
                                       pipeline_mode=pl.Buffered(1))
    row = lambda w: pl.BlockSpec((1, tm, w), lambda b, i: (b, i, 0))
    col = lambda r: pl.BlockSpec((1, r, tm), lambda b, i: (b, 0, i))
    out_shape = (
        jax.ShapeDtypeStruct((bsz, s, d), F32),
        jax.ShapeDtypeStruct((bsz, s, d), F32),
        jax.ShapeDtypeStruct((bsz, s, d), BF16),
        jax.ShapeDtypeStruct((bsz, s, d), F32),
        jax.ShapeDtypeStruct((bsz, s, d), F32),
        jax.ShapeDtypeStruct((bsz, s, IDX_DIM), BF16),
        jax.ShapeDtypeStruct((bsz, d, s), BF16),
        jax.ShapeDtypeStruct((bsz, d, s), BF16),
        jax.ShapeDtypeStruct((bsz, d, s), BF16),
        jax.ShapeDtypeStruct((bsz, IDX_HEADS, s), F32),
    )
    return pl.pallas_call(
        _inproj_kernel,
        grid=(bsz, s // tm),
        in_specs=[row(d),
                  pl.BlockSpec((1, N_MOD, d), lambda b, i: (b, 0, 0)),
                  pl.BlockSpec((1, d), lambda b, i: (0, 0)),
                  const(wn.shape), const(wki.shape), const(wt.shape), const(wwi.shape)],
        out_specs=(row(d), row(d), row(d), row(d), row(d), row(IDX_DIM),
                   col(d), col(d), col(d), col(IDX_HEADS)),
        out_shape=out_shape,
        compiler_params=_cparams("arbitrary", "arbitrary"),
        name="in_proj",
    )(x, mod3, g0, wn, wki, wt, wwi)


def _gelu_tanh(x):
    return 0.5 * x * (1.0 + jnp.tanh(math.sqrt(2.0 / math.pi) * (x + 0.044715 * (x * x * x))))


def _rglru_kernel(u_ref, ug_ref, cw_ref, cb_ref, wax_ref, ba_ref, bx_ref, lam_ref, y_ref,
                  ext_ref, a_ref, b_ref, carry_ref):
    ts = u_ref.shape[1]
    d = D_MODEL

    @pl.when(pl.program_id(1) == 0)
    def _():
        ext_ref[0:SUBLANES, :] = jnp.zeros((SUBLANES, d), F32)
        carry_ref[...] = jnp.zeros_like(carry_ref)

    ext_ref[SUBLANES:SUBLANES + ts, :] = u_ref[0]
    xc = cb_ref[...] + cw_ref[CONV_W - 1:CONV_W, :] * ext_ref[SUBLANES:SUBLANES + ts, :]
    for k in range(CONV_W - 1):
        off = SUBLANES - (CONV_W - 1) + k
        xc = xc + cw_ref[k:k + 1, :] * ext_ref[off:off + ts, :]
    ext_ref[0:SUBLANES, :] = ext_ref[ts:ts + SUBLANES, :]

    nl = -lam_ref[...]
    sp = jnp.maximum(nl, 0.0) + jnp.log1p(jnp.exp(-jnp.abs(nl)))
    for n in range(RNN_BLOCKS):
        cs = slice(n * RNN_BW, (n + 1) * RNN_BW)
        xb = xc[:, cs]
        g = _bdot(xb, wax_ref[n])
        r = jax.nn.sigmoid(g[:, :RNN_BW] + ba_ref[:, cs])
        i = jax.nn.sigmoid(g[:, RNN_BW:] + bx_ref[:, cs])
        log_a = (-LRU_C) * r * sp[:, cs]
        a_ref[:, cs] = jnp.exp(log_a)
        th = jnp.tanh(log_a)
        b_ref[:, cs] = jnp.sqrt(-2.0 * th / (1.0 - th)) * (i * xb)

    row = lax.broadcasted_iota(jnp.int32, (SUBLANES, d), 0)

    def group(gi, hprev):
        r0 = pl.multiple_of(gi * SUBLANES, SUBLANES)
        a = a_ref[pl.ds(r0, SUBLANES), :]
        b = b_ref[pl.ds(r0, SUBLANES), :]
        for sh in (1, 2, 4):
            keep = row >= sh
            a_s = jnp.where(keep, pltpu.roll(a, sh, 0), 1.0)
            b_s = jnp.where(keep, pltpu.roll(b, sh, 0), 0.0)
            b = a * b_s + b
            a = a * a_s
        h = b + a * hprev
        b_ref[pl.ds(r0, SUBLANES), :] = h
        return jnp.broadcast_to(h[SUBLANES - 1:SUBLANES, :], (SUBLANES, d))

    carry_ref[...] = lax.fori_loop(0, ts // SUBLANES, group, carry_ref[...])
    y_ref[0] = (b_ref[...] * _gelu_tanh(ug_ref[0])).astype(BF16)


def _rglru(u_rnn, u_gate, conv_w, conv_b, w_rg_a, b_rg_a, w_rg_x, b_rg_x, lam):
    bsz, s, d = u_rnn.shape
    ts = SCAN_TILE
    wax = jnp.concatenate([w_rg_a, w_rg_x], axis=-1).astype(BF16)
    vec = lambda v: v.reshape(1, d)
    c2 = lambda shape: pl.BlockSpec(shape, lambda b, i: (0,) * len(shape))
    row = pl.BlockSpec((1, ts, d), lambda b, i: (b, i, 0))
    return pl.pallas_call(
        _rglru_kernel,
        grid=(bsz, s // ts),
        in_specs=[row, row, c2((CONV_W, d)), c2((1, d)), c2(wax.shape), c2((1, d)), c2((1, d)),
                  c2((1, d))],
        out_specs=row,
        out_shape=jax.ShapeDtypeStruct((bsz, s, d), BF16),
        scratch_shapes=[pltpu.VMEM((ts + SUBLANES, d), F32), pltpu.VMEM((ts, d), F32),
                        pltpu.VMEM((ts, d), F32), pltpu.VMEM((SUBLANES, d), F32)],
        compiler_params=_cparams("arbitrary", "arbitrary"),
        name="rglru",
    )(u_rnn, u_gate, conv_w, vec(conv_b), wax, vec(b_rg_a), vec(b_rg_x), vec(lam))


def _t5_bucket_np(dist):
    max_exact = N_BUCKETS // 2
    dd = np.maximum(dist, 0)
    df = np.maximum(dd, 1).astype(np.float32)
    large = max_exact + (np.log(df / np.float32(max_exact)) / np.float32(math.log(MAX_DIST / max_exact))
                         * np.float32(N_BUCKETS - max_exact)).astype(np.int32)
    large = np.minimum(large, N_BUCKETS - 1)
    return np.where(dd < max_exact, dd, large)


def _near_bucket_ids():
    r = np.arange(KEY_GROUP)[None, :, None]
    c = np.arange(Q_TILE)[None, None, :]
    o = np.arange(2)[:, None, None]
    return _t5_bucket_np(c - r - (o - 1) * KEY_GROUP).astype(np.int32)


def _attn_kernel(rb_ref, bkt_ref, qt_ref, qit_ref, wit_ref, k_ref, vt_ref, ki_ref, y_ref,
                 s_ref, tab_ref, acc_ref, m_ref, lg_ref):
    tq = Q_TILE
    ks = KEY_SUB
    kg = KEY_GROUP
    jq = pl.program_id(1)
    t0 = jq * tq
    ngrp = jq + 1
    lane_t = t0 + lax.broadcasted_iota(jnp.int32, (1, tq), 1)

    @pl.when((pl.program_id(0) == 0) & (jq == 0))
    def _():
        for o in range(2):
            for h in range(N_HEADS):
                tab_ref[h, o] = jnp.zeros((kg, tq), F32)

            def fill(b, c):
                hit = bkt_ref[o] == b
                for h in range(N_HEADS):
                    val = (rb_ref[b, h] - rb_ref[N_BUCKETS - 1, h]) * LOG2E
                    tab_ref[h, o] = jnp.where(hit, val, tab_ref[h, o])
                return c

            lax.fori_loop(0, N_BUCKETS - 1, fill, 0)

    wi = wit_ref[0] * (IDX_HEADS ** -0.5 * IDX_DIM ** -0.5)

    def score_sub(i, mnmx, masked):
        r0 = pl.multiple_of(i * ks, ks)
        kic = ki_ref[0, pl.ds(r0, ks), :]
        acc = jnp.zeros((ks, tq), F32)
        for h in range(IDX_HEADS):
            dts = jnp.dot(kic, qit_ref[0, h * IDX_DIM:(h + 1) * IDX_DIM, :],
                          preferred_element_type=F32)
            acc = acc + jnp.maximum(dts, 0.0) * wi[h:h + 1, :]
        lo_src = acc
        if masked:
            key_s = r0 + lax.broadcasted_iota(jnp.int32, (ks, tq), 0)
            causal = key_s <= lane_t
            acc = jnp.where(causal, acc, NEG_INF)
            lo_src = jnp.where(causal, acc, jnp.inf)
        s_ref[pl.ds(r0, ks), :] = acc
        mn, mx = mnmx
        mn = jnp.minimum(mn, jnp.min(lo_src.reshape(ks // SUBLANES, SUBLANES, tq), axis=0))
        mx = jnp.maximum(mx, jnp.max(acc.reshape(ks // SUBLANES, SUBLANES, tq), axis=0))
        return mn, mx

    mnmx = (jnp.full((SUBLANES, tq), jnp.inf, F32), jnp.full((SUBLANES, tq), NEG_INF, F32))
    mnmx = lax.fori_loop(0, 2 * ngrp - 2, lambda i, c: score_sub(i, c, False), mnmx)
    mnmx = score_sub(2 * ngrp - 2, mnmx, True)
    mn8, mx8 = score_sub(2 * ngrp - 1, mnmx, True)
    smin = jnp.min(mn8, axis=0, keepdims=True)
    smax = jnp.max(mx8, axis=0, keepdims=True)

    n_causal = (lane_t + 1).astype(F32)
    k_eff = jnp.minimum(n_causal, float(TOPK_MAX))

    def count_rows(pred):
        part = kg // 8

        def body(g, c):
            r0 = pl.multiple_of(g * kg, kg)
            key_s = (r0 + lax.broadcasted_iota(jnp.int32, (kg, tq), 0)).astype(F32)
            ind = jnp.where(pred(s_ref[pl.ds(r0, kg), :], key_s), 1.0, 0.0)
            return c + jnp.sum(ind.reshape(8, part, tq), axis=0)

        c = lax.fori_loop(0, ngrp, body, jnp.zeros((part, tq), F32))
        return jnp.sum(c, axis=0, keepdims=True)

    def bis_body(st):
        it, lo, hi, c_lo, done, _ = st
        first = (jnp.zeros((1, tq), F32) + jnp.where(it == 0, 1.0, 0.0)) > 0.0
        probe = jnp.where(first, smax, 0.5 * lo + 0.5 * hi)
        collapsed = ~first & ((probe <= lo) | (probe >= hi))
        cnt = count_rows(lambda blk, _: blk >= probe)
        ge = cnt >= k_eff
        upd = (done == 0.0) & ~collapsed
        lo_n = jnp.where(upd & ge, probe, lo)
        c_lo_n = jnp.where(upd & ge, cnt, c_lo)
        hi_n = jnp.where(upd & ~ge, probe, hi)
        fin = collapsed | (cnt == k_eff) | (first & ge)
        done_n = jnp.where(fin, 1.0, done)
        return it + 1, lo_n, hi_n, c_lo_n, done_n, jnp.sum(1.0 - done_n)

    done0 = jnp.where(n_causal <= k_eff, 1.0, 0.0)
    _, thr, _, c_thr, _, _ = lax.while_loop(
        lambda st: st[5] > 0.0, bis_body,
        (jnp.int32(0), smin, smax, n_causal, done0, jnp.sum(1.0 - done0)))

    tie_all = jnp.zeros((1, tq), F32) + (t0 + tq).astype(F32)

    def tie_limit():
        need = k_eff - count_rows(lambda blk, _: blk > thr)

        def body(_, st):
            lo, hi = st
            mid = jnp.floor(0.5 * (lo + hi))
            ok = count_rows(lambda blk, key_s: (blk == thr) & (key_s < mid)) >= need
            return jnp.where(ok, lo, mid), jnp.where(ok, mid, hi)

        n_steps = int(math.ceil(math.log2(s_ref.shape[0]))) + 1
        return lax.fori_loop(0, n_steps, body, (jnp.zeros((1, tq), F32), tie_all))[1]

    excess = jnp.sum(jnp.where(c_thr > k_eff, 1.0, 0.0))
    tie_lim = lax.cond(excess > 0.0, tie_limit, lambda: tie_all)

    def mask_body(g, c):
        r0 = pl.multiple_of(g * kg, kg)
        blk = s_ref[pl.ds(r0, kg), :]
        key_s = (r0 + lax.broadcasted_iota(jnp.int32, (kg, tq), 0)).astype(F32)
        sel = (blk > thr) | ((blk == thr) & (key_s < tie_lim))
        s_ref[pl.ds(r0, kg), :] = jnp.where(sel, 0.0, NEG_INF)
        return c

    lax.fori_loop(0, ngrp, mask_body, 0)

    m_ref[...] = jnp.full_like(m_ref, NEG_INF)
    acc_ref[...] = jnp.zeros_like(acc_ref)
    ones_rows = jnp.ones((DEN_ROWS, kg), BF16)

    def attend(g, near):
        r0 = pl.multiple_of(g * kg, kg)
        msk = s_ref[pl.ds(r0, kg), :]
        col_max = []
        for h in range(N_HEADS):
            hs = slice(h * HEAD_DIM, (h + 1) * HEAD_DIM)
            kh = k_ref[0, pl.ds(r0, kg), hs]
            lg = jnp.dot(kh, qt_ref[0, hs, :], preferred_element_type=F32) + msk
            if near is not None:
                lg = lg + tab_ref[h, near]
            lg_ref[h] = lg
            col_max.append(jnp.max(lg, axis=0, keepdims=True))
        for h in range(N_HEADS):
            hs = slice(h * HEAD_DIM, (h + 1) * HEAD_DIM)
            m_old = m_ref[h:h + 1, :]
            m_new = jnp.maximum(m_old, col_max[h])
            m_safe = jnp.where(m_new == NEG_INF, 0.0, m_new)
            p = jnp.exp2(lg_ref[h] - m_safe).astype(BF16)
            alpha = jnp.exp2(m_old - m_safe)
            m_ref[h:h + 1, :] = m_new
            vh = jnp.concatenate([vt_ref[0, hs, pl.ds(r0, kg)], ones_rows], axis=0)
            acc_ref[h] = alpha * acc_ref[h] + jnp.dot(vh, p, preferred_element_type=F32)

    def _attend_body(g, c):
        attend(g, None)
        return c

    lax.fori_loop(0, jnp.maximum(ngrp - 2, 0), _attend_body, 0)

    @pl.when(jq > 0)
    def _():
        attend(ngrp - 2, 0)

    attend(ngrp - 1, 1)

    for h in range(N_HEADS):
        o = acc_ref[h, 0:HEAD_DIM, :] / acc_ref[h, HEAD_DIM:HEAD_DIM + 1, :]
        y_ref[0, :, h * HEAD_DIM:(h + 1) * HEAD_DIM] = o.T.astype(BF16)


def _attention(qt, qit, wit, k, vt, ki, rel_bias):
    bsz, d, s = qt.shape
    tq = Q_TILE
    bkt = jnp.asarray(_near_bucket_ids())
    once = lambda shape: pl.BlockSpec(shape, lambda b, j: (b,) + (0,) * (len(shape) - 1),
                                      pipeline_mode=pl.Buffered(1))
    col = lambda r: pl.BlockSpec((1, r, tq), lambda b, j: (b, 0, j))
    return pl.pallas_call(
        _attn_kernel,
        grid=(bsz, s // tq),
        in_specs=[pl.BlockSpec(memory_space=pltpu.SMEM),
                  pl.BlockSpec(bkt.shape, lambda b, j: (0, 0, 0)),
                  col(d), col(d), col(IDX_HEADS),
                  once((1, s, d)), once((1, d, s)), once((1, s, IDX_DIM))],
        out_specs=pl.BlockSpec((1, tq, d), lambda b, j: (b, j, 0)),
        out_shape=jax.ShapeDtypeStruct((bsz, s, d), BF16),
        scratch_shapes=[pltpu.VMEM((s, tq), F32),
                        pltpu.VMEM((N_HEADS, 2, KEY_GROUP, tq), F32),
                        pltpu.VMEM((N_HEADS, HEAD_DIM + DEN_ROWS, tq), F32),
                        pltpu.VMEM((N_HEADS, tq), F32),
                        pltpu.VMEM((N_HEADS, KEY_GROUP, tq), F32)],
        compiler_params=_cparams("arbitrary", "arbitrary"),
        name="attn",
    )(rel_bias, bkt, qt, qit, wit, k, vt, ki)


def _pack_bf16_pair(x):
    c = x.shape[1] // 2
    lo = lax.bitcast_convert_type(x[:, :c].astype(BF16).astype(F32), jnp.uint32) >> 16
    hi = lax.bitcast_convert_type(x[:, c:].astype(BF16).astype(F32), jnp.uint32) & jnp.uint32(0xFFFF0000)
    return lo | hi


def _unpack_bf16_pair(p):
    a = lax.bitcast_convert_type(p << 16, F32)
    b = lax.bitcast_convert_type(p & jnp.uint32(0xFFFF0000), F32)
    return a, b


def _unpack_rows_bf16(p):
    a, b = _unpack_bf16_pair(p)
    return jnp.concatenate([a, b], axis=1).astype(BF16)


def _merge_kernel(yr_ref, ya_ref, glr_ref, gla_ref, x_ref, mod_ref, g_ref,
                  wr_ref, wa_ref, wo_ref, x1_ref, h2p_ref):
    merged = (jax.nn.sigmoid(glr_ref[0]) * jnp.dot(yr_ref[0], wr_ref[...], preferred_element_type=F32)
              + jax.nn.sigmoid(gla_ref[0]) * jnp.dot(ya_ref[0], wa_ref[...], preferred_element_type=F32))
    y = _bdot(merged, wo_ref[...])
    x1 = x_ref[0] + mod_ref[0, 2:3, :] * _rms(y, g_ref[1:2, :])
    x1_ref[0] = x1
    h2 = _rms(x1, g_ref[2:3, :]) * (1.0 + mod_ref[0, 4:5, :]) + mod_ref[0, 3:4, :]
    h2p_ref[0] = _pack_bf16_pair(h2)


def _merge(y_rnn, y_attn, gl_rnn, gl_attn, x, mod3, gains, w_br_rnn, w_br_attn, w_out):
    bsz, s, d = x.shape
    tm = ROW_TILE
    row = pl.BlockSpec((1, tm, d), lambda b, i: (b, i, 0))
    half = pl.BlockSpec((1, tm, d // 2), lambda b, i: (b, i, 0))
    c2 = lambda shape: pl.BlockSpec(shape, lambda b, i: (0,) * len(shape))
    return pl.pallas_call(
        _merge_kernel,
        grid=(bsz, s // tm),
        in_specs=[row, row, row, row, row,
                  pl.BlockSpec((1, N_MOD, d), lambda b, i: (b, 0, 0)),
                  c2(gains.shape), c2((d, d)), c2((d, d)), c2((d, d))],
        out_specs=(row, half),
        out_shape=(jax.ShapeDtypeStruct((bsz, s, d), F32),
                   jax.ShapeDtypeStruct((bsz, s, d // 2), jnp.uint32)),
        compiler_params=_cparams("arbitrary", "arbitrary"),
        name="merge",
    )(y_rnn, y_attn, gl_rnn, gl_attn, x, mod3, gains,
      w_br_rnn.astype(BF16), w_br_attn.astype(BF16), w_out.astype(BF16))


def _router_kernel(hp_ref, wr_ref, rb_ref, dest_ref, wk_ref, seg_ref):
    t = hp_ref.shape[0]
    gsz = N_EXPERTS // N_GROUPS
    h = _unpack_rows_bf16(hp_ref[...])
    s = jax.nn.sigmoid(_bdot_nt(wr_ref[...], h))
    s_sel = s + rb_ref[...]
    g3 = s_sel.reshape(N_GROUPS, gsz, t)
    e_in_g = lax.broadcasted_iota(jnp.int32, (N_GROUPS, gsz, t), 1)
    top1 = jnp.max(g3, axis=1, keepdims=True)
    first = jnp.min(jnp.where(g3 == top1, e_in_g, gsz), axis=1, keepdims=True)
    top2 = jnp.max(jnp.where(e_in_g == first, NEG_INF, g3), axis=1, keepdims=True)
    gscore = jnp.broadcast_to(top1 + top2, (N_GROUPS, gsz, t))
    gi = lax.broadcasted_iota(jnp.int32, (N_GROUPS, gsz, t), 0)
    gmask = jnp.zeros((N_GROUPS, gsz, t), F32)
    for _ in range(TOPK_GROUPS):
        mx = jnp.max(gscore, axis=0, keepdims=True)
        pick = jnp.min(jnp.where(gscore == mx, gi, N_GROUPS), axis=0, keepdims=True)
        hit = gi == pick
        gmask = jnp.where(hit, 1.0, gmask)
        gscore = jnp.where(hit, NEG_INF, gscore)
    cand = jnp.where(gmask.reshape(N_EXPERTS, t) > 0.0, s_sel, NEG_INF)
    ei = lax.broadcasted_iota(jnp.int32, (N_EXPERTS, t), 0)
    sel = jnp.zeros((N_EXPERTS, t), F32)
    picks = []
    for _ in range(TOP_K):
        mx = jnp.max(cand, axis=0, keepdims=True)
        pick = jnp.min(jnp.where(cand == mx, ei, N_EXPERTS), axis=0, keepdims=True)
        hit = ei == pick
        sel = jnp.where(hit, 1.0, sel)
        cand = jnp.where(hit, NEG_INF, cand)
        picks.append(pick)
    w = s * sel
    w = w / jnp.sum(w, axis=0, keepdims=True) * ROUTED_SCALE

    selb = sel.astype(BF16)
    tok_r = lax.broadcasted_iota(jnp.int32, (t, t), 0)
    tok_c = lax.broadcasted_iota(jnp.int32, (t, t), 1)
    rank = jnp.dot(selb, jnp.where(tok_r < tok_c, 1.0, 0.0).astype(BF16), preferred_element_type=F32)
    e_r = lax.broadcasted_iota(jnp.int32, (N_EXPERTS, N_EXPERTS), 0)
    e_c = lax.broadcasted_iota(jnp.int32, (N_EXPERTS, N_EXPERTS), 1)
    cnt_col = jnp.sum(sel, axis=1, keepdims=True)
    pad_col = jnp.floor((cnt_col + (SUBLANES - 1.0)) * (1.0 / SUBLANES))
    off_col = SUBLANES * jnp.dot(jnp.where(e_c < e_r, 1.0, 0.0).astype(BF16),
                                 jnp.broadcast_to(pad_col, (N_EXPERTS, LANES)).astype(BF16),
                                 preferred_element_type=F32)[:, 0:1]
    slot = rank + off_col
    dest_rows, w_rows = [], []
    for k in range(TOP_K):
        hit = ei == picks[k]
        dest_rows.append(jnp.sum(jnp.where(hit, slot, 0.0), axis=0, keepdims=True))
        w_rows.append(jnp.sum(jnp.where(hit, w, 0.0), axis=0, keepdims=True))
    dest_ref[0] = jnp.concatenate(dest_rows, axis=0).astype(jnp.int32)
    wk_ref[...] = jnp.concatenate(w_rows, axis=0)
    cnt_row = _bdot_nt(jnp.ones((SUBLANES, t), BF16), selb)
    pad_row = jnp.floor((cnt_row + (SUBLANES - 1.0)) * (1.0 / SUBLANES))
    off_row = SUBLANES * jnp.dot(pad_row.astype(BF16), jnp.where(e_r < e_c, 1.0, 0.0).astype(BF16),
                                 preferred_element_type=F32)
    seg_ref[0] = jnp.concatenate([off_row[0:1], cnt_row[0:1]], axis=1).astype(jnp.int32)


def _router(h2p, w_router, router_bias):
    n, dh = h2p.shape
    t = MOE_TILE
    nt = n // t
    return pl.pallas_call(
        _router_kernel,
        grid=(nt,),
        in_specs=[pl.BlockSpec((t, dh), lambda i: (i, 0)),
                  pl.BlockSpec((N_EXPERTS, 2 * dh), lambda i: (0, 0)),
                  pl.BlockSpec((N_EXPERTS, 1), lambda i: (0, 0))],
        out_specs=(pl.BlockSpec((1, TOP_K, t), lambda i: (i, 0, 0)),
                   pl.BlockSpec((TOP_K, t), lambda i: (0, i)),
                   pl.BlockSpec((1, 1, 2 * N_EXPERTS), lambda i: (i, 0, 0))),
        out_shape=(jax.ShapeDtypeStruct((nt, TOP_K, t), jnp.int32),
                   jax.ShapeDtypeStruct((TOP_K, n), F32),
                   jax.ShapeDtypeStruct((nt, 1, 2 * N_EXPERTS), jnp.int32)),
        compiler_params=_cparams("arbitrary"),
        name="router",
    )(h2p, w_router.T.astype(BF16), router_bias.reshape(N_EXPERTS, 1))


def _swiglu(x, wgu, wd):
    gu = jnp.dot(x, wgu, preferred_element_type=F32)
    g = gu[:, :D_EXPERT]
    a = (g * jax.nn.sigmoid(g)) * gu[:, D_EXPERT:]
    return jnp.dot(a.astype(BF16), wd, preferred_element_type=F32)


def _moe_kernel(seg_ref, hp_ref, dest_ref, wt_ref, wgu_ref, wd_ref, wsgu_ref, wsd_ref, x1_ref,
                mod_ref, g_ref, o_ref, xs_ref, stage_ref, dest_smem, sem):
    i = pl.program_id(0)
    s = pl.program_id(1)
    n_s = pl.num_programs(1)
    t = MOE_TILE
    half = xs_ref.shape[1]

    @pl.when((i == 0) & (s == 0))
    def _():
        xs_ref[...] = jnp.zeros_like(xs_ref)

    @pl.when(s == 0)
    def _():
        cp = pltpu.make_async_copy(dest_ref.at[0], dest_smem, sem)
        cp.start()
        cp.wait()

        def body(nb, c):
            n0 = nb * SUBLANES
            for j in range(SUBLANES):
                row = hp_ref[pl.ds(n0 + j, 1), :]
                for k in range(TOP_K):
                    xs_ref[pl.ds(dest_smem[k, n0 + j], 1), :] = row
            return c

        lax.fori_loop(0, t // SUBLANES, body, 0)

    @pl.when(s < n_s - 1)
    def _():
        for eb in range(EXPERTS_PER_STEP):
            e = s * EXPERTS_PER_STEP + eb
            off = seg_ref[i, e]
            end = off + seg_ref[i, N_EXPERTS + e]

            def chunk(c, carry):
                r0 = pl.multiple_of(off + c * FFN_ROWS, SUBLANES)
                xp = xs_ref[pl.ds(r0, FFN_ROWS), :]
                y = _swiglu(_unpack_rows_bf16(xp), wgu_ref[eb], wd_ref[eb])
                rows = r0 + lax.broadcasted_iota(jnp.int32, (FFN_ROWS, half), 0)
                xs_ref[pl.ds(r0, FFN_ROWS), :] = jnp.where(rows < end, _pack_bf16_pair(y), xp)
                return carry

            n_chunks = lax.shift_right_logical(end - off + (FFN_ROWS - 1), FFN_ROWS.bit_length() - 1)
            lax.fori_loop(0, n_chunks, chunk, 0)

    @pl.when(s == n_s - 1)
    def _():
        o_ref[...] = _swiglu(_unpack_rows_bf16(hp_ref[...]), wsgu_ref[...], wsd_ref[...])

        def body(nb, c):
            n0 = pl.multiple_of(nb * SUBLANES, SUBLANES)
            for k in range(TOP_K):
                for j in range(SUBLANES):
                    stage_ref[k, pl.ds(j, 1), :] = xs_ref[pl.ds(dest_smem[k, n0 + j], 1), :]
            wrow = wt_ref[pl.ds(n0, SUBLANES), :]
            acc_a = jnp.zeros((SUBLANES, half), F32)
            acc_b = jnp.zeros((SUBLANES, half), F32)
            for k in range(TOP_K):
                a, b = _unpack_bf16_pair(stage_ref[k])
                acc_a = acc_a + a * wrow[:, k:k + 1]
                acc_b = acc_b + b * wrow[:, k:k + 1]
            o_ref[pl.ds(n0, SUBLANES), :] += jnp.concatenate([acc_a, acc_b], axis=1)
            return c

        lax.fori_loop(0, t // SUBLANES, body, 0)
        o_ref[...] = x1_ref[...] + mod_ref[0, 5:6, :] * _rms(o_ref[...], g_ref[...])


def _moe(h2p, dest, wt, seg, wgu, wd, wsgu, wsd, x1, mod3, g3, seq):
    n, dh = h2p.shape
    d = 2 * dh
    t = MOE_TILE
    n_steps = N_EXPERTS // EXPERTS_PER_STEP + 1
    tiles_per_seq = seq // t
    tok = lambda w: pl.BlockSpec((t, w), lambda i, s: (i, 0))
    wblk = lambda r, c: pl.BlockSpec((EXPERTS_PER_STEP, r, c),
                                     lambda i, s: (jnp.minimum(s, n_steps - 2), 0, 0))
    c2 = lambda shape: pl.BlockSpec(shape, lambda i, s: (0,) * len(shape))
    xs_rows = TOP_K * t + N_EXPERTS * SUBLANES + FFN_ROWS
    return pl.pallas_call(
        _moe_kernel,
        grid=(n // t, n_steps),
        in_specs=[pl.BlockSpec(memory_space=pltpu.SMEM),
                  tok(dh),
                  pl.BlockSpec((1, TOP_K, t), lambda i, s: (i, 0, 0)),
                  tok(TOP_K),
                  wblk(d, 2 * D_EXPERT), wblk(D_EXPERT, d),
                  c2((d, 2 * D_EXPERT)), c2((D_EXPERT, d)),
                  tok(d),
                  pl.BlockSpec((1, N_MOD, d), lambda i, s: (i // tiles_per_seq, 0, 0)),
                  c2((1, d))],
        out_specs=tok(d),
        out_shape=jax.ShapeDtypeStruct((n, d), F32),
        scratch_shapes=[pltpu.VMEM((xs_rows, dh), jnp.uint32),
                        pltpu.VMEM((TOP_K, SUBLANES, dh), jnp.uint32),
                        pltpu.SMEM((TOP_K, t), jnp.int32),
                        pltpu.SemaphoreType.DMA(())],
        compiler_params=_cparams("arbitrary", "arbitrary"),
        name="moe",
    )(seg, h2p, dest, wt, wgu, wd, wsgu, wsd, x1, mod3, g3)


def kernel(x, c, w_ada, b_ada, norm_gain, w_in, conv_w, conv_b, w_rg_a, b_rg_a, w_rg_x, b_rg_x,
           lru_lambda, w_br_rnn, w_br_attn, w_out, rel_bias, w_router, router_bias,
           w_exp_gate, w_exp_up, w_exp_down, w_sh_gate, w_sh_up, w_sh_down):
    bsz, s, d = x.shape
    depth = w_ada.shape[0]
    for l in range(depth):
        mod = _ada(c, w_ada[l], b_ada[l])
        mod3 = mod.reshape(bsz, N_MOD, d)
        gains = norm_gain[l]
        (u_rnn, u_gate, k, gl_rnn, gl_attn, ki, qt, vt, qit, wit) = _in_proj(x, mod3, gains[0:1], w_in[l])
        y_rnn = _rglru(u_rnn, u_gate, conv_w[l], conv_b[l], w_rg_a[l], b_rg_a[l], w_rg_x[l],
                       b_rg_x[l], lru_lambda[l])
        y_attn = _attention(qt, qit, wit, k, vt, ki, rel_bias)
        x1, h2p = _merge(y_rnn, y_attn, gl_rnn, gl_attn, x, mod3, gains, w_br_rnn[l], w_br_attn[l],
                         w_out[l])
        h2p = h2p.reshape(bsz * s, d // 2)
        dest, wk, seg = _router(h2p, w_router[l], router_bias[l])
        wgu = jnp.concatenate([w_exp_gate[l], w_exp_up[l]], axis=-1).astype(BF16)
        wsgu = jnp.concatenate([w_sh_gate[l], w_sh_up[l]], axis=-1).astype(BF16)
        x = _moe(h2p, dest, wk.T, seg.reshape(-1, 2 * N_EXPERTS), wgu, w_exp_down[l].astype(BF16),
                 wsgu, w_sh_down[l].astype(BF16), x1.reshape(bsz * s, d), mod3, gains[3:4],
                 s).reshape(bsz, s, d)
    return x
```

```python
import functools
import math

import jax
import jax.numpy as jnp
import numpy as np
from jax import lax
from jax.experimental import pallas as pl
from jax.experimental.pallas import tpu as pltpu

F32 = jnp.float32
BF16 = jnp.bfloat16

D_MODEL = 1024
RNN_BLOCKS = 8
RNN_BW = D_MODEL // RNN_BLOCKS
CONV_W = 4
LRU_C = 8.0
N_HEADS = 8
HEAD_DIM = 128
IDX_HEADS = 16
IDX_DIM = 64
TOPK_MAX = 256
N_BUCKETS = 32
MAX_DIST = 128
N_EXPERTS = 64
TOP_K = 8
N_GROUPS = 8
TOPK_GROUPS = 4
D_EXPERT = 256
ROUTED_SCALE = 2.5
N_MOD = 6
EPS = 1e-6

LANES = 128
SUBLANES = 8
VMEM_LIMIT_BYTES = 58 * 1024 * 1024

ROW_TILE = 256
SCAN_TILE = 512
Q_TILE = 256
KEY_SUB = 128
KEY_GROUP = 256
DEN_ROWS = 16
MOE_TILE = 1024
EXPERTS_PER_STEP = 4
FFN_ROWS = 160
MOVE_TOKENS = 4

NEG_INF = float("-inf")
LOG2E = math.log2(math.e)


def _cparams(*sem):
    return pltpu.CompilerParams(dimension_semantics=sem, vmem_limit_bytes=VMEM_LIMIT_BYTES)


def _bdot(a, b):
    return jnp.dot(a.astype(BF16), b.astype(BF16), preferred_element_type=F32)


def _bdot_nt(a, b):
    return lax.dot_general(a.astype(BF16), b.astype(BF16), (((1,), (1,)), ((), ())),
                           preferred_element_type=F32)


def _rms(x, g):
    ms = jnp.mean(x * x, axis=-1, keepdims=True)
    return x * lax.rsqrt(ms + EPS) * g


def _ada_kernel(c_ref, w_ref, b_ref, o_ref):
    c = c_ref[...]
    cond = c * jax.nn.sigmoid(c)
    o_ref[...] = _bdot(cond, w_ref[...]) + b_ref[...]


def _ada(c, w_ada, b_ada):
    bsz, d = c.shape
    n = w_ada.shape[1]
    tn = 1024
    return pl.pallas_call(
        _ada_kernel,
        grid=(n // tn,),
        in_specs=[pl.BlockSpec((bsz, d), lambda j: (0, 0)),
                  pl.BlockSpec((d, tn), lambda j: (0, j)),
                  pl.BlockSpec((1, tn), lambda j: (0, j))],
        out_specs=pl.BlockSpec((bsz, tn), lambda j: (0, j)),
        out_shape=jax.ShapeDtypeStruct((bsz, n), F32),
        compiler_params=_cparams("arbitrary"),
        name="ada",
    )(c, w_ada, b_ada.reshape(1, n))


def _inproj_kernel(x_ref, mod_ref, g_ref, wn_ref, wki_ref, wt_ref, wwi_ref,
                   urnn_ref, ugate_ref, k_ref, glr_ref, gla_ref, ki_ref,
                   qt_ref, vt_ref, qit_ref, wit_ref):
    d = D_MODEL
    x = x_ref[0]
    h = _rms(x, g_ref[...]) * (1.0 + mod_ref[0, 1:2, :]) + mod_ref[0, 0:1, :]
    hb = h.astype(BF16)
    urnn_ref[0] = jnp.dot(hb, wn_ref[:, 0 * d:1 * d], preferred_element_type=F32)
    ugate_ref[0] = jnp.dot(hb, wn_ref[:, 1 * d:2 * d], preferred_element_type=F32)
    k_ref[0] = jnp.dot(hb, wn_ref[:, 2 * d:3 * d], preferred_element_type=F32).astype(BF16)
    glr_ref[0] = jnp.dot(hb, wn_ref[:, 3 * d:4 * d], preferred_element_type=F32)
    gla_ref[0] = jnp.dot(hb, wn_ref[:, 4 * d:5 * d], preferred_element_type=F32)
    ki_ref[0] = jnp.dot(hb, wki_ref[...], preferred_element_type=F32).astype(BF16)
    nt = (((1,), (1,)), ((), ()))
    qt_ref[0] = (lax.dot_general(wt_ref[0 * d:1 * d, :], hb, nt, preferred_element_type=F32)
                 * (HEAD_DIM ** -0.5 * LOG2E)).astype(BF16)
    vt_ref[0] = lax.dot_general(wt_ref[1 * d:2 * d, :], hb, nt, preferred_element_type=F32).astype(BF16)
    qit_ref[0] = lax.dot_general(wt_ref[2 * d:3 * d, :], hb, nt, preferred_element_type=F32).astype(BF16)
    wit_ref[0] = lax.dot_general(wwi_ref[...], hb, nt, preferred_element_type=F32)


def _in_proj(x, mod3, g0, w_in):
    bsz, s, d = x.shape
    tm = ROW_TILE
    offs = np.cumsum([0, d, d, d, d, d, IDX_HEADS * IDX_DIM, IDX_DIM, IDX_HEADS, d, d])
    seg = lambda i: w_in[:, int(offs[i]):int(offs[i + 1])]
    wn = jnp.concatenate([seg(0), seg(1), seg(3), seg(8), seg(9)], axis=1).astype(BF16)
    wki = seg(6).astype(BF16)
    wt = jnp.concatenate([seg(2), seg(4), seg(5)], axis=1).T.astype(BF16)
    wwi = seg(7).T.astype(BF16)
    const = lambda shape: pl.BlockSpec(shape, lambda b, i: (0,) * len(shape),
                                       pipeline_mode=pl.Buffered(1))
    row = lambda w: pl.BlockSpec((1, tm, w), lambda b, i: (b, i, 0))
    col = lambda r: pl.BlockSpec((1, r, tm), lambda b, i: (b, 0, i))
    out_shape = (
        jax.ShapeDtypeStruct((bsz, s, d), F32),
        jax.ShapeDtypeStruct((bsz, s, d), F32),
        jax.ShapeDtypeStruct((bsz, s, d), BF16),
        jax.ShapeDtypeStruct((bsz, s, d), F32),
        jax.ShapeDtypeStruct((bsz, s, d), F32),
        jax.ShapeDtypeStruct((bsz, s, IDX_DIM), BF16),
        jax.ShapeDtypeStruct((bsz, d, s), BF16),
        jax.ShapeDtypeStruct((bsz, d, s), BF16),
        jax.ShapeDtypeStruct((bsz, d, s), BF16),
        jax.ShapeDtypeStruct((bsz, IDX_HEADS, s), F32),
    )
    return pl.pallas_call(
        _inproj_kernel,
        grid=(bsz, s // tm),
        in_specs=[row(d),
                  pl.BlockSpec((1, N_MOD, d), lambda b, i: (b, 0, 0)),
                  pl.BlockSpec((1, d), lambda b, i: (0, 0)),
                  const(wn.shape), const(wki.shape), const(wt.shape), const(wwi.shape)],
        out_specs=(row(d), row(d), row(d), row(d), row(d), row(IDX_DIM),
                   col(d), col(d), col(d), col(IDX_HEADS)),
        out_shape=out_shape,
        compiler_params=_cparams("arbitrary", "arbitrary"),
        name="in_proj",
    )(x, mod3, g0, wn, wki, wt, wwi)


def _gelu_tanh(x):
    return 0.5 * x * (1.0 + jnp.tanh(math.sqrt(2.0 / math.pi) * (x + 0.044715 * (x * x * x))))


def _rglru_kernel(u_ref, ug_ref, cw_ref, cb_ref, wax_ref, ba_ref, bx_ref, lam_ref, y_ref,
                  ext_ref, a_ref, b_ref, carry_ref):
    ts = u_ref.shape[1]
    d = D_MODEL

    @pl.when(pl.program_id(1) == 0)
    def _():
        ext_ref[0:SUBLANES, :] = jnp.zeros((SUBLANES, d), F32)
        carry_ref[...] = jnp.zeros_like(carry_ref)

    ext_ref[SUBLANES:SUBLANES + ts, :] = u_ref[0]
    xc = cb_ref[...] + cw_ref[CONV_W - 1:CONV_W, :] * ext_ref[SUBLANES:SUBLANES + ts, :]
    for k in range(CONV_W - 1):
        off = SUBLANES - (CONV_W - 1) + k
        xc = xc + cw_ref[k:k + 1, :] * ext_ref[off:off + ts, :]
    ext_ref[0:SUBLANES, :] = ext_ref[ts:ts + SUBLANES, :]

    nl = -lam_ref[...]
    sp = jnp.maximum(nl, 0.0) + jnp.log1p(jnp.exp(-jnp.abs(nl)))
    for n in range(RNN_BLOCKS):
        cs = slice(n * RNN_BW, (n + 1) * RNN_BW)
        xb = xc[:, cs]
        g = _bdot(xb, wax_ref[n])
        r = jax.nn.sigmoid(g[:, :RNN_BW] + ba_ref[:, cs])
        i = jax.nn.sigmoid(g[:, RNN_BW:] + bx_ref[:, cs])
        log_a = (-LRU_C) * r * sp[:, cs]
        a_ref[:, cs] = jnp.exp(log_a)
        th = jnp.tanh(log_a)
        b_ref[:, cs] = jnp.sqrt(-2.0 * th / (1.0 - th)) * (i * xb)

    row = lax.broadcasted_iota(jnp.int32, (SUBLANES, d), 0)

    def group(gi, hprev):
        r0 = pl.multiple_of(gi * SUBLANES, SUBLANES)
        a = a_ref[pl.ds(r0, SUBLANES), :]
        b = b_ref[pl.ds(r0, SUBLANES), :]
        for sh in (1, 2, 4):
            keep = row >= sh
            a_s = jnp.where(keep, pltpu.roll(a, sh, 0), 1.0)
            b_s = jnp.where(keep, pltpu.roll(b, sh, 0), 0.0)
            b = a * b_s + b
            a = a * a_s
        h = b + a * hprev
        b_ref[pl.ds(r0, SUBLANES), :] = h
        return jnp.broadcast_to(h[SUBLANES - 1:SUBLANES, :], (SUBLANES, d))

    carry_ref[...] = lax.fori_loop(0, ts // SUBLANES, group, carry_ref[...])
    y_ref[0] = (b_ref[...] * _gelu_tanh(ug_ref[0])).astype(BF16)


def _rglru(u_rnn, u_gate, conv_w, conv_b, w_rg_a, b_rg_a, w_rg_x, b_rg_x, lam):
    bsz, s, d = u_rnn.shape
    ts = SCAN_TILE
    wax = jnp.concatenate([w_rg_a, w_rg_x], axis=-1).astype(BF16)
    vec = lambda v: v.reshape(1, d)
    c2 = lambda shape: pl.BlockSpec(shape, lambda b, i: (0,) * len(shape))
    row = pl.BlockSpec((1, ts, d), lambda b, i: (b, i, 0))
    return pl.pallas_call(
        _rglru_kernel,
        grid=(bsz, s // ts),
        in_specs=[row, row, c2((CONV_W, d)), c2((1, d)), c2(wax.shape), c2((1, d)), c2((1, d)),
                  c2((1, d))],
        out_specs=row,
        out_shape=jax.ShapeDtypeStruct((bsz, s, d), BF16),
        scratch_shapes=[pltpu.VMEM((ts + SUBLANES, d), F32), pltpu.VMEM((ts, d), F32),
                        pltpu.VMEM((ts, d), F32), pltpu.VMEM((SUBLANES, d), F32)],
        compiler_params=_cparams("arbitrary", "arbitrary"),
        name="rglru",
    )(u_rnn, u_gate, conv_w, vec(conv_b), wax, vec(b_rg_a), vec(b_rg_x), vec(lam))


def _t5_bucket_np(dist):
    max_exact = N_BUCKETS // 2
    dd = np.maximum(dist, 0)
    df = np.maximum(dd, 1).astype(np.float32)
    large = max_exact + (np.log(df / np.float32(max_exact)) / np.float32(math.log(MAX_DIST / max_exact))
                         * np.float32(N_BUCKETS - max_exact)).astype(np.int32)
    large = np.minimum(large, N_BUCKETS - 1)
    return np.where(dd < max_exact, dd, large)


def _near_bucket_ids():
    r = np.arange(KEY_GROUP)[None, :, None]
    c = np.arange(Q_TILE)[None, None, :]
    o = np.arange(2)[:, None, None]
    return _t5_bucket_np(c - r - (o - 1) * KEY_GROUP).astype(np.int32)


def _attn_kernel(rb_ref, bkt_ref, qt_ref, qit_ref, wit_ref, k_ref, vt_ref, ki_ref, y_ref,
                 s_ref, tab_ref, acc_ref, m_ref, lg_ref):
    tq = Q_TILE
    ks = KEY_SUB
    kg = KEY_GROUP
    jq = pl.program_id(1)
    t0 = jq * tq
    ngrp = jq + 1
    lane_t = t0 + lax.broadcasted_iota(jnp.int32, (1, tq), 1)

    @pl.when((pl.program_id(0) == 0) & (jq == 0))
    def _():
        for o in range(2):
            for h in range(N_HEADS):
                tab_ref[h, o] = jnp.zeros((kg, tq), F32)

            def fill(b, c):
                hit = bkt_ref[o] == b
                for h in range(N_HEADS):
                    val = (rb_ref[b, h] - rb_ref[N_BUCKETS - 1, h]) * LOG2E
                    tab_ref[h, o] = jnp.where(hit, val, tab_ref[h, o])
                return c

            lax.fori_loop(0, N_BUCKETS - 1, fill, 0)

    wi = wit_ref[0] * (IDX_HEADS ** -0.5 * IDX_DIM ** -0.5)

    def score_sub(i, mnmx, masked):
        r0 = pl.multiple_of(i * ks, ks)
        kic = ki_ref[0, pl.ds(r0, ks), :]
        acc = jnp.zeros((ks, tq), F32)
        for h in range(IDX_HEADS):
            dts = jnp.dot(kic, qit_ref[0, h * IDX_DIM:(h + 1) * IDX_DIM, :],
                          preferred_element_type=F32)
            acc = acc + jnp.maximum(dts, 0.0) * wi[h:h + 1, :]
        lo_src = acc
        if masked:
            key_s = r0 + lax.broadcasted_iota(jnp.int32, (ks, tq), 0)
            causal = key_s <= lane_t
            acc = jnp.where(causal, acc, NEG_INF)
            lo_src = jnp.where(causal, acc, jnp.inf)
        s_ref[pl.ds(r0, ks), :] = acc
        mn, mx = mnmx
        mn = jnp.minimum(mn, jnp.min(lo_src.reshape(ks // SUBLANES, SUBLANES, tq), axis=0))
        mx = jnp.maximum(mx, jnp.max(acc.reshape(ks // SUBLANES, SUBLANES, tq), axis=0))
        return mn, mx

    mnmx = (jnp.full((SUBLANES, tq), jnp.inf, F32), jnp.full((SUBLANES, tq), NEG_INF, F32))
    mnmx = lax.fori_loop(0, 2 * ngrp - 2, lambda i, c: score_sub(i, c, False), mnmx)
    mnmx = score_sub(2 * ngrp - 2, mnmx, True)
    mn8, mx8 = score_sub(2 * ngrp - 1, mnmx, True)
    smin = jnp.min(mn8, axis=0, keepdims=True)
    smax = jnp.max(mx8, axis=0, keepdims=True)

    n_causal = (lane_t + 1).astype(F32)
    k_eff = jnp.minimum(n_causal, float(TOPK_MAX))

    def count_rows(pred):
        part = kg // 8

        def body(g, c):
            r0 = pl.multiple_of(g * kg, kg)
            key_s = (r0 + lax.broadcasted_iota(jnp.int32, (kg, tq), 0)).astype(F32)
            ind = jnp.where(pred(s_ref[pl.ds(r0, kg), :], key_s), 1.0, 0.0)
            return c + jnp.sum(ind.reshape(8, part, tq), axis=0)

        c = lax.fori_loop(0, ngrp, body, jnp.zeros((part, tq), F32))
        return jnp.sum(c, axis=0, keepdims=True)

    def bis_body(st):
        it, lo, hi, c_lo, done, _ = st
        first = (jnp.zeros((1, tq), F32) + jnp.where(it == 0, 1.0, 0.0)) > 0.0
        probe = jnp.where(first, smax, 0.5 * lo + 0.5 * hi)
        collapsed = ~first & ((probe <= lo) | (probe >= hi))
        cnt = count_rows(lambda blk, _: blk >= probe)
        ge = cnt >= k_eff
        upd = (done == 0.0) & ~collapsed
        lo_n = jnp.where(upd & ge, probe, lo)
        c_lo_n = jnp.where(upd & ge, cnt, c_lo)
        hi_n = jnp.where(upd & ~ge, probe, hi)
        fin = collapsed | (cnt == k_eff) | (first & ge)
        done_n = jnp.where(fin, 1.0, done)
        return it + 1, lo_n, hi_n, c_lo_n, done_n, jnp.sum(1.0 - done_n)

    done0 = jnp.where(n_causal <= k_eff, 1.0, 0.0)
    _, thr, _, c_thr, _, _ = lax.while_loop(
        lambda st: st[5] > 0.0, bis_body,
        (jnp.int32(0), smin, smax, n_causal, done0, jnp.sum(1.0 - done0)))

    tie_all = jnp.zeros((1, tq), F32) + (t0 + tq).astype(F32)

    def tie_limit():
        need = k_eff - count_rows(lambda blk, _: blk > thr)

        def body(_, st):
            lo, hi = st
            mid = jnp.floor(0.5 * (lo + hi))
            ok = count_rows(lambda blk, key_s: (blk == thr) & (key_s < mid)) >= need
            return jnp.where(ok, lo, mid), jnp.where(ok, mid, hi)

        n_steps = int(math.ceil(math.log2(s_ref.shape[0]))) + 1
        return lax.fori_loop(0, n_steps, body, (jnp.zeros((1, tq), F32), tie_all))[1]

    excess = jnp.sum(jnp.where(c_thr > k_eff, 1.0, 0.0))
    tie_lim = lax.cond(excess > 0.0, tie_limit, lambda: tie_all)

    def mask_body(g, c):
        r0 = pl.multiple_of(g * kg, kg)
        blk = s_ref[pl.ds(r0, kg), :]
        key_s = (r0 + lax.broadcasted_iota(jnp.int32, (kg, tq), 0)).astype(F32)
        sel = (blk > thr) | ((blk == thr) & (key_s < tie_lim))
        s_ref[pl.ds(r0, kg), :] = jnp.where(sel, 0.0, NEG_INF)
        return c

    lax.fori_loop(0, ngrp, mask_body, 0)

    m_ref[...] = jnp.full_like(m_ref, NEG_INF)
    acc_ref[...] = jnp.zeros_like(acc_ref)
    ones_rows = jnp.ones((DEN_ROWS, kg), BF16)

    def attend(g, near):
        r0 = pl.multiple_of(g * kg, kg)
        msk = s_ref[pl.ds(r0, kg), :]
        col_max = []
        for h in range(N_HEADS):
            hs = slice(h * HEAD_DIM, (h + 1) * HEAD_DIM)
            kh = k_ref[0, pl.ds(r0, kg), hs]
            lg = jnp.dot(kh, qt_ref[0, hs, :], preferred_element_type=F32) + msk
            if near is not None:
                lg = lg + tab_ref[h, near]
            lg_ref[h] = lg
            col_max.append(jnp.max(lg, axis=0, keepdims=True))
        for h in range(N_HEADS):
            hs = slice(h * HEAD_DIM, (h + 1) * HEAD_DIM)
            m_old = m_ref[h:h + 1, :]
            m_new = jnp.maximum(m_old, col_max[h])
            m_safe = jnp.where(m_new == NEG_INF, 0.0, m_new)
            p = jnp.exp2(lg_ref[h] - m_safe).astype(BF16)
            alpha = jnp.exp2(m_old - m_safe)
            m_ref[h:h + 1, :] = m_new
            vh = jnp.concatenate([vt_ref[0, hs, pl.ds(r0, kg)], ones_rows], axis=0)
            acc_ref[h] = alpha * acc_ref[h] + jnp.dot(vh, p, preferred_element_type=F32)

    def _attend_body(g, c):
        attend(g, None)
        return c

    lax.fori_loop(0, jnp.maximum(ngrp - 2, 0), _attend_body, 0)

    @pl.when(jq > 0)
    def _():
        attend(ngrp - 2, 0)

    attend(ngrp - 1, 1)

    for h in range(N_HEADS):
        o = acc_ref[h, 0:HEAD_DIM, :] / acc_ref[h, HEAD_DIM:HEAD_DIM + 1, :]
        y_ref[0, :, h * HEAD_DIM:(h + 1) * HEAD_DIM] = o.T.astype(BF16)


def _attention(qt, qit, wit, k, vt, ki, rel_bias):
    bsz, d, s = qt.shape
    tq = Q_TILE
    bkt = jnp.asarray(_near_bucket_ids())
    once = lambda shape: pl.BlockSpec(shape, lambda b, j: (b,) + (0,) * (len(shape) - 1),
                                      pipeline_mode=pl.Buffered(1))
    col = lambda r: pl.BlockSpec((1, r, tq), lambda b, j: (b, 0, j))
    return pl.pallas_call(
        _attn_kernel,
        grid=(bsz, s // tq),
        in_specs=[pl.BlockSpec(memory_space=pltpu.SMEM),
                  pl.BlockSpec(bkt.shape, lambda b, j: (0, 0, 0)),
                  col(d), col(d), col(IDX_HEADS),
                  once((1, s, d)), once((1, d, s)), once((1, s, IDX_DIM))],
        out_specs=pl.BlockSpec((1, tq, d), lambda b, j: (b, j, 0)),
        out_shape=jax.ShapeDtypeStruct((bsz, s, d), BF16),
        scratch_shapes=[pltpu.VMEM((s, tq), F32),
                        pltpu.VMEM((N_HEADS, 2, KEY_GROUP, tq), F32),
                        pltpu.VMEM((N_HEADS, HEAD_DIM + DEN_ROWS, tq), F32),
                        pltpu.VMEM((N_HEADS, tq), F32),
                        pltpu.VMEM((N_HEADS, KEY_GROUP, tq), F32)],
        compiler_params=_cparams("arbitrary", "arbitrary"),
        name="attn",
    )(rel_bias, bkt, qt, qit, wit, k, vt, ki)


def _pack_bf16_pair(x):
    c = x.shape[1] // 2
    lo = lax.bitcast_convert_type(x[:, :c].astype(BF16).astype(F32), jnp.uint32) >> 16
    hi = lax.bitcast_convert_type(x[:, c:].astype(BF16).astype(F32), jnp.uint32) & jnp.uint32(0xFFFF0000)
    return lo | hi


def _unpack_bf16_pair(p):
    a = lax.bitcast_convert_type(p << 16, F32)
    b = lax.bitcast_convert_type(p & jnp.uint32(0xFFFF0000), F32)
    return a, b


def _unpack_rows_bf16(p):
    a, b = _unpack_bf16_pair(p)
    return jnp.concatenate([a, b], axis=1).astype(BF16)


def _merge_kernel(yr_ref, ya_ref, glr_ref, gla_ref, x_ref, mod_ref, g_ref,
                  wr_ref, wa_ref, wo_ref, x1_ref, h2p_ref):
    merged = (jax.nn.sigmoid(glr_ref[0]) * jnp.dot(yr_ref[0], wr_ref[...], preferred_element_type=F32)
              + jax.nn.sigmoid(gla_ref[0]) * jnp.dot(ya_ref[0], wa_ref[...], preferred_element_type=F32))
    y = _bdot(merged, wo_ref[...])
    x1 = x_ref[0] + mod_ref[0, 2:3, :] * _rms(y, g_ref[1:2, :])
    x1_ref[0] = x1
    h2 = _rms(x1, g_ref[2:3, :]) * (1.0 + mod_ref[0, 4:5, :]) + mod_ref[0, 3:4, :]
    h2p_ref[0] = _pack_bf16_pair(h2)


def _merge(y_rnn, y_attn, gl_rnn, gl_attn, x, mod3, gains, w_br_rnn, w_br_attn, w_out):
    bsz, s, d = x.shape
    tm = ROW_TILE
    row = pl.BlockSpec((1, tm, d), lambda b, i: (b, i, 0))
    half = pl.BlockSpec((1, tm, d // 2), lambda b, i: (b, i, 0))
    c2 = lambda shape: pl.BlockSpec(shape, lambda b, i: (0,) * len(shape))
    return pl.pallas_call(
        _merge_kernel,
        grid=(bsz, s // tm),
        in_specs=[row, row, row, row, row,
                  pl.BlockSpec((1, N_MOD, d), lambda b, i: (b, 0, 0)),
                  c2(gains.shape), c2((d, d)), c2((d, d)), c2((d, d))],
        out_specs=(row, half),
        out_shape=(jax.ShapeDtypeStruct((bsz, s, d), F32),
                   jax.ShapeDtypeStruct((bsz, s, d // 2), jnp.uint32)),
        compiler_params=_cparams("arbitrary", "arbitrary"),
        name="merge",
    )(y_rnn, y_attn, gl_rnn, gl_attn, x, mod3, gains,
      w_br_rnn.astype(BF16), w_br_attn.astype(BF16), w_out.astype(BF16))


def _router_kernel(hp_ref, wr_ref, rb_ref, dest_ref, wk_ref, seg_ref):
    t = hp_ref.shape[0]
    gsz = N_EXPERTS // N_GROUPS
    h = _unpack_rows_bf16(hp_ref[...])
    s = jax.nn.sigmoid(_bdot_nt(wr_ref[...], h))
    s_sel = s + rb_ref[...]
    g3 = s_sel.reshape(N_GROUPS, gsz, t)
    e_in_g = lax.broadcasted_iota(jnp.int32, (N_GROUPS, gsz, t), 1)
    top1 = jnp.max(g3, axis=1, keepdims=True)
    first = jnp.min(jnp.where(g3 == top1, e_in_g, gsz), axis=1, keepdims=True)
    top2 = jnp.max(jnp.where(e_in_g == first, NEG_INF, g3), axis=1, keepdims=True)
    gscore = jnp.broadcast_to(top1 + top2, (N_GROUPS, gsz, t))
    gi = lax.broadcasted_iota(jnp.int32, (N_GROUPS, gsz, t), 0)
    gmask = jnp.zeros((N_GROUPS, gsz, t), F32)
    for _ in range(TOPK_GROUPS):
        mx = jnp.max(gscore, axis=0, keepdims=True)
        pick = jnp.min(jnp.where(gscore == mx, gi, N_GROUPS), axis=0, keepdims=True)
        hit = gi == pick
        gmask = jnp.where(hit, 1.0, gmask)
        gscore = jnp.where(hit, NEG_INF, gscore)
    cand = jnp.where(gmask.reshape(N_EXPERTS, t) > 0.0, s_sel, NEG_INF)
    ei = lax.broadcasted_iota(jnp.int32, (N_EXPERTS, t), 0)
    sel = jnp.zeros((N_EXPERTS, t), F32)
    picks = []
    for _ in range(TOP_K):
        mx = jnp.max(cand, axis=0, keepdims=True)
        pick = jnp.min(jnp.where(cand == mx, ei, N_EXPERTS), axis=0, keepdims=True)
        hit = ei == pick
        sel = jnp.where(hit, 1.0, sel)
        cand = jnp.where(hit, NEG_INF, cand)
        picks.append(pick)
    w = s * sel
    w = w / jnp.sum(w, axis=0, keepdims=True) * ROUTED_SCALE

    selb = sel.astype(BF16)
    tok_r = lax.broadcasted_iota(jnp.int32, (t, t), 0)
    tok_c = lax.broadcasted_iota(jnp.int32, (t, t), 1)
    rank = jnp.dot(selb, jnp.where(tok_r < tok_c, 1.0, 0.0).astype(BF16), preferred_element_type=F32)
    e_r = lax.broadcasted_iota(jnp.int32, (N_EXPERTS, N_EXPERTS), 0)
    e_c = lax.broadcasted_iota(jnp.int32, (N_EXPERTS, N_EXPERTS), 1)
    cnt_col = jnp.sum(sel, axis=1, keepdims=True)
    pad_col = jnp.floor((cnt_col + (SUBLANES - 1.0)) * (1.0 / SUBLANES))
    off_col = SUBLANES * jnp.dot(jnp.where(e_c < e_r, 1.0, 0.0).astype(BF16),
                                 jnp.broadcast_to(pad_col, (N_EXPERTS, LANES)).astype(BF16),
                                 preferred_element_type=F32)[:, 0:1]
    slot = rank + off_col
    dest_rows, w_rows = [], []
    for k in range(TOP_K):
        hit = ei == picks[k]
        dest_rows.append(jnp.sum(jnp.where(hit, slot, 0.0), axis=0, keepdims=True))
        w_rows.append(jnp.sum(jnp.where(hit, w, 0.0), axis=0, keepdims=True))
    dest_ref[0] = jnp.concatenate(dest_rows, axis=0).astype(jnp.int32)
    wk_ref[...] = jnp.concatenate(w_rows, axis=0)
    cnt_row = _bdot_nt(jnp.ones((SUBLANES, t), BF16), selb)
    pad_row = jnp.floor((cnt_row + (SUBLANES - 1.0)) * (1.0 / SUBLANES))
    off_row = SUBLANES * jnp.dot(pad_row.astype(BF16), jnp.where(e_r < e_c, 1.0, 0.0).astype(BF16),
                                 preferred_element_type=F32)
    seg_ref[0] = jnp.concatenate([off_row[0:1], cnt_row[0:1]], axis=1).astype(jnp.int32)


def _router(h2p, w_router, router_bias):
    n, dh = h2p.shape
    t = MOE_TILE
    nt = n // t
    return pl.pallas_call(
        _router_kernel,
        grid=(nt,),
        in_specs=[pl.BlockSpec((t, dh), lambda i: (i, 0)),
                  pl.BlockSpec((N_EXPERTS, 2 * dh), lambda i: (0, 0)),
                  pl.BlockSpec((N_EXPERTS, 1), lambda i: (0, 0))],
        out_specs=(pl.BlockSpec((1, TOP_K, t), lambda i: (i, 0, 0)),
                   pl.BlockSpec((TOP_K, t), lambda i: (0, i)),
                   pl.BlockSpec((1, 1, 2 * N_EXPERTS), lambda i: (i, 0, 0))),
        out_shape=(jax.ShapeDtypeStruct((nt, TOP_K, t), jnp.int32),
                   jax.ShapeDtypeStruct((TOP_K, n), F32),
                   jax.ShapeDtypeStruct((nt, 1, 2 * N_EXPERTS), jnp.int32)),
        compiler_params=_cparams("arbitrary"),
        name="router",
    )(h2p, w_router.T.astype(BF16), router_bias.reshape(N_EXPERTS, 1))


def _swiglu(x, wgu, wd):
    gu = jnp.dot(x, wgu, preferred_element_type=F32)
    g = gu[:, :D_EXPERT]
    a = (g * jax.nn.sigmoid(g)) * gu[:, D_EXPERT:]
    return jnp.dot(a.astype(BF16), wd, preferred_element_type=F32)


def _moe_kernel(seg_ref, hp_ref, dest_ref, wt_ref, wgu_ref, wd_ref, wsgu_ref, wsd_ref, x1_ref,
                mod_ref, g_ref, o_ref, xs_ref, stage_ref, dest_smem, sem):
    i = pl.program_id(0)
    s = pl.program_id(1)
    n_s = pl.num_programs(1)
    t = MOE_TILE
    half = xs_ref.shape[1]

    @pl.when((i == 0) & (s == 0))
    def _():
        xs_ref[...] = jnp.zeros_like(xs_ref)

    @pl.when(s == 0)
    def _():
        cp = pltpu.make_async_copy(dest_ref, dest_smem, sem)
        cp.start()
        cp.wait()

        def body(nb, c):
            n0 = nb * MOVE_TOKENS
            base = n0 * TOP_K
            for j in range(MOVE_TOKENS):
                row = hp_ref[pl.ds(n0 + j, 1), :]
                for k in range(TOP_K):
                    xs_ref[pl.ds(dest_smem[base + (j * TOP_K + k)], 1), :] = row
            return c

        lax.fori_loop(0, t // MOVE_TOKENS, body, 0)

    def put_rows(r0, end, y, old):
        rows = r0 + lax.broadcasted_iota(jnp.int32, (FFN_ROWS, half), 0)
        xs_ref[pl.ds(r0, FFN_ROWS), :] = jnp.where(rows < end, _pack_bf16_pair(y), old)

    @pl.when(s < n_s - 1)
    def _():
        offs, ends = [], []
        for eb in range(EXPERTS_PER_STEP):
            e = s * EXPERTS_PER_STEP + eb
            offs.append(pl.multiple_of(seg_ref[i, e], SUBLANES))
            ends.append(offs[eb] + seg_ref[i, N_EXPERTS + e])
        olds = [xs_ref[pl.ds(offs[eb], FFN_ROWS), :] for eb in range(EXPERTS_PER_STEP)]
        ys = [_swiglu(_unpack_rows_bf16(olds[eb]), wgu_ref[eb], wd_ref[eb])
              for eb in range(EXPERTS_PER_STEP)]
        for eb in range(EXPERTS_PER_STEP):
            put_rows(offs[eb], ends[eb], ys[eb], olds[eb])
        for eb in range(EXPERTS_PER_STEP):
            def chunk(c, carry, eb=eb):
                r0 = pl.multiple_of(offs[eb] + c * FFN_ROWS, SUBLANES)
                old = xs_ref[pl.ds(r0, FFN_ROWS), :]
                put_rows(r0, ends[eb], _swiglu(_unpack_rows_bf16(old), wgu_ref[eb], wd_ref[eb]), old)
                return carry

            n_chunks = lax.div(ends[eb] - offs[eb] + (FFN_ROWS - 1), FFN_ROWS)
            lax.fori_loop(1, n_chunks, chunk, 0)

    @pl.when(s == n_s - 1)
    def _():
        o_ref[...] = _swiglu(_unpack_rows_bf16(hp_ref[...]), wsgu_ref[...], wsd_ref[...])

        def body(nb, c):
            n0 = pl.multiple_of(nb * SUBLANES, SUBLANES)
            base = n0 * TOP_K
            for j in range(SUBLANES):
                for k in range(TOP_K):
                    stage_ref[k, pl.ds(j, 1), :] = xs_ref[pl.ds(dest_smem[base + (j * TOP_K + k)], 1), :]
            wrow = wt_ref[pl.ds(n0, SUBLANES), :]
            acc_a = jnp.zeros((SUBLANES, half), F32)
            acc_b = jnp.zeros((SUBLANES, half), F32)
            for k in range(TOP_K):
                a, b = _unpack_bf16_pair(stage_ref[k])
                acc_a = acc_a + a * wrow[:, k:k + 1]
                acc_b = acc_b + b * wrow[:, k:k + 1]
            o_ref[pl.ds(n0, SUBLANES), :] += jnp.concatenate([acc_a, acc_b], axis=1)
            return c

        lax.fori_loop(0, t // SUBLANES, body, 0)
        o_ref[...] = x1_ref[...] + mod_ref[0, 5:6, :] * _rms(o_ref[...], g_ref[...])


def _moe(h2p, dest, wt, seg, wgu, wd, wsgu, wsd, x1, mod3, g3, seq):
    n, dh = h2p.shape
    d = 2 * dh
    t = MOE_TILE
    n_steps = N_EXPERTS // EXPERTS_PER_STEP + 1
    tiles_per_seq = seq // t
    tok = lambda w: pl.BlockSpec((t, w), lambda i, s: (i, 0))
    wblk = lambda r, c: pl.BlockSpec((EXPERTS_PER_STEP, r, c),
                                     lambda i, s: (jnp.minimum(s, n_steps - 2), 0, 0))
    c2 = lambda shape: pl.BlockSpec(shape, lambda i, s: (0,) * len(shape))
    xs_rows = TOP_K * t + N_EXPERTS * SUBLANES + FFN_ROWS
    return pl.pallas_call(
        _moe_kernel,
        grid=(n // t, n_steps),
        in_specs=[pl.BlockSpec(memory_space=pltpu.SMEM),
                  tok(dh),
                  pl.BlockSpec((TOP_K * t,), lambda i, s: (i,)),
                  tok(TOP_K),
                  wblk(d, 2 * D_EXPERT), wblk(D_EXPERT, d),
                  c2((d, 2 * D_EXPERT)), c2((D_EXPERT, d)),
                  tok(d),
                  pl.BlockSpec((1, N_MOD, d), lambda i, s: (i // tiles_per_seq, 0, 0)),
                  c2((1, d))],
        out_specs=tok(d),
        out_shape=jax.ShapeDtypeStruct((n, d), F32),
        scratch_shapes=[pltpu.VMEM((xs_rows, dh), jnp.uint32),
                        pltpu.VMEM((TOP_K, SUBLANES, dh), jnp.uint32),
                        pltpu.SMEM((TOP_K * t,), jnp.int32),
                        pltpu.SemaphoreType.DMA(())],
        compiler_params=_cparams("arbitrary", "arbitrary"),
        name="moe",
    )(seg, h2p, dest, wt, wgu, wd, wsgu, wsd, x1, mod3, g3)


def kernel(x, c, w_ada, b_ada, norm_gain, w_in, conv_w, conv_b, w_rg_a, b_rg_a, w_rg_x, b_rg_x,
           lru_lambda, w_br_rnn, w_br_attn, w_out, rel_bias, w_router, router_bias,
           w_exp_gate, w_exp_up, w_exp_down, w_sh_gate, w_sh_up, w_sh_down):
    bsz, s, d = x.shape
    depth = w_ada.shape[0]
    for l in range(depth):
        mod = _ada(c, w_ada[l], b_ada[l])
        mod3 = mod.reshape(bsz, N_MOD, d)
        gains = norm_gain[l]
        (u_rnn, u_gate, k, gl_rnn, gl_attn, ki, qt, vt, qit, wit) = _in_proj(x, mod3, gains[0:1], w_in[l])
        y_rnn = _rglru(u_rnn, u_gate, conv_w[l], conv_b[l], w_rg_a[l], b_rg_a[l], w_rg_x[l],
                       b_rg_x[l], lru_lambda[l])
        y_attn = _attention(qt, qit, wit, k, vt, ki, rel_bias)
        x1, h2p = _merge(y_rnn, y_attn, gl_rnn, gl_attn, x, mod3, gains, w_br_rnn[l], w_br_attn[l],
                         w_out[l])
        h2p = h2p.reshape(bsz * s, d // 2)
        dest, wk, seg = _router(h2p, w_router[l], router_bias[l])
        wgu = jnp.concatenate([w_exp_gate[l], w_exp_up[l]], axis=-1).astype(BF16)
        wsgu = jnp.concatenate([w_sh_gate[l], w_sh_up[l]], axis=-1).astype(BF16)
        dest = jnp.transpose(dest, (0, 2, 1)).reshape(-1)
        x = _moe(h2p, dest, wk.T, seg.reshape(-1, 2 * N_EXPERTS), wgu, w_exp_down[l].astype(BF16),
                 wsgu, w_sh_down[l].astype(BF16), x1.reshape(bsz * s, d), mod3, gains[3:4],
                 s).reshape(bsz, s, d)
    return x
```

```python
import functools
import math

import jax
import jax.numpy as jnp
import numpy as np
from jax import lax
from jax.experimental import pallas as pl
from jax.experimental.pallas import tpu as pltpu

F32 = jnp.float32
BF16 = jnp.bfloat16

D_MODEL = 1024
RNN_BLOCKS = 8
RNN_BW = D_MODEL // RNN_BLOCKS
CONV_W = 4
LRU_C = 8.0
N_HEADS = 8
HEAD_DIM = 128
IDX_HEADS = 16
IDX_DIM = 64
TOPK_MAX = 256
N_BUCKETS = 32
MAX_DIST = 128
N_EXPERTS = 64
TOP_K = 8
N_GROUPS = 8
TOPK_GROUPS = 4
D_EXPERT = 256
ROUTED_SCALE = 2.5
N_MOD = 6
EPS = 1e-6

LANES = 128
SUBLANES = 8
VMEM_LIMIT_BYTES = 58 * 1024 * 1024

ROW_TILE = 256
SCAN_TILE = 512
Q_TILE = 256
KEY_SUB = 128
KEY_GROUP = 256
DEN_ROWS = 16
MOE_TILE = 1024
EXPERTS_PER_STEP = 4
FFN_ROWS = 160
MOVE_TOKENS = 4

NEG_INF = float("-inf")
LOG2E = math.log2(math.e)


def _cparams(*sem):
    return pltpu.CompilerParams(dimension_semantics=sem, vmem_limit_bytes=VMEM_LIMIT_BYTES)


def _bdot(a, b):
    return jnp.dot(a.astype(BF16), b.astype(BF16), preferred_element_type=F32)


def _bdot_nt(a, b):
    return lax.dot_general(a.astype(BF16), b.astype(BF16), (((1,), (1,)), ((), ())),
                           preferred_element_type=F32)


def _rms(x, g):
    ms = jnp.mean(x * x, axis=-1, keepdims=True)
    return x * lax.rsqrt(ms + EPS) * g


def _ada_kernel(c_ref, w_ref, b_ref, o_ref):
    c = c_ref[...]
    cond = c * jax.nn.sigmoid(c)
    o_ref[...] = _bdot(cond, w_ref[...]) + b_ref[...]


def _ada(c, w_ada, b_ada):
    bsz, d = c.shape
    n = w_ada.shape[1]
    tn = 1024
    return pl.pallas_call(
        _ada_kernel,
        grid=(n // tn,),
        in_specs=[pl.BlockSpec((bsz, d), lambda j: (0, 0)),
                  pl.BlockSpec((d, tn), lambda j: (0, j)),
                  pl.BlockSpec((1, tn), lambda j: (0, j))],
        out_specs=pl.BlockSpec((bsz, tn), lambda j: (0, j)),
        out_shape=jax.ShapeDtypeStruct((bsz, n), F32),
        compiler_params=_cparams("arbitrary"),
        name="ada",
    )(c, w_ada, b_ada.reshape(1, n))


def _inproj_kernel(x_ref, mod_ref, g_ref, wn_ref, wki_ref, wt_ref, wwi_ref,
                   urnn_ref, ugate_ref, k_ref, glr_ref, gla_ref, ki_ref,
                   qt_ref, vt_ref, qit_ref, wit_ref):
    d = D_MODEL
    x = x_ref[0]
    h = _rms(x, g_ref[...]) * (1.0 + mod_ref[0, 1:2, :]) + mod_ref[0, 0:1, :]
    hb = h.astype(BF16)
    urnn_ref[0] = jnp.dot(hb, wn_ref[:, 0 * d:1 * d], preferred_element_type=F32)
    ugate_ref[0] = jnp.dot(hb, wn_ref[:, 1 * d:2 * d], preferred_element_type=F32)
    k_ref[0] = jnp.dot(hb, wn_ref[:, 2 * d:3 * d], preferred_element_type=F32).astype(BF16)
    glr_ref[0] = jnp.dot(hb, wn_ref[:, 3 * d:4 * d], preferred_element_type=F32)
    gla_ref[0] = jnp.dot(hb, wn_ref[:, 4 * d:5 * d], preferred_element_type=F32)
    ki_ref[0] = jnp.dot(hb, wki_ref[...], preferred_element_type=F32).astype(BF16)
    nt = (((1,), (1,)), ((), ()))
    qt_ref[0] = (lax.dot_general(wt_ref[0 * d:1 * d, :], hb, nt, preferred_element_type=F32)
                 * (HEAD_DIM ** -0.5 * LOG2E)).astype(BF16)
    vt_ref[0] = lax.dot_general(wt_ref[1 * d:2 * d, :], hb, nt, preferred_element_type=F32).astype(BF16)
    qit_ref[0] = lax.dot_general(wt_ref[2 * d:3 * d, :], hb, nt, preferred_element_type=F32).astype(BF16)
    wit_ref[0] = lax.dot_general(wwi_ref[...], hb, nt, preferred_element_type=F32)


def _in_proj(x, mod3, g0, w_in):
    bsz, s, d = x.shape
    tm = ROW_TILE
    offs = np.cumsum([0, d, d, d, d, d, IDX_HEADS * IDX_DIM, IDX_DIM, IDX_HEADS, d, d])
    seg = lambda i: w_in[:, int(offs[i]):int(offs[i + 1])]
    wn = jnp.concatenate([seg(0), seg(1), seg(3), seg(8), seg(9)], axis=1).astype(BF16)
    wki = seg(6).astype(BF16)
    wt = jnp.concatenate([seg(2), seg(4), seg(5)], axis=1).T.astype(BF16)
    wwi = seg(7).T.astype(BF16)
    const = lambda shape: pl.BlockSpec(shape, lambda b, i: (0,) * len(shape),
                                       pipeline_mode=pl.Buffered(1))
    row = lambda w: pl.BlockSpec((1, tm, w), lambda b, i: (b, i, 0))
    col = lambda r: pl.BlockSpec((1, r, tm), lambda b, i: (b, 0, i))
    out_shape = (
        jax.ShapeDtypeStruct((bsz, s, d), F32),
        jax.ShapeDtypeStruct((bsz, s, d), F32),
        jax.ShapeDtypeStruct((bsz, s, d), BF16),
        jax.ShapeDtypeStruct((bsz, s, d), F32),
        jax.ShapeDtypeStruct((bsz, s, d), F32),
        jax.ShapeDtypeStruct((bsz, s, IDX_DIM), BF16),
        jax.ShapeDtypeStruct((bsz, d, s), BF16),
        jax.ShapeDtypeStruct((bsz, d, s), BF16),
        jax.ShapeDtypeStruct((bsz, d, s), BF16),
        jax.ShapeDtypeStruct((bsz, IDX_HEADS, s), F32),
    )
    return pl.pallas_call(
        _inproj_kernel,
        grid=(bsz, s // tm),
        in_specs=[row(d),
                  pl.BlockSpec((1, N_MOD, d), lambda b, i: (b, 0, 0)),
                  pl.BlockSpec((1, d), lambda b, i: (0, 0)),
                  const(wn.shape), const(wki.shape), const(wt.shape), const(wwi.shape)],
        out_specs=(row(d), row(d), row(d), row(d), row(d), row(IDX_DIM),
                   col(d), col(d), col(d), col(IDX_HEADS)),
        out_shape=out_shape,
        compiler_params=_cparams("arbitrary", "arbitrary"),
        name="in_proj",
    )(x, mod3, g0, wn, wki, wt, wwi)


def _gelu_tanh(x):
    return 0.5 * x * (1.0 + jnp.tanh(math.sqrt(2.0 / math.pi) * (x + 0.044715 * (x * x * x))))


def _rglru_kernel(u_ref, ug_ref, cw_ref, cb_ref, wax_ref, ba_ref, bx_ref, lam_ref, y_ref,
                  ext_ref, a_ref, b_ref, carry_ref):
    ts = u_ref.shape[1]
    d = D_MODEL

    @pl.when(pl.program_id(1) == 0)
    def _():
        ext_ref[0:SUBLANES, :] = jnp.zeros((SUBLANES, d), F32)
        carry_ref[...] = jnp.zeros_like(carry_ref)

    ext_ref[SUBLANES:SUBLANES + ts, :] = u_ref[0]
    xc = cb_ref[...] + cw_ref[CONV_W - 1:CONV_W, :] * ext_ref[SUBLANES:SUBLANES + ts, :]
    for k in range(CONV_W - 1):
        off = SUBLANES - (CONV_W - 1) + k
        xc = xc + cw_ref[k:k + 1, :] * ext_ref[off:off + ts, :]
    ext_ref[0:SUBLANES, :] = ext_ref[ts:ts + SUBLANES, :]

    nl = -lam_ref[...]
    sp = jnp.maximum(nl, 0.0) + jnp.log1p(jnp.exp(-jnp.abs(nl)))
    for n in range(RNN_BLOCKS):
        cs = slice(n * RNN_BW, (n + 1) * RNN_BW)
        xb = xc[:, cs]
        g = _bdot(xb, wax_ref[n])
        r = jax.nn.sigmoid(g[:, :RNN_BW] + ba_ref[:, cs])
        i = jax.nn.sigmoid(g[:, RNN_BW:] + bx_ref[:, cs])
        log_a = (-LRU_C) * r * sp[:, cs]
        a_ref[:, cs] = jnp.exp(log_a)
        th = jnp.tanh(log_a)
        b_ref[:, cs] = jnp.sqrt(-2.0 * th / (1.0 - th)) * (i * xb)

    row = lax.broadcasted_iota(jnp.int32, (SUBLANES, d), 0)

    def group(gi, hprev):
        r0 = pl.multiple_of(gi * SUBLANES, SUBLANES)
        a = a_ref[pl.ds(r0, SUBLANES), :]
        b = b_ref[pl.ds(r0, SUBLANES), :]
        for sh in (1, 2, 4):
            keep = row >= sh
            a_s = jnp.where(keep, pltpu.roll(a, sh, 0), 1.0)
            b_s = jnp.where(keep, pltpu.roll(b, sh, 0), 0.0)
            b = a * b_s + b
            a = a * a_s
        h = b + a * hprev
        b_ref[pl.ds(r0, SUBLANES), :] = h
        return jnp.broadcast_to(h[SUBLANES - 1:SUBLANES, :], (SUBLANES, d))

    carry_ref[...] = lax.fori_loop(0, ts // SUBLANES, group, carry_ref[...])
    y_ref[0] = (b_ref[...] * _gelu_tanh(ug_ref[0])).astype(BF16)


def _rglru(u_rnn, u_gate, conv_w, conv_b, w_rg_a, b_rg_a, w_rg_x, b_rg_x, lam):
    bsz, s, d = u_rnn.shape
    ts = SCAN_TILE
    wax = jnp.concatenate([w_rg_a, w_rg_x], axis=-1).astype(BF16)
    vec = lambda v: v.reshape(1, d)
    c2 = lambda shape: pl.BlockSpec(shape, lambda b, i: (0,) * len(shape))
    row = pl.BlockSpec((1, ts, d), lambda b, i: (b, i, 0))
    return pl.pallas_call(
        _rglru_kernel,
        grid=(bsz, s // ts),
        in_specs=[row, row, c2((CONV_W, d)), c2((1, d)), c2(wax.shape), c2((1, d)), c2((1, d)),
                  c2((1, d))],
        out_specs=row,
        out_shape=jax.ShapeDtypeStruct((bsz, s, d), BF16),
        scratch_shapes=[pltpu.VMEM((ts + SUBLANES, d), F32), pltpu.VMEM((ts, d), F32),
                        pltpu.VMEM((ts, d), F32), pltpu.VMEM((SUBLANES, d), F32)],
        compiler_params=_cparams("arbitrary", "arbitrary"),
        name="rglru",
    )(u_rnn, u_gate, conv_w, vec(conv_b), wax, vec(b_rg_a), vec(b_rg_x), vec(lam))


def _t5_bucket_np(dist):
    max_exact = N_BUCKETS // 2
    dd = np.maximum(dist, 0)
    df = np.maximum(dd, 1).astype(np.float32)
    large = max_exact + (np.log(df / np.float32(max_exact)) / np.float32(math.log(MAX_DIST / max_exact))
                         * np.float32(N_BUCKETS - max_exact)).astype(np.int32)
    large = np.minimum(large, N_BUCKETS - 1)
    return np.where(dd < max_exact, dd, large)


def _near_bucket_ids():
    r = np.arange(KEY_GROUP)[None, :, None]
    c = np.arange(Q_TILE)[None, None, :]
    o = np.arange(2)[:, None, None]
    return _t5_bucket_np(c - r - (o - 1) * KEY_GROUP).astype(np.int32)


def _attn_kernel(rb_ref, bkt_ref, qt_ref, qit_ref, wit_ref, k_ref, vt_ref, ki_ref, y_ref,
                 s_ref, tab_ref, acc_ref, m_ref, lg_ref):
    tq = Q_TILE
    ks = KEY_SUB
    kg = KEY_GROUP
    jq = pl.program_id(1)
    t0 = jq * tq
    ngrp = jq + 1
    lane_t = t0 + lax.broadcasted_iota(jnp.int32, (1, tq), 1)

    @pl.when((pl.program_id(0) == 0) & (jq == 0))
    def _():
        for o in range(2):
            for h in range(N_HEADS):
                tab_ref[h, o] = jnp.zeros((kg, tq), F32)

            def fill(b, c):
                hit = bkt_ref[o] == b
                for h in range(N_HEADS):
                    val = (rb_ref[b, h] - rb_ref[N_BUCKETS - 1, h]) * LOG2E
                    tab_ref[h, o] = jnp.where(hit, val, tab_ref[h, o])
                return c

            lax.fori_loop(0, N_BUCKETS - 1, fill, 0)

    wi = wit_ref[0] * (IDX_HEADS ** -0.5 * IDX_DIM ** -0.5)

    def score_sub(i, mnmx, masked):
        r0 = pl.multiple_of(i * ks, ks)
        kic = ki_ref[0, pl.ds(r0, ks), :]
        acc = jnp.zeros((ks, tq), F32)
        for h in range(IDX_HEADS):
            dts = jnp.dot(kic, qit_ref[0, h * IDX_DIM:(h + 1) * IDX_DIM, :],
                          preferred_element_type=F32)
            acc = acc + jnp.maximum(dts, 0.0) * wi[h:h + 1, :]
        lo_src = acc
        if masked:
            key_s = r0 + lax.broadcasted_iota(jnp.int32, (ks, tq), 0)
            causal = key_s <= lane_t
            acc = jnp.where(causal, acc, NEG_INF)
            lo_src = jnp.where(causal, acc, jnp.inf)
        s_ref[pl.ds(r0, ks), :] = acc
        mn, mx = mnmx
        mn = jnp.minimum(mn, jnp.min(lo_src.reshape(ks // SUBLANES, SUBLANES, tq), axis=0))
        mx = jnp.maximum(mx, jnp.max(acc.reshape(ks // SUBLANES, SUBLANES, tq), axis=0))
        return mn, mx

    mnmx = (jnp.full((SUBLANES, tq), jnp.inf, F32), jnp.full((SUBLANES, tq), NEG_INF, F32))
    mnmx = lax.fori_loop(0, ngrp - 1,
                         lambda g, c: score_sub(2 * g + 1, score_sub(2 * g, c, False), False), mnmx)
    mnmx = score_sub(2 * ngrp - 2, mnmx, True)
    mn8, mx8 = score_sub(2 * ngrp - 1, mnmx, True)
    smin = jnp.min(mn8, axis=0, keepdims=True)
    smax = jnp.max(mx8, axis=0, keepdims=True)

    n_causal = (lane_t + 1).astype(F32)
    k_eff = jnp.minimum(n_causal, float(TOPK_MAX))

    def count_rows(pred):
        part = kg // 8

        def body(g, c):
            r0 = pl.multiple_of(g * kg, kg)
            key_s = (r0 + lax.broadcasted_iota(jnp.int32, (kg, tq), 0)).astype(F32)
            ind = jnp.where(pred(s_ref[pl.ds(r0, kg), :], key_s), 1.0, 0.0)
            return c + jnp.sum(ind.reshape(8, part, tq), axis=0)

        c = lax.fori_loop(0, ngrp, body, jnp.zeros((part, tq), F32))
        return jnp.sum(c, axis=0, keepdims=True)

    def bis_body(st):
        it, lo, hi, c_lo, done, _ = st
        first = (jnp.zeros((1, tq), F32) + jnp.where(it == 0, 1.0, 0.0)) > 0.0
        probe = jnp.where(first, smax, 0.5 * lo + 0.5 * hi)
        collapsed = ~first & ((probe <= lo) | (probe >= hi))
        cnt = count_rows(lambda blk, _: blk >= probe)
        ge = cnt >= k_eff
        upd = (done == 0.0) & ~collapsed
        lo_n = jnp.where(upd & ge, probe, lo)
        c_lo_n = jnp.where(upd & ge, cnt, c_lo)
        hi_n = jnp.where(upd & ~ge, probe, hi)
        fin = collapsed | (cnt == k_eff) | (first & ge)
        done_n = jnp.where(fin, 1.0, done)
        return it + 1, lo_n, hi_n, c_lo_n, done_n, jnp.sum(1.0 - done_n)

    done0 = jnp.where(n_causal <= k_eff, 1.0, 0.0)
    _, thr, _, c_thr, _, _ = lax.while_loop(
        lambda st: st[5] > 0.0, bis_body,
        (jnp.int32(0), smin, smax, n_causal, done0, jnp.sum(1.0 - done0)))

    tie_all = jnp.zeros((1, tq), F32) + (t0 + tq).astype(F32)

    def tie_limit():
        need = k_eff - count_rows(lambda blk, _: blk > thr)

        def body(_, st):
            lo, hi = st
            mid = jnp.floor(0.5 * (lo + hi))
            ok = count_rows(lambda blk, key_s: (blk == thr) & (key_s < mid)) >= need
            return jnp.where(ok, lo, mid), jnp.where(ok, mid, hi)

        n_steps = int(math.ceil(math.log2(s_ref.shape[0]))) + 1
        return lax.fori_loop(0, n_steps, body, (jnp.zeros((1, tq), F32), tie_all))[1]

    excess = jnp.sum(jnp.where(c_thr > k_eff, 1.0, 0.0))
    tie_lim = lax.cond(excess > 0.0, tie_limit, lambda: tie_all)

    def mask_body(g, c):
        r0 = pl.multiple_of(g * kg, kg)
        blk = s_ref[pl.ds(r0, kg), :]
        key_s = (r0 + lax.broadcasted_iota(jnp.int32, (kg, tq), 0)).astype(F32)
        sel = (blk > thr) | ((blk == thr) & (key_s < tie_lim))
        s_ref[pl.ds(r0, kg), :] = jnp.where(sel, 0.0, NEG_INF)
        return c

    lax.fori_loop(0, ngrp, mask_body, 0)

    m_ref[...] = jnp.full_like(m_ref, NEG_INF)
    acc_ref[...] = jnp.zeros_like(acc_ref)
    ones_rows = jnp.ones((DEN_ROWS, kg), BF16)

    def attend(g, near):
        r0 = pl.multiple_of(g * kg, kg)
        msk = s_ref[pl.ds(r0, kg), :]
        col_max = []
        for h in range(N_HEADS):
            hs = slice(h * HEAD_DIM, (h + 1) * HEAD_DIM)
            kh = k_ref[0, pl.ds(r0, kg), hs]
            lg = jnp.dot(kh, qt_ref[0, hs, :], preferred_element_type=F32) + msk
            if near is not None:
                lg = lg + tab_ref[h, near]
            lg_ref[h] = lg
            col_max.append(jnp.max(lg, axis=0, keepdims=True))
        for h in range(N_HEADS):
            hs = slice(h * HEAD_DIM, (h + 1) * HEAD_DIM)
            m_old = m_ref[h:h + 1, :]
            m_new = jnp.maximum(m_old, col_max[h])
            m_safe = jnp.where(m_new == NEG_INF, 0.0, m_new)
            p = jnp.exp2(lg_ref[h] - m_safe).astype(BF16)
            alpha = jnp.exp2(m_old - m_safe)
            m_ref[h:h + 1, :] = m_new
            vh = jnp.concatenate([vt_ref[0, hs, pl.ds(r0, kg)], ones_rows], axis=0)
            acc_ref[h] = alpha * acc_ref[h] + jnp.dot(vh, p, preferred_element_type=F32)

    n_far = jnp.maximum(ngrp - 2, 0)

    def _attend_pair(gp, c):
        attend(2 * gp, None)
        attend(2 * gp + 1, None)
        return c

    lax.fori_loop(0, n_far // 2, _attend_pair, 0)

    @pl.when(n_far % 2 == 1)
    def _():
        attend(n_far - 1, None)

    @pl.when(jq > 0)
    def _():
        attend(ngrp - 2, 0)

    attend(ngrp - 1, 1)

    for h in range(N_HEADS):
        o = acc_ref[h, 0:HEAD_DIM, :] / acc_ref[h, HEAD_DIM:HEAD_DIM + 1, :]
        y_ref[0, :, h * HEAD_DIM:(h + 1) * HEAD_DIM] = o.T.astype(BF16)


def _attention(qt, qit, wit, k, vt, ki, rel_bias):
    bsz, d, s = qt.shape
    tq = Q_TILE
    bkt = jnp.asarray(_near_bucket_ids())
    once = lambda shape: pl.BlockSpec(shape, lambda b, j: (b,) + (0,) * (len(shape) - 1),
                                      pipeline_mode=pl.Buffered(1))
    col = lambda r: pl.BlockSpec((1, r, tq), lambda b, j: (b, 0, j))
    return pl.pallas_call(
        _attn_kernel,
        grid=(bsz, s // tq),
        in_specs=[pl.BlockSpec(memory_space=pltpu.SMEM),
                  pl.BlockSpec(bkt.shape, lambda b, j: (0, 0, 0)),
                  col(d), col(d), col(IDX_HEADS),
                  once((1, s, d)), once((1, d, s)), once((1, s, IDX_DIM))],
        out_specs=pl.BlockSpec((1, tq, d), lambda b, j: (b, j, 0)),
        out_shape=jax.ShapeDtypeStruct((bsz, s, d), BF16),
        scratch_shapes=[pltpu.VMEM((s, tq), F32),
                        pltpu.VMEM((N_HEADS, 2, KEY_GROUP, tq), F32),
                        pltpu.VMEM((N_HEADS, HEAD_DIM + DEN_ROWS, tq), F32),
                        pltpu.VMEM((N_HEADS, tq), F32),
                        pltpu.VMEM((N_HEADS, KEY_GROUP, tq), F32)],
        compiler_params=_cparams("arbitrary", "arbitrary"),
        name="attn",
    )(rel_bias, bkt, qt, qit, wit, k, vt, ki)


def _pack_bf16_pair(x):
    c = x.shape[1] // 2
    lo = lax.bitcast_convert_type(x[:, :c].astype(BF16).astype(F32), jnp.uint32) >> 16
    hi = lax.bitcast_convert_type(x[:, c:].astype(BF16).astype(F32), jnp.uint32) & jnp.uint32(0xFFFF0000)
    return lo | hi


def _unpack_bf16_pair(p):
    a = lax.bitcast_convert_type(p << 16, F32)
    b = lax.bitcast_convert_type(p & jnp.uint32(0xFFFF0000), F32)
    return a, b


def _unpack_rows_bf16(p):
    a, b = _unpack_bf16_pair(p)
    return jnp.concatenate([a, b], axis=1).astype(BF16)


def _merge_kernel(yr_ref, ya_ref, glr_ref, gla_ref, x_ref, mod_ref, g_ref,
                  wr_ref, wa_ref, wo_ref, x1_ref, h2p_ref):
    merged = (jax.nn.sigmoid(glr_ref[0]) * jnp.dot(yr_ref[0], wr_ref[...], preferred_element_type=F32)
              + jax.nn.sigmoid(gla_ref[0]) * jnp.dot(ya_ref[0], wa_ref[...], preferred_element_type=F32))
    y = _bdot(merged, wo_ref[...])
    x1 = x_ref[0] + mod_ref[0, 2:3, :] * _rms(y, g_ref[1:2, :])
    x1_ref[0] = x1
    h2 = _rms(x1, g_ref[2:3, :]) * (1.0 + mod_ref[0, 4:5, :]) + mod_ref[0, 3:4, :]
    h2p_ref[0] = _pack_bf16_pair(h2)


def _merge(y_rnn, y_attn, gl_rnn, gl_attn, x, mod3, gains, w_br_rnn, w_br_attn, w_out):
    bsz, s, d = x.shape
    tm = ROW_TILE
    row = pl.BlockSpec((1, tm, d), lambda b, i: (b, i, 0))
    half = pl.BlockSpec((1, tm, d // 2), lambda b, i: (b, i, 0))
    c2 = lambda shape: pl.BlockSpec(shape, lambda b, i: (0,) * len(shape))
    return pl.pallas_call(
        _merge_kernel,
        grid=(bsz, s // tm),
        in_specs=[row, row, row, row, row,
                  pl.BlockSpec((1, N_MOD, d), lambda b, i: (b, 0, 0)),
                  c2(gains.shape), c2((d, d)), c2((d, d)), c2((d, d))],
        out_specs=(row, half),
        out_shape=(jax.ShapeDtypeStruct((bsz, s, d), F32),
                   jax.ShapeDtypeStruct((bsz, s, d // 2), jnp.uint32)),
        compiler_params=_cparams("arbitrary", "arbitrary"),
        name="merge",
    )(y_rnn, y_attn, gl_rnn, gl_attn, x, mod3, gains,
      w_br_rnn.astype(BF16), w_br_attn.astype(BF16), w_out.astype(BF16))


def _router_kernel(hp_ref, wr_ref, rb_ref, dest_ref, wk_ref, seg_ref):
    t = hp_ref.shape[0]
    gsz = N_EXPERTS // N_GROUPS
    h = _unpack_rows_bf16(hp_ref[...])
    s = jax.nn.sigmoid(_bdot_nt(wr_ref[...], h))
    s_sel = s + rb_ref[...]
    g3 = s_sel.reshape(N_GROUPS, gsz, t)
    e_in_g = lax.broadcasted_iota(jnp.int32, (N_GROUPS, gsz, t), 1)
    top1 = jnp.max(g3, axis=1, keepdims=True)
    first = jnp.min(jnp.where(g3 == top1, e_in_g, gsz), axis=1, keepdims=True)
    top2 = jnp.max(jnp.where(e_in_g == first, NEG_INF, g3), axis=1, keepdims=True)
    gscore = jnp.broadcast_to(top1 + top2, (N_GROUPS, gsz, t))
    gi = lax.broadcasted_iota(jnp.int32, (N_GROUPS, gsz, t), 0)
    gmask = jnp.zeros((N_GROUPS, gsz, t), F32)
    for _ in range(TOPK_GROUPS):
        mx = jnp.max(gscore, axis=0, keepdims=True)
        pick = jnp.min(jnp.where(gscore == mx, gi, N_GROUPS), axis=0, keepdims=True)
        hit = gi == pick
        gmask = jnp.where(hit, 1.0, gmask)
        gscore = jnp.where(hit, NEG_INF, gscore)
    cand = jnp.where(gmask.reshape(N_EXPERTS, t) > 0.0, s_sel, NEG_INF)
    ei = lax.broadcasted_iota(jnp.int32, (N_EXPERTS, t), 0)
    sel = jnp.zeros((N_EXPERTS, t), F32)
    picks = []
    for _ in range(TOP_K):
        mx = jnp.max(cand, axis=0, keepdims=True)
        pick = jnp.min(jnp.where(cand == mx, ei, N_EXPERTS), axis=0, keepdims=True)
        hit = ei == pick
        sel = jnp.where(hit, 1.0, sel)
        cand = jnp.where(hit, NEG_INF, cand)
        picks.append(pick)
    w = s * sel
    w = w / jnp.sum(w, axis=0, keepdims=True) * ROUTED_SCALE

    selb = sel.astype(BF16)
    tok_r = lax.broadcasted_iota(jnp.int32, (t, t), 0)
    tok_c = lax.broadcasted_iota(jnp.int32, (t, t), 1)
    rank = jnp.dot(selb, jnp.where(tok_r < tok_c, 1.0, 0.0).astype(BF16), preferred_element_type=F32)
    e_r = lax.broadcasted_iota(jnp.int32, (N_EXPERTS, N_EXPERTS), 0)
    e_c = lax.broadcasted_iota(jnp.int32, (N_EXPERTS, N_EXPERTS), 1)
    cnt_col = jnp.sum(sel, axis=1, keepdims=True)
    pad_col = jnp.floor((cnt_col + (SUBLANES - 1.0)) * (1.0 / SUBLANES))
    off_col = SUBLANES * jnp.dot(jnp.where(e_c < e_r, 1.0, 0.0).astype(BF16),
                                 jnp.broadcast_to(pad_col, (N_EXPERTS, LANES)).astype(BF16),
                                 preferred_element_type=F32)[:, 0:1]
    slot = rank + off_col
    dest_rows, w_rows = [], []
    for k in range(TOP_K):
        hit = ei == picks[k]
        dest_rows.append(jnp.sum(jnp.where(hit, slot, 0.0), axis=0, keepdims=True))
        w_rows.append(jnp.sum(jnp.where(hit, w, 0.0), axis=0, keepdims=True))
    dest_ref[0] = jnp.concatenate(dest_rows, axis=0).astype(jnp.int32)
    wk_ref[...] = jnp.concatenate(w_rows, axis=0)
    cnt_row = _bdot_nt(jnp.ones((SUBLANES, t), BF16), selb)
    pad_row = jnp.floor((cnt_row + (SUBLANES - 1.0)) * (1.0 / SUBLANES))
    off_row = SUBLANES * jnp.dot(pad_row.astype(BF16), jnp.where(e_r < e_c, 1.0, 0.0).astype(BF16),
                                 preferred_element_type=F32)
    seg_ref[0] = jnp.concatenate([off_row[0:1], cnt_row[0:1]], axis=1).astype(jnp.int32)


def _router(h2p, w_router, router_bias):
    n, dh = h2p.shape
    t = MOE_TILE
    nt = n // t
    return pl.pallas_call(
        _router_kernel,
        grid=(nt,),
        in_specs=[pl.BlockSpec((t, dh), lambda i: (i, 0)),
                  pl.BlockSpec((N_EXPERTS, 2 * dh), lambda i: (0, 0)),
                  pl.BlockSpec((N_EXPERTS, 1), lambda i: (0, 0))],
        out_specs=(pl.BlockSpec((1, TOP_K, t), lambda i: (i, 0, 0)),
                   pl.BlockSpec((TOP_K, t), lambda i: (0, i)),
                   pl.BlockSpec((1, 1, 2 * N_EXPERTS), lambda i: (i, 0, 0))),
        out_shape=(jax.ShapeDtypeStruct((nt, TOP_K, t), jnp.int32),
                   jax.ShapeDtypeStruct((TOP_K, n), F32),
                   jax.ShapeDtypeStruct((nt, 1, 2 * N_EXPERTS), jnp.int32)),
        compiler_params=_cparams("arbitrary"),
        name="router",
    )(h2p, w_router.T.astype(BF16), router_bias.reshape(N_EXPERTS, 1))


def _swiglu(x, wgu, wd):
    gu = jnp.dot(x, wgu, preferred_element_type=F32)
    g = gu[:, :D_EXPERT]
    a = (g * jax.nn.sigmoid(g)) * gu[:, D_EXPERT:]
    return jnp.dot(a.astype(BF16), wd, preferred_element_type=F32)


def _moe_kernel(seg_ref, hp_ref, dest_ref, wt_ref, wgu_ref, wd_ref, wsgu_ref, wsd_ref, x1_ref,
                mod_ref, g_ref, o_ref, xs_ref, stage_ref, dest_smem, sem):
    i = pl.program_id(0)
    s = pl.program_id(1)
    n_s = pl.num_programs(1)
    t = MOE_TILE
    half = xs_ref.shape[1]

    @pl.when((i == 0) & (s == 0))
    def _():
        xs_ref[...] = jnp.zeros_like(xs_ref)

    @pl.when(s == 0)
    def _():
        cp = pltpu.make_async_copy(dest_ref, dest_smem, sem)
        cp.start()
        cp.wait()

        def body(nb, c):
            n0 = nb * MOVE_TOKENS
            base = n0 * TOP_K
            for j in range(MOVE_TOKENS):
                row = hp_ref[pl.ds(n0 + j, 1), :]
                for k in range(TOP_K):
                    xs_ref[pl.ds(dest_smem[base + (j * TOP_K + k)], 1), :] = row
            return c

        lax.fori_loop(0, t // MOVE_TOKENS, body, 0)

    def put_rows(r0, end, y, old):
        rows = r0 + lax.broadcasted_iota(jnp.int32, (FFN_ROWS, half), 0)
        xs_ref[pl.ds(r0, FFN_ROWS), :] = jnp.where(rows < end, _pack_bf16_pair(y), old)

    @pl.when(s < n_s - 1)
    def _():
        offs, ends = [], []
        for eb in range(EXPERTS_PER_STEP):
            e = s * EXPERTS_PER_STEP + eb
            offs.append(pl.multiple_of(seg_ref[i, e], SUBLANES))
            ends.append(offs[eb] + seg_ref[i, N_EXPERTS + e])
        olds = [xs_ref[pl.ds(offs[eb], FFN_ROWS), :] for eb in range(EXPERTS_PER_STEP)]
        ys = [_swiglu(_unpack_rows_bf16(olds[eb]), wgu_ref[eb], wd_ref[eb])
              for eb in range(EXPERTS_PER_STEP)]
        for eb in range(EXPERTS_PER_STEP):
            put_rows(offs[eb], ends[eb], ys[eb], olds[eb])
        for eb in range(EXPERTS_PER_STEP):
            def chunk(c, carry, eb=eb):
                r0 = pl.multiple_of(offs[eb] + c * FFN_ROWS, SUBLANES)
                old = xs_ref[pl.ds(r0, FFN_ROWS), :]
                put_rows(r0, ends[eb], _swiglu(_unpack_rows_bf16(old), wgu_ref[eb], wd_ref[eb]), old)
                return carry

            n_chunks = lax.div(ends[eb] - offs[eb] + (FFN_ROWS - 1), FFN_ROWS)
            lax.fori_loop(1, n_chunks, chunk, 0)

    @pl.when(s == n_s - 1)
    def _():
        o_ref[...] = _swiglu(_unpack_rows_bf16(hp_ref[...]), wsgu_ref[...], wsd_ref[...])

        def body(nb, c):
            n0 = pl.multiple_of(nb * SUBLANES, SUBLANES)
            base = n0 * TOP_K
            for j in range(SUBLANES):
                for k in range(TOP_K):
                    stage_ref[k, pl.ds(j, 1), :] = xs_ref[pl.ds(dest_smem[base + (j * TOP_K + k)], 1), :]
            wrow = wt_ref[pl.ds(n0, SUBLANES), :]
            acc_a = jnp.zeros((SUBLANES, half), F32)
            acc_b = jnp.zeros((SUBLANES, half), F32)
            for k in range(TOP_K):
                a, b = _unpack_bf16_pair(stage_ref[k])
                acc_a = acc_a + a * wrow[:, k:k + 1]
                acc_b = acc_b + b * wrow[:, k:k + 1]
            o_ref[pl.ds(n0, SUBLANES), :] += jnp.concatenate([acc_a, acc_b], axis=1)
            return c

        lax.fori_loop(0, t // SUBLANES, body, 0)
        o_ref[...] = x1_ref[...] + mod_ref[0, 5:6, :] * _rms(o_ref[...], g_ref[...])


def _moe(h2p, dest, wt, seg, wgu, wd, wsgu, wsd, x1, mod3, g3, seq):
    n, dh = h2p.shape
    d = 2 * dh
    t = MOE_TILE
    n_steps = N_EXPERTS // EXPERTS_PER_STEP + 1
    tiles_per_seq = seq // t
    tok = lambda w: pl.BlockSpec((t, w), lambda i, s: (i, 0))
    wblk = lambda r, c: pl.BlockSpec((EXPERTS_PER_STEP, r, c),
                                     lambda i, s: (jnp.minimum(s, n_steps - 2), 0, 0))
    c2 = lambda shape: pl.BlockSpec(shape, lambda i, s: (0,) * len(shape))
    xs_rows = TOP_K * t + N_EXPERTS * SUBLANES + FFN_ROWS
    return pl.pallas_call(
        _moe_kernel,
        grid=(n // t, n_steps),
        in_specs=[pl.BlockSpec(memory_space=pltpu.SMEM),
                  tok(dh),
                  pl.BlockSpec((TOP_K * t,), lambda i, s: (i,)),
                  tok(TOP_K),
                  wblk(d, 2 * D_EXPERT), wblk(D_EXPERT, d),
                  c2((d, 2 * D_EXPERT)), c2((D_EXPERT, d)),
                  tok(d),
                  pl.BlockSpec((1, N_MOD, d), lambda i, s: (i // tiles_per_seq, 0, 0)),
                  c2((1, d))],
        out_specs=tok(d),
        out_shape=jax.ShapeDtypeStruct((n, d), F32),
        scratch_shapes=[pltpu.VMEM((xs_rows, dh), jnp.uint32),
                        pltpu.VMEM((TOP_K, SUBLANES, dh), jnp.uint32),
                        pltpu.SMEM((TOP_K * t,), jnp.int32),
                        pltpu.SemaphoreType.DMA(())],
        compiler_params=_cparams("arbitrary", "arbitrary"),
        name="moe",
    )(seg, h2p, dest, wt, wgu, wd, wsgu, wsd, x1, mod3, g3)


def kernel(x, c, w_ada, b_ada, norm_gain, w_in, conv_w, conv_b, w_rg_a, b_rg_a, w_rg_x, b_rg_x,
           lru_lambda, w_br_rnn, w_br_attn, w_out, rel_bias, w_router, router_bias,
           w_exp_gate, w_exp_up, w_exp_down, w_sh_gate, w_sh_up, w_sh_down):
    bsz, s, d = x.shape
    depth = w_ada.shape[0]
    for l in range(depth):
        mod = _ada(c, w_ada[l], b_ada[l])
        mod3 = mod.reshape(bsz, N_MOD, d)
        gains = norm_gain[l]
        (u_rnn, u_gate, k, gl_rnn, gl_attn, ki, qt, vt, qit, wit) = _in_proj(x, mod3, gains[0:1], w_in[l])
        y_rnn = _rglru(u_rnn, u_gate, conv_w[l], conv_b[l], w_rg_a[l], b_rg_a[l], w_rg_x[l],
                       b_rg_x[l], lru_lambda[l])
        y_attn = _attention(qt, qit, wit, k, vt, ki, rel_bias)
        x1, h2p = _merge(y_rnn, y_attn, gl_rnn, gl_attn, x, mod3, gains, w_br_rnn[l], w_br_attn[l],
                         w_out[l])
        h2p = h2p.reshape(bsz * s, d // 2)
        dest, wk, seg = _router(h2p, w_router[l], router_bias[l])
        wgu = jnp.concatenate([w_exp_gate[l], w_exp_up[l]], axis=-1).astype(BF16)
        wsgu = jnp.concatenate([w_sh_gate[l], w_sh_up[l]], axis=-1).astype(BF16)
        dest = jnp.transpose(dest, (0, 2, 1)).reshape(-1)
        x = _moe(h2p, dest, wk.T, seg.reshape(-1, 2 * N_EXPERTS), wgu, w_exp_down[l].astype(BF16),
                 wsgu, w_sh_down[l].astype(BF16), x1.reshape(bsz * s, d), mod3, gains[3:4],
                 s).reshape(bsz, s, d)
    return x
```

```python
import functools
import math

import jax
import jax.numpy as jnp
import numpy as np
from jax import lax
from jax.experimental import pallas as pl
from jax.experimental.pallas import tpu as pltpu

F32 = jnp.float32
BF16 = jnp.bfloat16

D_MODEL = 1024
RNN_BLOCKS = 8
RNN_BW = D_MODEL // RNN_BLOCKS
CONV_W = 4
LRU_C = 8.0
N_HEADS = 8
HEAD_DIM = 128
IDX_HEADS = 16
IDX_DIM = 64
TOPK_MAX = 256
N_BUCKETS = 32
MAX_DIST = 128
N_EXPERTS = 64
TOP_K = 8
N_GROUPS = 8
TOPK_GROUPS = 4
D_EXPERT = 256
ROUTED_SCALE = 2.5
N_MOD = 6
EPS = 1e-6

LANES = 128
SUBLANES = 8
VMEM_LIMIT_BYTES = 58 * 1024 * 1024

ROW_TILE = 512
UNCHECKED_BISECT_STEPS = 12
SCAN_TILE = 512
Q_TILE = 256
KEY_SUB = 128
KEY_GROUP = 256
DEN_ROWS = 16
MOE_TILE = 1024
EXPERTS_PER_STEP = 4
FFN_ROWS = 160
MOVE_TOKENS = 4

NEG_INF = float("-inf")
LOG2E = math.log2(math.e)


def _cparams(*sem):
    return pltpu.CompilerParams(dimension_semantics=sem, vmem_limit_bytes=VMEM_LIMIT_BYTES)


def _bdot(a, b):
    return jnp.dot(a.astype(BF16), b.astype(BF16), preferred_element_type=F32)


def _bdot_nt(a, b):
    return lax.dot_general(a.astype(BF16), b.astype(BF16), (((1,), (1,)), ((), ())),
                           preferred_element_type=F32)


def _rms(x, g):
    ms = jnp.mean(x * x, axis=-1, keepdims=True)
    return x * lax.rsqrt(ms + EPS) * g


def _ada_kernel(c_ref, w_ref, b_ref, o_ref):
    c = c_ref[...]
    cond = c * jax.nn.sigmoid(c)
    o_ref[...] = _bdot(cond, w_ref[...]) + b_ref[...]


def _ada(c, w_ada, b_ada):
    bsz, d = c.shape
    n = w_ada.shape[1]
    tn = 1024
    return pl.pallas_call(
        _ada_kernel,
        grid=(n // tn,),
        in_specs=[pl.BlockSpec((bsz, d), lambda j: (0, 0)),
                  pl.BlockSpec((d, tn), lambda j: (0, j)),
                  pl.BlockSpec((1, tn), lambda j: (0, j))],
        out_specs=pl.BlockSpec((bsz, tn), lambda j: (0, j)),
        out_shape=jax.ShapeDtypeStruct((bsz, n), F32),
        compiler_params=_cparams("arbitrary"),
        name="ada",
    )(c, w_ada, b_ada.reshape(1, n))


def _inproj_kernel(x_ref, mod_ref, g_ref, wn_ref, wki_ref, wt_ref, wwi_ref,
                   urnn_ref, ugate_ref, k_ref, glr_ref, gla_ref, ki_ref,
                   qt_ref, vt_ref, qit_ref, wit_ref):
    d = D_MODEL
    x = x_ref[0]
    h = _rms(x, g_ref[...]) * (1.0 + mod_ref[0, 1:2, :]) + mod_ref[0, 0:1, :]
    hb = h.astype(BF16)
    urnn_ref[0] = jnp.dot(hb, wn_ref[:, 0 * d:1 * d], preferred_element_type=F32)
    ugate_ref[0] = jnp.dot(hb, wn_ref[:, 1 * d:2 * d], preferred_element_type=F32)
    k_ref[0] = jnp.dot(hb, wn_ref[:, 2 * d:3 * d], preferred_element_type=F32).astype(BF16)
    glr_ref[0] = jnp.dot(hb, wn_ref[:, 3 * d:4 * d], preferred_element_type=F32)
    gla_ref[0] = jnp.dot(hb, wn_ref[:, 4 * d:5 * d], preferred_element_type=F32)
    ki_ref[0] = jnp.dot(hb, wki_ref[...], preferred_element_type=F32).astype(BF16)
    nt = (((1,), (1,)), ((), ()))
    qt_ref[0] = (lax.dot_general(wt_ref[0 * d:1 * d, :], hb, nt, preferred_element_type=F32)
                 * (HEAD_DIM ** -0.5 * LOG2E)).astype(BF16)
    vt_ref[0] = lax.dot_general(wt_ref[1 * d:2 * d, :], hb, nt, preferred_element_type=F32).astype(BF16)
    qit_ref[0] = lax.dot_general(wt_ref[2 * d:3 * d, :], hb, nt, preferred_element_type=F32).astype(BF16)
    wit_ref[0] = lax.dot_general(wwi_ref[...], hb, nt, preferred_element_type=F32)


def _in_proj(x, mod3, g0, w_in):
    bsz, s, d = x.shape
    tm = ROW_TILE
    offs = np.cumsum([0, d, d, d, d, d, IDX_HEADS * IDX_DIM, IDX_DIM, IDX_HEADS, d, d])
    seg = lambda i: w_in[:, int(offs[i]):int(offs[i + 1])]
    wn = jnp.concatenate([seg(0), seg(1), seg(3), seg(8), seg(9)], axis=1).astype(BF16)
    wki = seg(6).astype(BF16)
    wt = jnp.concatenate([seg(2), seg(4), seg(5)], axis=1).T.astype(BF16)
    wwi = seg(7).T.astype(BF16)
    const = lambda shape: pl.BlockSpec(shape, lambda b, i: (0,) * len(shape),
                                       pipeline_mode=pl.Buffered(1))
    row = lambda w: pl.BlockSpec((1, tm, w), lambda b, i: (b, i, 0))
    col = lambda r: pl.BlockSpec((1, r, tm), lambda b, i: (b, 0, i))
    out_shape = (
        jax.ShapeDtypeStruct((bsz, s, d), F32),
        jax.ShapeDtypeStruct((bsz, s, d), F32),
        jax.ShapeDtypeStruct((bsz, s, d), BF16),
        jax.ShapeDtypeStruct((bsz, s, d), F32),
        jax.ShapeDtypeStruct((bsz, s, d), F32),
        jax.ShapeDtypeStruct((bsz, s, IDX_DIM), BF16),
        jax.ShapeDtypeStruct((bsz, d, s), BF16),
        jax.ShapeDtypeStruct((bsz, d, s), BF16),
        jax.ShapeDtypeStruct((bsz, d, s), BF16),
        jax.ShapeDtypeStruct((bsz, IDX_HEADS, s), F32),
    )
    return pl.pallas_call(
        _inproj_kernel,
        grid=(bsz, s // tm),
        in_specs=[row(d),
                  pl.BlockSpec((1, N_MOD, d), lambda b, i: (b, 0, 0)),
                  pl.BlockSpec((1, d), lambda b, i: (0, 0)),
                  const(wn.shape), const(wki.shape), const(wt.shape), const(wwi.shape)],
        out_specs=(row(d), row(d), row(d), row(d), row(d), row(IDX_DIM),
                   col(d), col(d), col(d), col(IDX_HEADS)),
        out_shape=out_shape,
        compiler_params=_cparams("arbitrary", "arbitrary"),
        name="in_proj",
    )(x, mod3, g0, wn, wki, wt, wwi)


def _gelu_tanh(x):
    return 0.5 * x * (1.0 + jnp.tanh(math.sqrt(2.0 / math.pi) * (x + 0.044715 * (x * x * x))))


def _rglru_kernel(u_ref, ug_ref, cw_ref, cb_ref, wax_ref, ba_ref, bx_ref, lam_ref, y_ref,
                  ext_ref, a_ref, b_ref, carry_ref):
    ts = u_ref.shape[1]
    d = D_MODEL

    @pl.when(pl.program_id(1) == 0)
    def _():
        ext_ref[0:SUBLANES, :] = jnp.zeros((SUBLANES, d), F32)
        carry_ref[...] = jnp.zeros_like(carry_ref)

    ext_ref[SUBLANES:SUBLANES + ts, :] = u_ref[0]
    xc = cb_ref[...] + cw_ref[CONV_W - 1:CONV_W, :] * ext_ref[SUBLANES:SUBLANES + ts, :]
    for k in range(CONV_W - 1):
        off = SUBLANES - (CONV_W - 1) + k
        xc = xc + cw_ref[k:k + 1, :] * ext_ref[off:off + ts, :]
    ext_ref[0:SUBLANES, :] = ext_ref[ts:ts + SUBLANES, :]

    nl = -lam_ref[...]
    sp = jnp.maximum(nl, 0.0) + jnp.log1p(jnp.exp(-jnp.abs(nl)))
    for n in range(RNN_BLOCKS):
        cs = slice(n * RNN_BW, (n + 1) * RNN_BW)
        xb = xc[:, cs]
        g = _bdot(xb, wax_ref[n])
        r = jax.nn.sigmoid(g[:, :RNN_BW] + ba_ref[:, cs])
        i = jax.nn.sigmoid(g[:, RNN_BW:] + bx_ref[:, cs])
        log_a = (-LRU_C) * r * sp[:, cs]
        a_ref[:, cs] = jnp.exp(log_a)
        th = jnp.tanh(log_a)
        b_ref[:, cs] = jnp.sqrt(-2.0 * th / (1.0 - th)) * (i * xb)

    row = lax.broadcasted_iota(jnp.int32, (SUBLANES, d), 0)

    def group(gi, hprev):
        r0 = pl.multiple_of(gi * SUBLANES, SUBLANES)
        a = a_ref[pl.ds(r0, SUBLANES), :]
        b = b_ref[pl.ds(r0, SUBLANES), :]
        for sh in (1, 2, 4):
            keep = row >= sh
            a_s = jnp.where(keep, pltpu.roll(a, sh, 0), 1.0)
            b_s = jnp.where(keep, pltpu.roll(b, sh, 0), 0.0)
            b = a * b_s + b
            a = a * a_s
        h = b + a * hprev
        b_ref[pl.ds(r0, SUBLANES), :] = h
        return jnp.broadcast_to(h[SUBLANES - 1:SUBLANES, :], (SUBLANES, d))

    carry_ref[...] = lax.fori_loop(0, ts // SUBLANES, group, carry_ref[...])
    y_ref[0] = (b_ref[...] * _gelu_tanh(ug_ref[0])).astype(BF16)


def _rglru(u_rnn, u_gate, conv_w, conv_b, w_rg_a, b_rg_a, w_rg_x, b_rg_x, lam):
    bsz, s, d = u_rnn.shape
    ts = SCAN_TILE
    wax = jnp.concatenate([w_rg_a, w_rg_x], axis=-1).astype(BF16)
    vec = lambda v: v.reshape(1, d)
    c2 = lambda shape: pl.BlockSpec(shape, lambda b, i: (0,) * len(shape))
    row = pl.BlockSpec((1, ts, d), lambda b, i: (b, i, 0))
    return pl.pallas_call(
        _rglru_kernel,
        grid=(bsz, s // ts),
        in_specs=[row, row, c2((CONV_W, d)), c2((1, d)), c2(wax.shape), c2((1, d)), c2((1, d)),
                  c2((1, d))],
        out_specs=row,
        out_shape=jax.ShapeDtypeStruct((bsz, s, d), BF16),
        scratch_shapes=[pltpu.VMEM((ts + SUBLANES, d), F32), pltpu.VMEM((ts, d), F32),
                        pltpu.VMEM((ts, d), F32), pltpu.VMEM((SUBLANES, d), F32)],
        compiler_params=_cparams("arbitrary", "arbitrary"),
        name="rglru",
    )(u_rnn, u_gate, conv_w, vec(conv_b), wax, vec(b_rg_a), vec(b_rg_x), vec(lam))


def _t5_bucket_np(dist):
    max_exact = N_BUCKETS // 2
    dd = np.maximum(dist, 0)
    df = np.maximum(dd, 1).astype(np.float32)
    large = max_exact + (np.log(df / np.float32(max_exact)) / np.float32(math.log(MAX_DIST / max_exact))
                         * np.float32(N_BUCKETS - max_exact)).astype(np.int32)
    large = np.minimum(large, N_BUCKETS - 1)
    return np.where(dd < max_exact, dd, large)


def _near_bucket_ids():
    r = np.arange(KEY_GROUP)[None, :, None]
    c = np.arange(Q_TILE)[None, None, :]
    o = np.arange(2)[:, None, None]
    return _t5_bucket_np(c - r - (o - 1) * KEY_GROUP).astype(np.int32)


def _attn_kernel(rb_ref, bkt_ref, qt_ref, qit_ref, wit_ref, k_ref, vt_ref, ki_ref, y_ref,
                 s_ref, tab_ref, acc_ref, m_ref, lg_ref):
    tq = Q_TILE
    ks = KEY_SUB
    kg = KEY_GROUP
    jq = pl.program_id(1)
    t0 = jq * tq
    ngrp = jq + 1
    lane_t = t0 + lax.broadcasted_iota(jnp.int32, (1, tq), 1)

    @pl.when((pl.program_id(0) == 0) & (jq == 0))
    def _():
        for o in range(2):
            for h in range(N_HEADS):
                tab_ref[h, o] = jnp.zeros((kg, tq), F32)

            def fill(b, c):
                hit = bkt_ref[o] == b
                for h in range(N_HEADS):
                    val = (rb_ref[b, h] - rb_ref[N_BUCKETS - 1, h]) * LOG2E
                    tab_ref[h, o] = jnp.where(hit, val, tab_ref[h, o])
                return c

            lax.fori_loop(0, N_BUCKETS - 1, fill, 0)

    wi = wit_ref[0] * (IDX_HEADS ** -0.5 * IDX_DIM ** -0.5)

    def score_sub(i, mnmx, masked):
        r0 = pl.multiple_of(i * ks, ks)
        kic = ki_ref[0, pl.ds(r0, ks), :]
        acc = jnp.zeros((ks, tq), F32)
        for h in range(IDX_HEADS):
            dts = jnp.dot(kic, qit_ref[0, h * IDX_DIM:(h + 1) * IDX_DIM, :],
                          preferred_element_type=F32)
            acc = acc + jnp.maximum(dts, 0.0) * wi[h:h + 1, :]
        lo_src = acc
        if masked:
            key_s = r0 + lax.broadcasted_iota(jnp.int32, (ks, tq), 0)
            causal = key_s <= lane_t
            acc = jnp.where(causal, acc, NEG_INF)
            lo_src = jnp.where(causal, acc, jnp.inf)
        s_ref[pl.ds(r0, ks), :] = acc
        mn, mx = mnmx
        mn = jnp.minimum(mn, jnp.min(lo_src.reshape(ks // SUBLANES, SUBLANES, tq), axis=0))
        mx = jnp.maximum(mx, jnp.max(acc.reshape(ks // SUBLANES, SUBLANES, tq), axis=0))
        return mn, mx

    mnmx = (jnp.full((SUBLANES, tq), jnp.inf, F32), jnp.full((SUBLANES, tq), NEG_INF, F32))
    mnmx = lax.fori_loop(0, ngrp - 1,
                         lambda g, c: score_sub(2 * g + 1, score_sub(2 * g, c, False), False), mnmx)
    mnmx = score_sub(2 * ngrp - 2, mnmx, True)
    mn8, mx8 = score_sub(2 * ngrp - 1, mnmx, True)
    smin = jnp.min(mn8, axis=0, keepdims=True)
    smax = jnp.max(mx8, axis=0, keepdims=True)

    n_causal = (lane_t + 1).astype(F32)
    k_eff = jnp.minimum(n_causal, float(TOPK_MAX))

    def count_rows(pred):
        part = kg // 8

        def body(g, c):
            r0 = pl.multiple_of(g * kg, kg)
            key_s = (r0 + lax.broadcasted_iota(jnp.int32, (kg, tq), 0)).astype(F32)
            ind = jnp.where(pred(s_ref[pl.ds(r0, kg), :], key_s), 1.0, 0.0)
            return c + jnp.sum(ind.reshape(8, part, tq), axis=0)

        c = lax.fori_loop(0, ngrp, body, jnp.zeros((part, tq), F32))
        return jnp.sum(c, axis=0, keepdims=True)

    def bis_step(st):
        it, lo, hi, c_lo, done = st
        first = (jnp.zeros((1, tq), F32) + jnp.where(it == 0, 1.0, 0.0)) > 0.0
        probe = jnp.where(first, smax, 0.5 * lo + 0.5 * hi)
        collapsed = ~first & ((probe <= lo) | (probe >= hi))
        cnt = count_rows(lambda blk, _: blk >= probe)
        ge = cnt >= k_eff
        upd = (done == 0.0) & ~collapsed
        lo_n = jnp.where(upd & ge, probe, lo)
        c_lo_n = jnp.where(upd & ge, cnt, c_lo)
        hi_n = jnp.where(upd & ~ge, probe, hi)
        fin = collapsed | (cnt == k_eff) | (first & ge)
        done_n = jnp.where(fin, 1.0, done)
        return it + 1, lo_n, hi_n, c_lo_n, done_n

    def n_open(st):
        return jnp.sum(1.0 - st[4])

    done0 = jnp.where(n_causal <= k_eff, 1.0, 0.0)
    st = lax.fori_loop(0, UNCHECKED_BISECT_STEPS, lambda _, s: bis_step(s),
                       (jnp.int32(0), smin, smax, n_causal, done0))
    st = lax.while_loop(lambda s: s[5] > 0.0,
                        lambda s: (lambda nxt: nxt + (n_open(nxt),))(bis_step(s[:5])),
                        st + (n_open(st),))
    thr, c_thr = st[1], st[3]

    tie_all = jnp.zeros((1, tq), F32) + (t0 + tq).astype(F32)

    def tie_limit():
        need = k_eff - count_rows(lambda blk, _: blk > thr)

        def body(_, st):
            lo, hi = st
            mid = jnp.floor(0.5 * (lo + hi))
            ok = count_rows(lambda blk, key_s: (blk == thr) & (key_s < mid)) >= need
            return jnp.where(ok, lo, mid), jnp.where(ok, mid, hi)

        n_steps = int(math.ceil(math.log2(s_ref.shape[0]))) + 1
        return lax.fori_loop(0, n_steps, body, (jnp.zeros((1, tq), F32), tie_all))[1]

    excess = jnp.sum(jnp.where(c_thr > k_eff, 1.0, 0.0))
    tie_lim = lax.cond(excess > 0.0, tie_limit, lambda: tie_all)

    def mask_body(g, c):
        r0 = pl.multiple_of(g * kg, kg)
        blk = s_ref[pl.ds(r0, kg), :]
        key_s = (r0 + lax.broadcasted_iota(jnp.int32, (kg, tq), 0)).astype(F32)
        sel = (blk > thr) | ((blk == thr) & (key_s < tie_lim))
        s_ref[pl.ds(r0, kg), :] = jnp.where(sel, 0.0, NEG_INF)
        return c

    lax.fori_loop(0, ngrp, mask_body, 0)

    m_ref[...] = jnp.full_like(m_ref, NEG_INF)
    acc_ref[...] = jnp.zeros_like(acc_ref)
    ones_rows = jnp.ones((DEN_ROWS, kg), BF16)

    def attend(g, near):
        r0 = pl.multiple_of(g * kg, kg)
        msk = s_ref[pl.ds(r0, kg), :]
        col_max = []
        for h in range(N_HEADS):
            hs = slice(h * HEAD_DIM, (h + 1) * HEAD_DIM)
            kh = k_ref[0, pl.ds(r0, kg), hs]
            lg = jnp.dot(kh, qt_ref[0, hs, :], preferred_element_type=F32) + msk
            if near is not None:
                lg = lg + tab_ref[h, near]
            lg_ref[h] = lg
            col_max.append(jnp.max(lg, axis=0, keepdims=True))
        for h in range(N_HEADS):
            hs = slice(h * HEAD_DIM, (h + 1) * HEAD_DIM)
            m_old = m_ref[h:h + 1, :]
            m_new = jnp.maximum(m_old, col_max[h])
            m_safe = jnp.where(m_new == NEG_INF, 0.0, m_new)
            p = jnp.exp2(lg_ref[h] - m_safe).astype(BF16)
            alpha = jnp.exp2(m_old - m_safe)
            m_ref[h:h + 1, :] = m_new
            vh = jnp.concatenate([vt_ref[0, hs, pl.ds(r0, kg)], ones_rows], axis=0)
            acc_ref[h] = alpha * acc_ref[h] + jnp.dot(vh, p, preferred_element_type=F32)

    n_far = jnp.maximum(ngrp - 2, 0)

    def _attend_pair(gp, c):
        attend(2 * gp, None)
        attend(2 * gp + 1, None)
        return c

    lax.fori_loop(0, n_far // 2, _attend_pair, 0)

    @pl.when(n_far % 2 == 1)
    def _():
        attend(n_far - 1, None)

    @pl.when(jq > 0)
    def _():
        attend(ngrp - 2, 0)

    attend(ngrp - 1, 1)

    for h in range(N_HEADS):
        o = acc_ref[h, 0:HEAD_DIM, :] / acc_ref[h, HEAD_DIM:HEAD_DIM + 1, :]
        y_ref[0, :, h * HEAD_DIM:(h + 1) * HEAD_DIM] = o.T.astype(BF16)


def _attention(qt, qit, wit, k, vt, ki, rel_bias):
    bsz, d, s = qt.shape
    tq = Q_TILE
    bkt = jnp.asarray(_near_bucket_ids())
    once = lambda shape: pl.BlockSpec(shape, lambda b, j: (b,) + (0,) * (len(shape) - 1),
                                      pipeline_mode=pl.Buffered(1))
    col = lambda r: pl.BlockSpec((1, r, tq), lambda b, j: (b, 0, j))
    return pl.pallas_call(
        _attn_kernel,
        grid=(bsz, s // tq),
        in_specs=[pl.BlockSpec(memory_space=pltpu.SMEM),
                  pl.BlockSpec(bkt.shape, lambda b, j: (0, 0, 0)),
                  col(d), col(d), col(IDX_HEADS),
                  once((1, s, d)), once((1, d, s)), once((1, s, IDX_DIM))],
        out_specs=pl.BlockSpec((1, tq, d), lambda b, j: (b, j, 0)),
        out_shape=jax.ShapeDtypeStruct((bsz, s, d), BF16),
        scratch_shapes=[pltpu.VMEM((s, tq), F32),
                        pltpu.VMEM((N_HEADS, 2, KEY_GROUP, tq), F32),
                        pltpu.VMEM((N_HEADS, HEAD_DIM + DEN_ROWS, tq), F32),
                        pltpu.VMEM((N_HEADS, tq), F32),
                        pltpu.VMEM((N_HEADS, KEY_GROUP, tq), F32)],
        compiler_params=_cparams("arbitrary", "arbitrary"),
        name="attn",
    )(rel_bias, bkt, qt, qit, wit, k, vt, ki)


def _pack_bf16_pair(x):
    c = x.shape[1] // 2
    lo = lax.bitcast_convert_type(x[:, :c].astype(BF16).astype(F32), jnp.uint32) >> 16
    hi = lax.bitcast_convert_type(x[:, c:].astype(BF16).astype(F32), jnp.uint32) & jnp.uint32(0xFFFF0000)
    return lo | hi


def _unpack_bf16_pair(p):
    a = lax.bitcast_convert_type(p << 16, F32)
    b = lax.bitcast_convert_type(p & jnp.uint32(0xFFFF0000), F32)
    return a, b


def _unpack_rows_bf16(p):
    a, b = _unpack_bf16_pair(p)
    return jnp.concatenate([a, b], axis=1).astype(BF16)


def _merge_kernel(yr_ref, ya_ref, glr_ref, gla_ref, x_ref, mod_ref, g_ref,
                  wr_ref, wa_ref, wo_ref, x1_ref, h2p_ref):
    merged = (jax.nn.sigmoid(glr_ref[0]) * jnp.dot(yr_ref[0], wr_ref[...], preferred_element_type=F32)
              + jax.nn.sigmoid(gla_ref[0]) * jnp.dot(ya_ref[0], wa_ref[...], preferred_element_type=F32))
    y = _bdot(merged, wo_ref[...])
    x1 = x_ref[0] + mod_ref[0, 2:3, :] * _rms(y, g_ref[1:2, :])
    x1_ref[0] = x1
    h2 = _rms(x1, g_ref[2:3, :]) * (1.0 + mod_ref[0, 4:5, :]) + mod_ref[0, 3:4, :]
    h2p_ref[0] = _pack_bf16_pair(h2)


def _merge(y_rnn, y_attn, gl_rnn, gl_attn, x, mod3, gains, w_br_rnn, w_br_attn, w_out):
    bsz, s, d = x.shape
    tm = ROW_TILE
    row = pl.BlockSpec((1, tm, d), lambda b, i: (b, i, 0))
    half = pl.BlockSpec((1, tm, d // 2), lambda b, i: (b, i, 0))
    c2 = lambda shape: pl.BlockSpec(shape, lambda b, i: (0,) * len(shape))
    return pl.pallas_call(
        _merge_kernel,
        grid=(bsz, s // tm),
        in_specs=[row, row, row, row, row,
                  pl.BlockSpec((1, N_MOD, d), lambda b, i: (b, 0, 0)),
                  c2(gains.shape), c2((d, d)), c2((d, d)), c2((d, d))],
        out_specs=(row, half),
        out_shape=(jax.ShapeDtypeStruct((bsz, s, d), F32),
                   jax.ShapeDtypeStruct((bsz, s, d // 2), jnp.uint32)),
        compiler_params=_cparams("arbitrary", "arbitrary"),
        name="merge",
    )(y_rnn, y_attn, gl_rnn, gl_attn, x, mod3, gains,
      w_br_rnn.astype(BF16), w_br_attn.astype(BF16), w_out.astype(BF16))


def _router_kernel(hp_ref, wr_ref, rb_ref, dest_ref, wk_ref, seg_ref):
    t = hp_ref.shape[0]
    gsz = N_EXPERTS // N_GROUPS
    h = _unpack_rows_bf16(hp_ref[...])
    s = jax.nn.sigmoid(_bdot_nt(wr_ref[...], h))
    s_sel = s + rb_ref[...]
    g3 = s_sel.reshape(N_GROUPS, gsz, t)
    e_in_g = lax.broadcasted_iota(jnp.int32, (N_GROUPS, gsz, t), 1)
    top1 = jnp.max(g3, axis=1, keepdims=True)
    first = jnp.min(jnp.where(g3 == top1, e_in_g, gsz), axis=1, keepdims=True)
    top2 = jnp.max(jnp.where(e_in_g == first, NEG_INF, g3), axis=1, keepdims=True)
    gscore = jnp.broadcast_to(top1 + top2, (N_GROUPS, gsz, t))
    gi = lax.broadcasted_iota(jnp.int32, (N_GROUPS, gsz, t), 0)
    gmask = jnp.zeros((N_GROUPS, gsz, t), F32)
    for _ in range(TOPK_GROUPS):
        mx = jnp.max(gscore, axis=0, keepdims=True)
        pick = jnp.min(jnp.where(gscore == mx, gi, N_GROUPS), axis=0, keepdims=True)
        hit = gi == pick
        gmask = jnp.where(hit, 1.0, gmask)
        gscore = jnp.where(hit, NEG_INF, gscore)
    cand = jnp.where(gmask.reshape(N_EXPERTS, t) > 0.0, s_sel, NEG_INF)
    ei = lax.broadcasted_iota(jnp.int32, (N_EXPERTS, t), 0)
    sel = jnp.zeros((N_EXPERTS, t), F32)
    picks = []
    for _ in range(TOP_K):
        mx = jnp.max(cand, axis=0, keepdims=True)
        pick = jnp.min(jnp.where(cand == mx, ei, N_EXPERTS), axis=0, keepdims=True)
        hit = ei == pick
        sel = jnp.where(hit, 1.0, sel)
        cand = jnp.where(hit, NEG_INF, cand)
        picks.append(pick)
    w = s * sel
    w = w / jnp.sum(w, axis=0, keepdims=True) * ROUTED_SCALE

    selb = sel.astype(BF16)
    tok_r = lax.broadcasted_iota(jnp.int32, (t, t), 0)
    tok_c = lax.broadcasted_iota(jnp.int32, (t, t), 1)
    rank = jnp.dot(selb, jnp.where(tok_r < tok_c, 1.0, 0.0).astype(BF16), preferred_element_type=F32)
    e_r = lax.broadcasted_iota(jnp.int32, (N_EXPERTS, N_EXPERTS), 0)
    e_c = lax.broadcasted_iota(jnp.int32, (N_EXPERTS, N_EXPERTS), 1)
    cnt_col = jnp.sum(sel, axis=1, keepdims=True)
    pad_col = jnp.floor((cnt_col + (SUBLANES - 1.0)) * (1.0 / SUBLANES))
    off_col = SUBLANES * jnp.dot(jnp.where(e_c < e_r, 1.0, 0.0).astype(BF16),
                                 jnp.broadcast_to(pad_col, (N_EXPERTS, LANES)).astype(BF16),
                                 preferred_element_type=F32)[:, 0:1]
    slot = rank + off_col
    dest_rows, w_rows = [], []
    for k in range(TOP_K):
        hit = ei == picks[k]
        dest_rows.append(jnp.sum(jnp.where(hit, slot, 0.0), axis=0, keepdims=True))
        w_rows.append(jnp.sum(jnp.where(hit, w, 0.0), axis=0, keepdims=True))
    dest_ref[0] = jnp.concatenate(dest_rows, axis=0).astype(jnp.int32)
    wk_ref[...] = jnp.concatenate(w_rows, axis=0)
    cnt_row = _bdot_nt(jnp.ones((SUBLANES, t), BF16), selb)
    pad_row = jnp.floor((cnt_row + (SUBLANES - 1.0)) * (1.0 / SUBLANES))
    off_row = SUBLANES * jnp.dot(pad_row.astype(BF16), jnp.where(e_r < e_c, 1.0, 0.0).astype(BF16),
                                 preferred_element_type=F32)
    seg_ref[0] = jnp.concatenate([off_row[0:1], cnt_row[0:1]], axis=1).astype(jnp.int32)


def _router(h2p, w_router, router_bias):
    n, dh = h2p.shape
    t = MOE_TILE
    nt = n // t
    return pl.pallas_call(
        _router_kernel,
        grid=(nt,),
        in_specs=[pl.BlockSpec((t, dh), lambda i: (i, 0)),
                  pl.BlockSpec((N_EXPERTS, 2 * dh), lambda i: (0, 0)),
                  pl.BlockSpec((N_EXPERTS, 1), lambda i: (0, 0))],
        out_specs=(pl.BlockSpec((1, TOP_K, t), lambda i: (i, 0, 0)),
                   pl.BlockSpec((TOP_K, t), lambda i: (0, i)),
                   pl.BlockSpec((1, 1, 2 * N_EXPERTS), lambda i: (i, 0, 0))),
        out_shape=(jax.ShapeDtypeStruct((nt, TOP_K, t), jnp.int32),
                   jax.ShapeDtypeStruct((TOP_K, n), F32),
                   jax.ShapeDtypeStruct((nt, 1, 2 * N_EXPERTS), jnp.int32)),
        compiler_params=_cparams("arbitrary"),
        name="router",
    )(h2p, w_router.T.astype(BF16), router_bias.reshape(N_EXPERTS, 1))


def _swiglu(x, wgu, wd):
    gu = jnp.dot(x, wgu, preferred_element_type=F32)
    g = gu[:, :D_EXPERT]
    a = (g * jax.nn.sigmoid(g)) * gu[:, D_EXPERT:]
    return jnp.dot(a.astype(BF16), wd, preferred_element_type=F32)


def _moe_kernel(seg_ref, hp_ref, dest_ref, wt_ref, wgu_ref, wd_ref, wsgu_ref, wsd_ref, x1_ref,
                mod_ref, g_ref, o_ref, xs_ref, stage_ref, dest_smem, sem):
    i = pl.program_id(0)
    s = pl.program_id(1)
    n_s = pl.num_programs(1)
    t = MOE_TILE
    half = xs_ref.shape[1]

    @pl.when((i == 0) & (s == 0))
    def _():
        xs_ref[...] = jnp.zeros_like(xs_ref)

    @pl.when(s == 0)
    def _():
        cp = pltpu.make_async_copy(dest_ref, dest_smem, sem)
        cp.start()
        cp.wait()

        def body(nb, c):
            n0 = nb * MOVE_TOKENS
            base = n0 * TOP_K
            for j in range(MOVE_TOKENS):
                row = hp_ref[pl.ds(n0 + j, 1), :]
                for k in range(TOP_K):
                    xs_ref[pl.ds(dest_smem[base + (j * TOP_K + k)], 1), :] = row
            return c

        lax.fori_loop(0, t // MOVE_TOKENS, body, 0)

    def put_rows(r0, end, y, old):
        rows = r0 + lax.broadcasted_iota(jnp.int32, (FFN_ROWS, half), 0)
        xs_ref[pl.ds(r0, FFN_ROWS), :] = jnp.where(rows < end, _pack_bf16_pair(y), old)

    @pl.when(s < n_s - 1)
    def _():
        offs, ends = [], []
        for eb in range(EXPERTS_PER_STEP):
            e = s * EXPERTS_PER_STEP + eb
            offs.append(pl.multiple_of(seg_ref[i, e], SUBLANES))
            ends.append(offs[eb] + seg_ref[i, N_EXPERTS + e])
        olds = [xs_ref[pl.ds(offs[eb], FFN_ROWS), :] for eb in range(EXPERTS_PER_STEP)]
        ys = [_swiglu(_unpack_rows_bf16(olds[eb]), wgu_ref[eb], wd_ref[eb])
              for eb in range(EXPERTS_PER_STEP)]
        for eb in range(EXPERTS_PER_STEP):
            put_rows(offs[eb], ends[eb], ys[eb], olds[eb])
        for eb in range(EXPERTS_PER_STEP):
            def chunk(c, carry, eb=eb):
                r0 = pl.multiple_of(offs[eb] + c * FFN_ROWS, SUBLANES)
                old = xs_ref[pl.ds(r0, FFN_ROWS), :]
                put_rows(r0, ends[eb], _swiglu(_unpack_rows_bf16(old), wgu_ref[eb], wd_ref[eb]), old)
                return carry

            n_chunks = lax.div(ends[eb] - offs[eb] + (FFN_ROWS - 1), FFN_ROWS)
            lax.fori_loop(1, n_chunks, chunk, 0)

    @pl.when(s == n_s - 1)
    def _():
        o_ref[...] = _swiglu(_unpack_rows_bf16(hp_ref[...]), wsgu_ref[...], wsd_ref[...])

        def body(nb, c):
            n0 = pl.multiple_of(nb * SUBLANES, SUBLANES)
            base = n0 * TOP_K
            for j in range(SUBLANES):
                for k in range(TOP_K):
                    stage_ref[k, pl.ds(j, 1), :] = xs_ref[pl.ds(dest_smem[base + (j * TOP_K + k)], 1), :]
            wrow = wt_ref[pl.ds(n0, SUBLANES), :]
            acc_a = jnp.zeros((SUBLANES, half), F32)
            acc_b = jnp.zeros((SUBLANES, half), F32)
            for k in range(TOP_K):
                a, b = _unpack_bf16_pair(stage_ref[k])
                acc_a = acc_a + a * wrow[:, k:k + 1]
                acc_b = acc_b + b * wrow[:, k:k + 1]
            o_ref[pl.ds(n0, SUBLANES), :] += jnp.concatenate([acc_a, acc_b], axis=1)
            return c

        lax.fori_loop(0, t // SUBLANES, body, 0)
        o_ref[...] = x1_ref[...] + mod_ref[0, 5:6, :] * _rms(o_ref[...], g_ref[...])


def _moe(h2p, dest, wt, seg, wgu, wd, wsgu, wsd, x1, mod3, g3, seq):
    n, dh = h2p.shape
    d = 2 * dh
    t = MOE_TILE
    n_steps = N_EXPERTS // EXPERTS_PER_STEP + 1
    tiles_per_seq = seq // t
    tok = lambda w: pl.BlockSpec((t, w), lambda i, s: (i, 0))
    wblk = lambda r, c: pl.BlockSpec((EXPERTS_PER_STEP, r, c),
                                     lambda i, s: (jnp.minimum(s, n_steps - 2), 0, 0))
    c2 = lambda shape: pl.BlockSpec(shape, lambda i, s: (0,) * len(shape))
    xs_rows = TOP_K * t + N_EXPERTS * SUBLANES + FFN_ROWS
    return pl.pallas_call(
        _moe_kernel,
        grid=(n // t, n_steps),
        in_specs=[pl.BlockSpec(memory_space=pltpu.SMEM),
                  tok(dh),
                  pl.BlockSpec((TOP_K * t,), lambda i, s: (i,)),
                  tok(TOP_K),
                  wblk(d, 2 * D_EXPERT), wblk(D_EXPERT, d),
                  c2((d, 2 * D_EXPERT)), c2((D_EXPERT, d)),
                  tok(d),
                  pl.BlockSpec((1, N_MOD, d), lambda i, s: (i // tiles_per_seq, 0, 0)),
                  c2((1, d))],
        out_specs=tok(d),
        out_shape=jax.ShapeDtypeStruct((n, d), F32),
        scratch_shapes=[pltpu.VMEM((xs_rows, dh), jnp.uint32),
                        pltpu.VMEM((TOP_K, SUBLANES, dh), jnp.uint32),
                        pltpu.SMEM((TOP_K * t,), jnp.int32),
                        pltpu.SemaphoreType.DMA(())],
        compiler_params=_cparams("arbitrary", "arbitrary"),
        name="moe",
    )(seg, h2p, dest, wt, wgu, wd, wsgu, wsd, x1, mod3, g3)


def kernel(x, c, w_ada, b_ada, norm_gain, w_in, conv_w, conv_b, w_rg_a, b_rg_a, w_rg_x, b_rg_x,
           lru_lambda, w_br_rnn, w_br_attn, w_out, rel_bias, w_router, router_bias,
           w_exp_gate, w_exp_up, w_exp_down, w_sh_gate, w_sh_up, w_sh_down):
    bsz, s, d = x.shape
    depth = w_ada.shape[0]
    for l in range(depth):
        mod = _ada(c, w_ada[l], b_ada[l])
        mod3 = mod.reshape(bsz, N_MOD, d)
        gains = norm_gain[l]
        (u_rnn, u_gate, k, gl_rnn, gl_attn, ki, qt, vt, qit, wit) = _in_proj(x, mod3, gains[0:1], w_in[l])
        y_rnn = _rglru(u_rnn, u_gate, conv_w[l], conv_b[l], w_rg_a[l], b_rg_a[l], w_rg_x[l],
                       b_rg_x[l], lru_lambda[l])
        y_attn = _attention(qt, qit, wit, k, vt, ki, rel_bias)
        x1, h2p = _merge(y_rnn, y_attn, gl_rnn, gl_attn, x, mod3, gains, w_br_rnn[l], w_br_attn[l],
                         w_out[l])
        h2p = h2p.reshape(bsz * s, d // 2)
        dest, wk, seg = _router(h2p, w_router[l], router_bias[l])
        wgu = jnp.concatenate([w_exp_gate[l], w_exp_up[l]], axis=-1).astype(BF16)
        wsgu = jnp.concatenate([w_sh_gate[l], w_sh_up[l]], axis=-1).astype(BF16)
        dest = jnp.transpose(dest, (0, 2, 1)).reshape(-1)
        x = _moe(h2p, dest, wk.T, seg.reshape(-1, 2 * N_EXPERTS), wgu, w_exp_down[l].astype(BF16),
                 wsgu, w_sh_down[l].astype(BF16), x1.reshape(bsz * s, d), mod3, gains[3:4],
                 s).reshape(bsz, s, d)
    return x
```

```python
import functools
import math

import jax
import jax.numpy as jnp
import numpy as np
from jax import lax
from jax.experimental import pallas as pl
from jax.experimental.pallas import tpu as pltpu

F32 = jnp.float32
BF16 = jnp.bfloat16

D_MODEL = 1024
RNN_BLOCKS = 8
RNN_BW = D_MODEL // RNN_BLOCKS
CONV_W = 4
LRU_C = 8.0
N_HEADS = 8
HEAD_DIM = 128
IDX_HEADS = 16
IDX_DIM = 64
TOPK_MAX = 256
N_BUCKETS = 32
MAX_DIST = 128
N_EXPERTS = 64
TOP_K = 8
N_GROUPS = 8
TOPK_GROUPS = 4
D_EXPERT = 256
ROUTED_SCALE = 2.5
N_MOD = 6
EPS = 1e-6

LANES = 128
SUBLANES = 8
VMEM_LIMIT_BYTES = 58 * 1024 * 1024

ROW_TILE = 512
UNCHECKED_BISECT_STEPS = 12
SCAN_TILE = 512
Q_TILE = 256
KEY_SUB = 128
KEY_GROUP = 256
DEN_ROWS = 16
MOE_TILE = 1024
EXPERTS_PER_STEP = 4
FFN_ROWS = 160
MOVE_TOKENS = 4
SLAB = 4

NEG_INF = float("-inf")
LOG2E = math.log2(math.e)


def _cparams(*sem):
    return pltpu.CompilerParams(dimension_semantics=sem, vmem_limit_bytes=VMEM_LIMIT_BYTES)


def _bdot(a, b):
    return jnp.dot(a.astype(BF16), b.astype(BF16), preferred_element_type=F32)


def _bdot_nt(a, b):
    return lax.dot_general(a.astype(BF16), b.astype(BF16), (((1,), (1,)), ((), ())),
                           preferred_element_type=F32)


def _rms(x, g):
    ms = jnp.mean(x * x, axis=-1, keepdims=True)
    return x * lax.rsqrt(ms + EPS) * g


def _ada_kernel(c_ref, w_ref, b_ref, o_ref):
    c = c_ref[...]
    cond = c * jax.nn.sigmoid(c)
    o_ref[...] = _bdot(cond, w_ref[...]) + b_ref[...]


def _ada(c, w_ada, b_ada):
    bsz, d = c.shape
    n = w_ada.shape[1]
    tn = 1024
    return pl.pallas_call(
        _ada_kernel,
        grid=(n // tn,),
        in_specs=[pl.BlockSpec((bsz, d), lambda j: (0, 0)),
                  pl.BlockSpec((d, tn), lambda j: (0, j)),
                  pl.BlockSpec((1, tn), lambda j: (0, j))],
        out_specs=pl.BlockSpec((bsz, tn), lambda j: (0, j)),
        out_shape=jax.ShapeDtypeStruct((bsz, n), F32),
        compiler_params=_cparams("arbitrary"),
        name="ada",
    )(c, w_ada, b_ada.reshape(1, n))


def _inproj_kernel(x_ref, mod_ref, g_ref, wn_ref, wki_ref, wt_ref, wwi_ref,
                   urnn_ref, ugate_ref, k_ref, glr_ref, gla_ref, ki_ref,
                   qt_ref, vt_ref, qit_ref, wit_ref):
    d = D_MODEL
    x = x_ref[0]
    h = _rms(x, g_ref[...]) * (1.0 + mod_ref[0, 1:2, :]) + mod_ref[0, 0:1, :]
    hb = h.astype(BF16)
    urnn_ref[0] = jnp.dot(hb, wn_ref[:, 0 * d:1 * d], preferred_element_type=F32)
    ugate_ref[0] = jnp.dot(hb, wn_ref[:, 1 * d:2 * d], preferred_element_type=F32)
    k_ref[0] = jnp.dot(hb, wn_ref[:, 2 * d:3 * d], preferred_element_type=F32).astype(BF16)
    glr_ref[0] = jnp.dot(hb, wn_ref[:, 3 * d:4 * d], preferred_element_type=F32)
    gla_ref[0] = jnp.dot(hb, wn_ref[:, 4 * d:5 * d], preferred_element_type=F32)
    ki_ref[0] = jnp.dot(hb, wki_ref[...], preferred_element_type=F32).astype(BF16)
    nt = (((1,), (1,)), ((), ()))
    qt_ref[0] = (lax.dot_general(wt_ref[0 * d:1 * d, :], hb, nt, preferred_element_type=F32)
                 * (HEAD_DIM ** -0.5 * LOG2E)).astype(BF16)
    vt_ref[0] = lax.dot_general(wt_ref[1 * d:2 * d, :], hb, nt, preferred_element_type=F32).astype(BF16)
    qit_ref[0] = lax.dot_general(wt_ref[2 * d:3 * d, :], hb, nt, preferred_element_type=F32).astype(BF16)
    wit_ref[0] = lax.dot_general(wwi_ref[...], hb, nt, preferred_element_type=F32)


def _in_proj(x, mod3, g0, w_in):
    bsz, s, d = x.shape
    tm = ROW_TILE
    offs = np.cumsum([0, d, d, d, d, d, IDX_HEADS * IDX_DIM, IDX_DIM, IDX_HEADS, d, d])
    seg = lambda i: w_in[:, int(offs[i]):int(offs[i + 1])]
    wn = jnp.concatenate([seg(0), seg(1), seg(3), seg(8), seg(9)], axis=1).astype(BF16)
    wki = seg(6).astype(BF16)
    wt = jnp.concatenate([seg(2), seg(4), seg(5)], axis=1).T.astype(BF16)
    wwi = seg(7).T.astype(BF16)
    const = lambda shape: pl.BlockSpec(shape, lambda b, i: (0,) * len(shape),
                                       pipeline_mode=pl.Buffered(1))
    row = lambda w: pl.BlockSpec((1, tm, w), lambda b, i: (b, i, 0))
    col = lambda r: pl.BlockSpec((1, r, tm), lambda b, i: (b, 0, i))
    out_shape = (
        jax.ShapeDtypeStruct((bsz, s, d), F32),
        jax.ShapeDtypeStruct((bsz, s, d), F32),
        jax.ShapeDtypeStruct((bsz, s, d), BF16),
        jax.ShapeDtypeStruct((bsz, s, d), F32),
        jax.ShapeDtypeStruct((bsz, s, d), F32),
        jax.ShapeDtypeStruct((bsz, s, IDX_DIM), BF16),
        jax.ShapeDtypeStruct((bsz, d, s), BF16),
        jax.ShapeDtypeStruct((bsz, d, s), BF16),
        jax.ShapeDtypeStruct((bsz, d, s), BF16),
        jax.ShapeDtypeStruct((bsz, IDX_HEADS, s), F32),
    )
    return pl.pallas_call(
        _inproj_kernel,
        grid=(bsz, s // tm),
        in_specs=[row(d),
                  pl.BlockSpec((1, N_MOD, d), lambda b, i: (b, 0, 0)),
                  pl.BlockSpec((1, d), lambda b, i: (0, 0)),
                  const(wn.shape), const(wki.shape), const(wt.shape), const(wwi.shape)],
        out_specs=(row(d), row(d), row(d), row(d), row(d), row(IDX_DIM),
                   col(d), col(d), col(d), col(IDX_HEADS)),
        out_shape=out_shape,
        compiler_params=_cparams("arbitrary", "arbitrary"),
        name="in_proj",
    )(x, mod3, g0, wn, wki, wt, wwi)


def _gelu_tanh(x):
    return 0.5 * x * (1.0 + jnp.tanh(math.sqrt(2.0 / math.pi) * (x + 0.044715 * (x * x * x))))


def _rglru_kernel(u_ref, ug_ref, cw_ref, cb_ref, wax_ref, ba_ref, bx_ref, lam_ref, y_ref,
                  ext_ref, a_ref, b_ref, carry_ref):
    ts = u_ref.shape[1]
    d = D_MODEL

    @pl.when(pl.program_id(1) == 0)
    def _():
        ext_ref[0:SUBLANES, :] = jnp.zeros((SUBLANES, d), F32)
        carry_ref[...] = jnp.zeros_like(carry_ref)

    ext_ref[SUBLANES:SUBLANES + ts, :] = u_ref[0]
    xc = cb_ref[...] + cw_ref[CONV_W - 1:CONV_W, :] * ext_ref[SUBLANES:SUBLANES + ts, :]
    for k in range(CONV_W - 1):
        off = SUBLANES - (CONV_W - 1) + k
        xc = xc + cw_ref[k:k + 1, :] * ext_ref[off:off + ts, :]
    ext_ref[0:SUBLANES, :] = ext_ref[ts:ts + SUBLANES, :]

    nl = -lam_ref[...]
    sp = jnp.maximum(nl, 0.0) + jnp.log1p(jnp.exp(-jnp.abs(nl)))
    for n in range(RNN_BLOCKS):
        cs = slice(n * RNN_BW, (n + 1) * RNN_BW)
        xb = xc[:, cs]
        g = _bdot(xb, wax_ref[n])
        r = jax.nn.sigmoid(g[:, :RNN_BW] + ba_ref[:, cs])
        i = jax.nn.sigmoid(g[:, RNN_BW:] + bx_ref[:, cs])
        log_a = (-LRU_C) * r * sp[:, cs]
        a_ref[:, cs] = jnp.exp(log_a)
        th = jnp.tanh(log_a)
        b_ref[:, cs] = jnp.sqrt(-2.0 * th / (1.0 - th)) * (i * xb)

    row = lax.broadcasted_iota(jnp.int32, (SUBLANES, d), 0)

    def group(gi, hprev):
        r0 = pl.multiple_of(gi * SUBLANES, SUBLANES)
        a = a_ref[pl.ds(r0, SUBLANES), :]
        b = b_ref[pl.ds(r0, SUBLANES), :]
        for sh in (1, 2, 4):
            keep = row >= sh
            a_s = jnp.where(keep, pltpu.roll(a, sh, 0), 1.0)
            b_s = jnp.where(keep, pltpu.roll(b, sh, 0), 0.0)
            b = a * b_s + b
            a = a * a_s
        h = b + a * hprev
        b_ref[pl.ds(r0, SUBLANES), :] = h
        return jnp.broadcast_to(h[SUBLANES - 1:SUBLANES, :], (SUBLANES, d))

    carry_ref[...] = lax.fori_loop(0, ts // SUBLANES, group, carry_ref[...])
    y_ref[0] = (b_ref[...] * _gelu_tanh(ug_ref[0])).astype(BF16)


def _rglru(u_rnn, u_gate, conv_w, conv_b, w_rg_a, b_rg_a, w_rg_x, b_rg_x, lam):
    bsz, s, d = u_rnn.shape
    ts = SCAN_TILE
    wax = jnp.concatenate([w_rg_a, w_rg_x], axis=-1).astype(BF16)
    vec = lambda v: v.reshape(1, d)
    c2 = lambda shape: pl.BlockSpec(shape, lambda b, i: (0,) * len(shape))
    row = pl.BlockSpec((1, ts, d), lambda b, i: (b, i, 0))
    return pl.pallas_call(
        _rglru_kernel,
        grid=(bsz, s // ts),
        in_specs=[row, row, c2((CONV_W, d)), c2((1, d)), c2(wax.shape), c2((1, d)), c2((1, d)),
                  c2((1, d))],
        out_specs=row,
        out_shape=jax.ShapeDtypeStruct((bsz, s, d), BF16),
        scratch_shapes=[pltpu.VMEM((ts + SUBLANES, d), F32), pltpu.VMEM((ts, d), F32),
                        pltpu.VMEM((ts, d), F32), pltpu.VMEM((SUBLANES, d), F32)],
        compiler_params=_cparams("arbitrary", "arbitrary"),
        name="rglru",
    )(u_rnn, u_gate, conv_w, vec(conv_b), wax, vec(b_rg_a), vec(b_rg_x), vec(lam))


def _t5_bucket_np(dist):
    max_exact = N_BUCKETS // 2
    dd = np.maximum(dist, 0)
    df = np.maximum(dd, 1).astype(np.float32)
    large = max_exact + (np.log(df / np.float32(max_exact)) / np.float32(math.log(MAX_DIST / max_exact))
                         * np.float32(N_BUCKETS - max_exact)).astype(np.int32)
    large = np.minimum(large, N_BUCKETS - 1)
    return np.where(dd < max_exact, dd, large)


def _near_bucket_ids():
    r = np.arange(KEY_GROUP)[None, :, None]
    c = np.arange(Q_TILE)[None, None, :]
    o = np.arange(2)[:, None, None]
    return _t5_bucket_np(c - r - (o - 1) * KEY_GROUP).astype(np.int32)


def _attn_kernel(rb_ref, bkt_ref, qt_ref, qit_ref, wit_ref, k_ref, vt_ref, ki_ref, y_ref,
                 s_ref, tab_ref, acc_ref, m_ref, lg_ref):
    tq = Q_TILE
    ks = KEY_SUB
    kg = KEY_GROUP
    jq = pl.program_id(1)
    t0 = jq * tq
    ngrp = jq + 1
    lane_t = t0 + lax.broadcasted_iota(jnp.int32, (1, tq), 1)

    @pl.when((pl.program_id(0) == 0) & (jq == 0))
    def _():
        for o in range(2):
            for h in range(N_HEADS):
                tab_ref[h, o] = jnp.zeros((kg, tq), F32)

            def fill(b, c):
                hit = bkt_ref[o] == b
                for h in range(N_HEADS):
                    val = (rb_ref[b, h] - rb_ref[N_BUCKETS - 1, h]) * LOG2E
                    tab_ref[h, o] = jnp.where(hit, val, tab_ref[h, o])
                return c

            lax.fori_loop(0, N_BUCKETS - 1, fill, 0)

    wi = wit_ref[0] * (IDX_HEADS ** -0.5 * IDX_DIM ** -0.5)

    def score_sub(i, mnmx, masked):
        r0 = pl.multiple_of(i * ks, ks)
        kic = ki_ref[0, pl.ds(r0, ks), :]
        acc = jnp.zeros((ks, tq), F32)
        for h in range(IDX_HEADS):
            dts = jnp.dot(kic, qit_ref[0, h * IDX_DIM:(h + 1) * IDX_DIM, :],
                          preferred_element_type=F32)
            acc = acc + jnp.maximum(dts, 0.0) * wi[h:h + 1, :]
        lo_src = acc
        if masked:
            key_s = r0 + lax.broadcasted_iota(jnp.int32, (ks, tq), 0)
            causal = key_s <= lane_t
            acc = jnp.where(causal, acc, NEG_INF)
            lo_src = jnp.where(causal, acc, jnp.inf)
        s_ref[pl.ds(r0, ks), :] = acc
        mn, mx = mnmx
        mn = jnp.minimum(mn, jnp.min(lo_src.reshape(ks // SUBLANES, SUBLANES, tq), axis=0))
        mx = jnp.maximum(mx, jnp.max(acc.reshape(ks // SUBLANES, SUBLANES, tq), axis=0))
        return mn, mx

    mnmx = (jnp.full((SUBLANES, tq), jnp.inf, F32), jnp.full((SUBLANES, tq), NEG_INF, F32))
    mnmx = lax.fori_loop(0, ngrp - 1,
                         lambda g, c: score_sub(2 * g + 1, score_sub(2 * g, c, False), False), mnmx)
    mnmx = score_sub(2 * ngrp - 2, mnmx, True)
    mn8, mx8 = score_sub(2 * ngrp - 1, mnmx, True)
    smin = jnp.min(mn8, axis=0, keepdims=True)
    smax = jnp.max(mx8, axis=0, keepdims=True)

    n_causal = (lane_t + 1).astype(F32)
    k_eff = jnp.minimum(n_causal, float(TOPK_MAX))

    def count_rows(pred):
        part = kg // 8

        def body(g, c):
            r0 = pl.multiple_of(g * kg, kg)
            key_s = (r0 + lax.broadcasted_iota(jnp.int32, (kg, tq), 0)).astype(F32)
            ind = jnp.where(pred(s_ref[pl.ds(r0, kg), :], key_s), 1.0, 0.0)
            return c + jnp.sum(ind.reshape(8, part, tq), axis=0)

        c = lax.fori_loop(0, ngrp, body, jnp.zeros((part, tq), F32))
        return jnp.sum(c, axis=0, keepdims=True)

    def bis_step(st):
        it, lo, hi, c_lo, done = st
        first = (jnp.zeros((1, tq), F32) + jnp.where(it == 0, 1.0, 0.0)) > 0.0
        probe = jnp.where(first, smax, 0.5 * lo + 0.5 * hi)
        collapsed = ~first & ((probe <= lo) | (probe >= hi))
        cnt = count_rows(lambda blk, _: blk >= probe)
        ge = cnt >= k_eff
        upd = (done == 0.0) & ~collapsed
        lo_n = jnp.where(upd & ge, probe, lo)
        c_lo_n = jnp.where(upd & ge, cnt, c_lo)
        hi_n = jnp.where(upd & ~ge, probe, hi)
        fin = collapsed | (cnt == k_eff) | (first & ge)
        done_n = jnp.where(fin, 1.0, done)
        return it + 1, lo_n, hi_n, c_lo_n, done_n

    def n_open(st):
        return jnp.sum(1.0 - st[4])

    done0 = jnp.where(n_causal <= k_eff, 1.0, 0.0)
    st = lax.fori_loop(0, UNCHECKED_BISECT_STEPS, lambda _, s: bis_step(s),
                       (jnp.int32(0), smin, smax, n_causal, done0))
    st = lax.while_loop(lambda s: s[5] > 0.0,
                        lambda s: (lambda nxt: nxt + (n_open(nxt),))(bis_step(s[:5])),
                        st + (n_open(st),))
    thr, c_thr = st[1], st[3]

    tie_all = jnp.zeros((1, tq), F32) + (t0 + tq).astype(F32)

    def tie_limit():
        need = k_eff - count_rows(lambda blk, _: blk > thr)

        def body(_, st):
            lo, hi = st
            mid = jnp.floor(0.5 * (lo + hi))
            ok = count_rows(lambda blk, key_s: (blk == thr) & (key_s < mid)) >= need
            return jnp.where(ok, lo, mid), jnp.where(ok, mid, hi)

        n_steps = int(math.ceil(math.log2(s_ref.shape[0]))) + 1
        return lax.fori_loop(0, n_steps, body, (jnp.zeros((1, tq), F32), tie_all))[1]

    excess = jnp.sum(jnp.where(c_thr > k_eff, 1.0, 0.0))
    tie_lim = lax.cond(excess > 0.0, tie_limit, lambda: tie_all)

    def mask_body(g, c):
        r0 = pl.multiple_of(g * kg, kg)
        blk = s_ref[pl.ds(r0, kg), :]
        key_s = (r0 + lax.broadcasted_iota(jnp.int32, (kg, tq), 0)).astype(F32)
        sel = (blk > thr) | ((blk == thr) & (key_s < tie_lim))
        s_ref[pl.ds(r0, kg), :] = jnp.where(sel, 0.0, NEG_INF)
        return c

    lax.fori_loop(0, ngrp, mask_body, 0)

    m_ref[...] = jnp.full_like(m_ref, NEG_INF)
    acc_ref[...] = jnp.zeros_like(acc_ref)
    ones_rows = jnp.ones((DEN_ROWS, kg), BF16)

    def attend(g, near):
        r0 = pl.multiple_of(g * kg, kg)
        msk = s_ref[pl.ds(r0, kg), :]
        col_max = []
        for h in range(N_HEADS):
            hs = slice(h * HEAD_DIM, (h + 1) * HEAD_DIM)
            kh = k_ref[0, pl.ds(r0, kg), hs]
            lg = jnp.dot(kh, qt_ref[0, hs, :], preferred_element_type=F32) + msk
            if near is not None:
                lg = lg + tab_ref[h, near]
            lg_ref[h] = lg
            col_max.append(jnp.max(lg, axis=0, keepdims=True))
        for h in range(N_HEADS):
            hs = slice(h * HEAD_DIM, (h + 1) * HEAD_DIM)
            m_old = m_ref[h:h + 1, :]
            m_new = jnp.maximum(m_old, col_max[h])
            m_safe = jnp.where(m_new == NEG_INF, 0.0, m_new)
            p = jnp.exp2(lg_ref[h] - m_safe).astype(BF16)
            alpha = jnp.exp2(m_old - m_safe)
            m_ref[h:h + 1, :] = m_new
            vh = jnp.concatenate([vt_ref[0, hs, pl.ds(r0, kg)], ones_rows], axis=0)
            acc_ref[h] = alpha * acc_ref[h] + jnp.dot(vh, p, preferred_element_type=F32)

    n_far = jnp.maximum(ngrp - 2, 0)

    def _attend_pair(gp, c):
        attend(2 * gp, None)
        attend(2 * gp + 1, None)
        return c

    lax.fori_loop(0, n_far // 2, _attend_pair, 0)

    @pl.when(n_far % 2 == 1)
    def _():
        attend(n_far - 1, None)

    @pl.when(jq > 0)
    def _():
        attend(ngrp - 2, 0)

    attend(ngrp - 1, 1)

    for h in range(N_HEADS):
        o = acc_ref[h, 0:HEAD_DIM, :] / acc_ref[h, HEAD_DIM:HEAD_DIM + 1, :]
        y_ref[0, :, h * HEAD_DIM:(h + 1) * HEAD_DIM] = o.T.astype(BF16)


def _attention(qt, qit, wit, k, vt, ki, rel_bias):
    bsz, d, s = qt.shape
    tq = Q_TILE
    bkt = jnp.asarray(_near_bucket_ids())
    once = lambda shape: pl.BlockSpec(shape, lambda b, j: (b,) + (0,) * (len(shape) - 1),
                                      pipeline_mode=pl.Buffered(1))
    col = lambda r: pl.BlockSpec((1, r, tq), lambda b, j: (b, 0, j))
    return pl.pallas_call(
        _attn_kernel,
        grid=(bsz, s // tq),
        in_specs=[pl.BlockSpec(memory_space=pltpu.SMEM),
                  pl.BlockSpec(bkt.shape, lambda b, j: (0, 0, 0)),
                  col(d), col(d), col(IDX_HEADS),
                  once((1, s, d)), once((1, d, s)), once((1, s, IDX_DIM))],
        out_specs=pl.BlockSpec((1, tq, d), lambda b, j: (b, j, 0)),
        out_shape=jax.ShapeDtypeStruct((bsz, s, d), BF16),
        scratch_shapes=[pltpu.VMEM((s, tq), F32),
                        pltpu.VMEM((N_HEADS, 2, KEY_GROUP, tq), F32),
                        pltpu.VMEM((N_HEADS, HEAD_DIM + DEN_ROWS, tq), F32),
                        pltpu.VMEM((N_HEADS, tq), F32),
                        pltpu.VMEM((N_HEADS, KEY_GROUP, tq), F32)],
        compiler_params=_cparams("arbitrary", "arbitrary"),
        name="attn",
    )(rel_bias, bkt, qt, qit, wit, k, vt, ki)


def _pack_bf16_pair(x):
    c = x.shape[1] // 2
    lo = lax.bitcast_convert_type(x[:, :c].astype(BF16).astype(F32), jnp.uint32) >> 16
    hi = lax.bitcast_convert_type(x[:, c:].astype(BF16).astype(F32), jnp.uint32) & jnp.uint32(0xFFFF0000)
    return lo | hi


def _unpack_bf16_pair(p):
    a = lax.bitcast_convert_type(p << 16, F32)
    b = lax.bitcast_convert_type(p & jnp.uint32(0xFFFF0000), F32)
    return a, b


def _unpack_rows_bf16(p):
    a, b = _unpack_bf16_pair(p)
    return jnp.concatenate([a, b], axis=1).astype(BF16)


def _merge_kernel(yr_ref, ya_ref, glr_ref, gla_ref, x_ref, mod_ref, g_ref,
                  wr_ref, wa_ref, wo_ref, x1_ref, h2p_ref):
    merged = (jax.nn.sigmoid(glr_ref[0]) * jnp.dot(yr_ref[0], wr_ref[...], preferred_element_type=F32)
              + jax.nn.sigmoid(gla_ref[0]) * jnp.dot(ya_ref[0], wa_ref[...], preferred_element_type=F32))
    y = _bdot(merged, wo_ref[...])
    x1 = x_ref[0] + mod_ref[0, 2:3, :] * _rms(y, g_ref[1:2, :])
    x1_ref[0] = x1
    h2 = _rms(x1, g_ref[2:3, :]) * (1.0 + mod_ref[0, 4:5, :]) + mod_ref[0, 3:4, :]
    h2p_ref[0] = _pack_bf16_pair(h2)


def _merge(y_rnn, y_attn, gl_rnn, gl_attn, x, mod3, gains, w_br_rnn, w_br_attn, w_out):
    bsz, s, d = x.shape
    tm = ROW_TILE
    row = pl.BlockSpec((1, tm, d), lambda b, i: (b, i, 0))
    half = pl.BlockSpec((1, tm, d // 2), lambda b, i: (b, i, 0))
    c2 = lambda shape: pl.BlockSpec(shape, lambda b, i: (0,) * len(shape))
    return pl.pallas_call(
        _merge_kernel,
        grid=(bsz, s // tm),
        in_specs=[row, row, row, row, row,
                  pl.BlockSpec((1, N_MOD, d), lambda b, i: (b, 0, 0)),
                  c2(gains.shape), c2((d, d)), c2((d, d)), c2((d, d))],
        out_specs=(row, half),
        out_shape=(jax.ShapeDtypeStruct((bsz, s, d), F32),
                   jax.ShapeDtypeStruct((bsz, s, d // 2), jnp.uint32)),
        compiler_params=_cparams("arbitrary", "arbitrary"),
        name="merge",
    )(y_rnn, y_attn, gl_rnn, gl_attn, x, mod3, gains,
      w_br_rnn.astype(BF16), w_br_attn.astype(BF16), w_out.astype(BF16))


def _router_kernel(hp_ref, wr_ref, rb_ref, dest_ref, wk_ref, seg_ref):
    t = hp_ref.shape[0]
    gsz = N_EXPERTS // N_GROUPS
    h = _unpack_rows_bf16(hp_ref[...])
    s = jax.nn.sigmoid(_bdot_nt(wr_ref[...], h))
    s_sel = s + rb_ref[...]
    g3 = s_sel.reshape(N_GROUPS, gsz, t)
    e_in_g = lax.broadcasted_iota(jnp.int32, (N_GROUPS, gsz, t), 1)
    top1 = jnp.max(g3, axis=1, keepdims=True)
    first = jnp.min(jnp.where(g3 == top1, e_in_g, gsz), axis=1, keepdims=True)
    top2 = jnp.max(jnp.where(e_in_g == first, NEG_INF, g3), axis=1, keepdims=True)
    gscore = jnp.broadcast_to(top1 + top2, (N_GROUPS, gsz, t))
    gi = lax.broadcasted_iota(jnp.int32, (N_GROUPS, gsz, t), 0)
    gmask = jnp.zeros((N_GROUPS, gsz, t), F32)
    for _ in range(TOPK_GROUPS):
        mx = jnp.max(gscore, axis=0, keepdims=True)
        pick = jnp.min(jnp.where(gscore == mx, gi, N_GROUPS), axis=0, keepdims=True)
        hit = gi == pick
        gmask = jnp.where(hit, 1.0, gmask)
        gscore = jnp.where(hit, NEG_INF, gscore)
    cand = jnp.where(gmask.reshape(N_EXPERTS, t) > 0.0, s_sel, NEG_INF)
    ei = lax.broadcasted_iota(jnp.int32, (N_EXPERTS, t), 0)
    sel = jnp.zeros((N_EXPERTS, t), F32)
    picks = []
    for _ in range(TOP_K):
        mx = jnp.max(cand, axis=0, keepdims=True)
        pick = jnp.min(jnp.where(cand == mx, ei, N_EXPERTS), axis=0, keepdims=True)
        hit = ei == pick
        sel = jnp.where(hit, 1.0, sel)
        cand = jnp.where(hit, NEG_INF, cand)
        picks.append(pick)
    w = s * sel
    w = w / jnp.sum(w, axis=0, keepdims=True) * ROUTED_SCALE

    selb = sel.astype(BF16)
    tok_r = lax.broadcasted_iota(jnp.int32, (t, t), 0)
    tok_c = lax.broadcasted_iota(jnp.int32, (t, t), 1)
    rank = jnp.dot(selb, jnp.where(tok_r < tok_c, 1.0, 0.0).astype(BF16), preferred_element_type=F32)
    e_r = lax.broadcasted_iota(jnp.int32, (N_EXPERTS, N_EXPERTS), 0)
    e_c = lax.broadcasted_iota(jnp.int32, (N_EXPERTS, N_EXPERTS), 1)
    cnt_col = jnp.sum(sel, axis=1, keepdims=True)
    pad_col = jnp.floor((cnt_col + (SUBLANES - 1.0)) * (1.0 / SUBLANES))
    off_col = SUBLANES * jnp.dot(jnp.where(e_c < e_r, 1.0, 0.0).astype(BF16),
                                 jnp.broadcast_to(pad_col, (N_EXPERTS, LANES)).astype(BF16),
                                 preferred_element_type=F32)[:, 0:1]
    slot = rank + off_col
    dest_rows, w_rows = [], []
    for k in range(TOP_K):
        hit = ei == picks[k]
        dest_rows.append(jnp.sum(jnp.where(hit, slot, 0.0), axis=0, keepdims=True))
        w_rows.append(jnp.sum(jnp.where(hit, w, 0.0), axis=0, keepdims=True))
    dest_ref[0] = jnp.concatenate(dest_rows, axis=0).astype(jnp.int32)
    wk_ref[...] = jnp.concatenate(w_rows, axis=0)
    cnt_row = _bdot_nt(jnp.ones((SUBLANES, t), BF16), selb)
    pad_row = jnp.floor((cnt_row + (SUBLANES - 1.0)) * (1.0 / SUBLANES))
    off_row = SUBLANES * jnp.dot(pad_row.astype(BF16), jnp.where(e_r < e_c, 1.0, 0.0).astype(BF16),
                                 preferred_element_type=F32)
    seg_ref[0] = jnp.concatenate([off_row[0:1], cnt_row[0:1]], axis=1).astype(jnp.int32)


def _router(h2p, w_router, router_bias):
    n, dh = h2p.shape
    t = MOE_TILE
    nt = n // t
    return pl.pallas_call(
        _router_kernel,
        grid=(nt,),
        in_specs=[pl.BlockSpec((t, dh), lambda i: (i, 0)),
                  pl.BlockSpec((N_EXPERTS, 2 * dh), lambda i: (0, 0)),
                  pl.BlockSpec((N_EXPERTS, 1), lambda i: (0, 0))],
        out_specs=(pl.BlockSpec((1, TOP_K, t), lambda i: (i, 0, 0)),
                   pl.BlockSpec((TOP_K, t), lambda i: (0, i)),
                   pl.BlockSpec((1, 1, 2 * N_EXPERTS), lambda i: (i, 0, 0))),
        out_shape=(jax.ShapeDtypeStruct((nt, TOP_K, t), jnp.int32),
                   jax.ShapeDtypeStruct((TOP_K, n), F32),
                   jax.ShapeDtypeStruct((nt, 1, 2 * N_EXPERTS), jnp.int32)),
        compiler_params=_cparams("arbitrary"),
        name="router",
    )(h2p, w_router.T.astype(BF16), router_bias.reshape(N_EXPERTS, 1))


def _swiglu(x, wgu, wd):
    gu = jnp.dot(x, wgu, preferred_element_type=F32)
    g = gu[:, :D_EXPERT]
    a = (g * jax.nn.sigmoid(g)) * gu[:, D_EXPERT:]
    return jnp.dot(a.astype(BF16), wd, preferred_element_type=F32)


def _moe_kernel(seg_ref, hp_ref, dest_ref, wk_ref, wgu_ref, wd_ref, wsgu_ref, wsd_ref, x1_ref,
                mod_ref, g_ref, o_ref, xs_ref, acca_ref, accb_ref, dest_smem, w_smem, sem):
    i = pl.program_id(0)
    s = pl.program_id(1)
    n_s = pl.num_programs(1)
    t = MOE_TILE

    def slab(ref, row):
        return ref.at[pl.ds(pl.multiple_of(row * SLAB, SLAB), SLAB), :]

    def load_rows(ref, r0, n):
        return jnp.concatenate([ref[pl.ds(SLAB * r0 + c, n, stride=SLAB), :] for c in range(SLAB)],
                               axis=1)

    @pl.when((i == 0) & (s == 0))
    def _():
        xs_ref[...] = jnp.zeros_like(xs_ref)

    @pl.when(s == 0)
    def _():
        copies = (pltpu.make_async_copy(dest_ref, dest_smem, sem.at[0]),
                  pltpu.make_async_copy(wk_ref, w_smem, sem.at[1]))
        for cp in copies:
            cp.start()
        for cp in copies:
            cp.wait()

        def body(nb, c):
            n0 = nb * MOVE_TOKENS
            base = n0 * TOP_K
            for j in range(MOVE_TOKENS):
                row = slab(hp_ref, n0 + j)[...]
                for k in range(TOP_K):
                    slab(xs_ref, dest_smem[base + (j * TOP_K + k)])[...] = row
            return c

        lax.fori_loop(0, t // MOVE_TOKENS, body, 0)

    def put_rows(r0, end, y, old):
        new = _pack_bf16_pair(y)
        rows = r0 + lax.broadcasted_iota(jnp.int32, (FFN_ROWS, LANES), 0)
        for c in range(SLAB):
            cs = slice(c * LANES, (c + 1) * LANES)
            xs_ref[pl.ds(SLAB * r0 + c, FFN_ROWS, stride=SLAB), :] = jnp.where(rows < end, new[:, cs],
                                                                               old[:, cs])

    @pl.when(s < n_s - 1)
    def _():
        offs, ends = [], []
        for eb in range(EXPERTS_PER_STEP):
            e = s * EXPERTS_PER_STEP + eb
            offs.append(pl.multiple_of(seg_ref[i, e], SUBLANES))
            ends.append(offs[eb] + seg_ref[i, N_EXPERTS + e])
        olds = [load_rows(xs_ref, offs[eb], FFN_ROWS) for eb in range(EXPERTS_PER_STEP)]
        ys = [_swiglu(_unpack_rows_bf16(olds[eb]), wgu_ref[eb], wd_ref[eb])
              for eb in range(EXPERTS_PER_STEP)]
        for eb in range(EXPERTS_PER_STEP):
            put_rows(offs[eb], ends[eb], ys[eb], olds[eb])
        for eb in range(EXPERTS_PER_STEP):
            def chunk(c, carry, eb=eb):
                r0 = pl.multiple_of(offs[eb] + c * FFN_ROWS, SUBLANES)
                old = load_rows(xs_ref, r0, FFN_ROWS)
                put_rows(r0, ends[eb], _swiglu(_unpack_rows_bf16(old), wgu_ref[eb], wd_ref[eb]), old)
                return carry

            n_chunks = lax.div(ends[eb] - offs[eb] + (FFN_ROWS - 1), FFN_ROWS)
            lax.fori_loop(1, n_chunks, chunk, 0)

    @pl.when(s == n_s - 1)
    def _():
        o_ref[...] = _swiglu(_unpack_rows_bf16(load_rows(hp_ref, 0, t)), wsgu_ref[...], wsd_ref[...])

        def body(nb, c):
            n0 = pl.multiple_of(nb * SUBLANES, SUBLANES)
            base = n0 * TOP_K
            for j in range(SUBLANES):
                acc_a = jnp.zeros((SLAB, LANES), F32)
                acc_b = jnp.zeros((SLAB, LANES), F32)
                for k in range(TOP_K):
                    idx = base + (j * TOP_K + k)
                    a, b = _unpack_bf16_pair(slab(xs_ref, dest_smem[idx])[...])
                    w = w_smem[idx]
                    acc_a = acc_a + a * w
                    acc_b = acc_b + b * w
                acca_ref[j * SLAB:(j + 1) * SLAB, :] = acc_a
                accb_ref[j * SLAB:(j + 1) * SLAB, :] = acc_b
            lo = [acca_ref[pl.ds(c, SUBLANES, stride=SLAB), :] for c in range(SLAB)]
            hi = [accb_ref[pl.ds(c, SUBLANES, stride=SLAB), :] for c in range(SLAB)]
            o_ref[pl.ds(n0, SUBLANES), :] += jnp.concatenate(lo + hi, axis=1)
            return c

        lax.fori_loop(0, t // SUBLANES, body, 0)
        o_ref[...] = x1_ref[...] + mod_ref[0, 5:6, :] * _rms(o_ref[...], g_ref[...])


def _moe(h2p, dest, wk, seg, wgu, wd, wsgu, wsd, x1, mod3, g3, seq):
    n, dh = h2p.shape
    d = 2 * dh
    t = MOE_TILE
    n_steps = N_EXPERTS // EXPERTS_PER_STEP + 1
    tiles_per_seq = seq // t
    tok = lambda w: pl.BlockSpec((t, w), lambda i, s: (i, 0))
    table = pl.BlockSpec((TOP_K * t,), lambda i, s: (i,))
    wblk = lambda r, c: pl.BlockSpec((EXPERTS_PER_STEP, r, c),
                                     lambda i, s: (jnp.minimum(s, n_steps - 2), 0, 0))
    c2 = lambda shape: pl.BlockSpec(shape, lambda i, s: (0,) * len(shape))
    xs_rows = TOP_K * t + N_EXPERTS * SUBLANES + FFN_ROWS
    return pl.pallas_call(
        _moe_kernel,
        grid=(n // t, n_steps),
        in_specs=[pl.BlockSpec(memory_space=pltpu.SMEM),
                  pl.BlockSpec((SLAB * t, LANES), lambda i, s: (i, 0)),
                  table, table,
                  wblk(d, 2 * D_EXPERT), wblk(D_EXPERT, d),
                  c2((d, 2 * D_EXPERT)), c2((D_EXPERT, d)),
                  tok(d),
                  pl.BlockSpec((1, N_MOD, d), lambda i, s: (i // tiles_per_seq, 0, 0)),
                  c2((1, d))],
        out_specs=tok(d),
        out_shape=jax.ShapeDtypeStruct((n, d), F32),
        scratch_shapes=[pltpu.VMEM((SLAB * xs_rows, LANES), jnp.uint32),
                        pltpu.VMEM((SLAB * SUBLANES, LANES), F32),
                        pltpu.VMEM((SLAB * SUBLANES, LANES), F32),
                        pltpu.SMEM((TOP_K * t,), jnp.int32),
                        pltpu.SMEM((TOP_K * t,), F32),
                        pltpu.SemaphoreType.DMA((2,))],
        compiler_params=_cparams("arbitrary", "arbitrary"),
        name="moe",
    )(seg, h2p.reshape(n * SLAB, LANES), dest, wk, wgu, wd, wsgu, wsd, x1, mod3, g3)


def kernel(x, c, w_ada, b_ada, norm_gain, w_in, conv_w, conv_b, w_rg_a, b_rg_a, w_rg_x, b_rg_x,
           lru_lambda, w_br_rnn, w_br_attn, w_out, rel_bias, w_router, router_bias,
           w_exp_gate, w_exp_up, w_exp_down, w_sh_gate, w_sh_up, w_sh_down):
    bsz, s, d = x.shape
    depth = w_ada.shape[0]
    for l in range(depth):
        mod = _ada(c, w_ada[l], b_ada[l])
        mod3 = mod.reshape(bsz, N_MOD, d)
        gains = norm_gain[l]
        (u_rnn, u_gate, k, gl_rnn, gl_attn, ki, qt, vt, qit, wit) = _in_proj(x, mod3, gains[0:1], w_in[l])
        y_rnn = _rglru(u_rnn, u_gate, conv_w[l], conv_b[l], w_rg_a[l], b_rg_a[l], w_rg_x[l],
                       b_rg_x[l], lru_lambda[l])
        y_attn = _attention(qt, qit, wit, k, vt, ki, rel_bias)
        x1, h2p = _merge(y_rnn, y_attn, gl_rnn, gl_attn, x, mod3, gains, w_br_rnn[l], w_br_attn[l],
                         w_out[l])
        h2p = h2p.reshape(bsz * s, d // 2)
        dest, wk, seg = _router(h2p, w_router[l], router_bias[l])
        wgu = jnp.concatenate([w_exp_gate[l], w_exp_up[l]], axis=-1).astype(BF16)
        wsgu = jnp.concatenate([w_sh_gate[l], w_sh_up[l]], axis=-1).astype(BF16)
        dest = jnp.transpose(dest, (0, 2, 1)).reshape(-1)
        wk = jnp.transpose(wk).reshape(-1)
        x = _moe(h2p, dest, wk, seg.reshape(-1, 2 * N_EXPERTS), wgu, w_exp_down[l].astype(BF16),
                 wsgu, w_sh_down[l].astype(BF16), x1.reshape(bsz * s, d), mod3, gains[3:4],
                 s).reshape(bsz, s, d)
    return x
```

```python
import functools
import math

import jax
import jax.numpy as jnp
import numpy as np
from jax import lax
from jax.experimental import pallas as pl
from jax.experimental.pallas import tpu as pltpu

F32 = jnp.float32
BF16 = jnp.bfloat16

D_MODEL = 1024
RNN_BLOCKS = 8
RNN_BW = D_MODEL // RNN_BLOCKS
CONV_W = 4
LRU_C = 8.0
N_HEADS = 8
HEAD_DIM = 128
IDX_HEADS = 16
IDX_DIM = 64
TOPK_MAX = 256
N_BUCKETS = 32
MAX_DIST = 128
N_EXPERTS = 64
TOP_K = 8
N_GROUPS = 8
TOPK_GROUPS = 4
D_EXPERT = 256
ROUTED_SCALE = 2.5
N_MOD = 6
EPS = 1e-6

LANES = 128
SUBLANES = 8
VMEM_LIMIT_BYTES = 58 * 1024 * 1024

ROW_TILE = 512
UNCHECKED_BISECT_STEPS = 16
SCAN_TILE = 512
Q_TILE = 256
KEY_SUB = 128
KEY_GROUP = 256
DEN_ROWS = 16
MOE_TILE = 1024
EXPERTS_PER_STEP = 4
FFN_ROWS = 160
MOVE_TOKENS = 4
SLAB = 4

NEG_INF = float("-inf")
LOG2E = math.log2(math.e)


def _cparams(*sem):
    return pltpu.CompilerParams(dimension_semantics=sem, vmem_limit_bytes=VMEM_LIMIT_BYTES)


def _bdot(a, b):
    return jnp.dot(a.astype(BF16), b.astype(BF16), preferred_element_type=F32)


def _bdot_nt(a, b):
    return lax.dot_general(a.astype(BF16), b.astype(BF16), (((1,), (1,)), ((), ())),
                           preferred_element_type=F32)


def _rms(x, g):
    ms = jnp.mean(x * x, axis=-1, keepdims=True)
    return x * lax.rsqrt(ms + EPS) * g


def _ada_kernel(c_ref, w_ref, b_ref, o_ref):
    c = c_ref[...]
    cond = c * jax.nn.sigmoid(c)
    o_ref[...] = _bdot(cond, w_ref[...]) + b_ref[...]


def _ada(c, w_ada, b_ada):
    bsz, d = c.shape
    n = w_ada.shape[1]
    tn = 1024
    return pl.pallas_call(
        _ada_kernel,
        grid=(n // tn,),
        in_specs=[pl.BlockSpec((bsz, d), lambda j: (0, 0)),
                  pl.BlockSpec((d, tn), lambda j: (0, j)),
                  pl.BlockSpec((1, tn), lambda j: (0, j))],
        out_specs=pl.BlockSpec((bsz, tn), lambda j: (0, j)),
        out_shape=jax.ShapeDtypeStruct((bsz, n), F32),
        compiler_params=_cparams("arbitrary"),
        name="ada",
    )(c, w_ada, b_ada.reshape(1, n))


def _inproj_kernel(x_ref, mod_ref, g_ref, wn_ref, wki_ref, wt_ref, wwi_ref,
                   urnn_ref, ugate_ref, k_ref, glr_ref, gla_ref, ki_ref,
                   qt_ref, vt_ref, qit_ref, wit_ref):
    d = D_MODEL
    x = x_ref[0]
    h = _rms(x, g_ref[...]) * (1.0 + mod_ref[0, 1:2, :]) + mod_ref[0, 0:1, :]
    hb = h.astype(BF16)
    urnn_ref[0] = jnp.dot(hb, wn_ref[:, 0 * d:1 * d], preferred_element_type=F32)
    ugate_ref[0] = jnp.dot(hb, wn_ref[:, 1 * d:2 * d], preferred_element_type=F32)
    k_ref[0] = jnp.dot(hb, wn_ref[:, 2 * d:3 * d], preferred_element_type=F32).astype(BF16)
    glr_ref[0] = jnp.dot(hb, wn_ref[:, 3 * d:4 * d], preferred_element_type=F32)
    gla_ref[0] = jnp.dot(hb, wn_ref[:, 4 * d:5 * d], preferred_element_type=F32)
    ki_ref[0] = jnp.dot(hb, wki_ref[...], preferred_element_type=F32).astype(BF16)
    nt = (((1,), (1,)), ((), ()))
    qt_ref[0] = (lax.dot_general(wt_ref[0 * d:1 * d, :], hb, nt, preferred_element_type=F32)
                 * (HEAD_DIM ** -0.5 * LOG2E)).astype(BF16)
    vt_ref[0] = lax.dot_general(wt_ref[1 * d:2 * d, :], hb, nt, preferred_element_type=F32).astype(BF16)
    qit_ref[0] = lax.dot_general(wt_ref[2 * d:3 * d, :], hb, nt, preferred_element_type=F32).astype(BF16)
    wit_ref[0] = lax.dot_general(wwi_ref[...], hb, nt, preferred_element_type=F32)


def _in_proj(x, mod3, g0, w_in):
    bsz, s, d = x.shape
    tm = ROW_TILE
    offs = np.cumsum([0, d, d, d, d, d, IDX_HEADS * IDX_DIM, IDX_DIM, IDX_HEADS, d, d])
    seg = lambda i: w_in[:, int(offs[i]):int(offs[i + 1])]
    wn = jnp.concatenate([seg(0), seg(1), seg(3), seg(8), seg(9)], axis=1).astype(BF16)
    wki = seg(6).astype(BF16)
    wt = jnp.concatenate([seg(2), seg(4), seg(5)], axis=1).T.astype(BF16)
    wwi = seg(7).T.astype(BF16)
    const = lambda shape: pl.BlockSpec(shape, lambda b, i: (0,) * len(shape),
                                       pipeline_mode=pl.Buffered(1))
    row = lambda w: pl.BlockSpec((1, tm, w), lambda b, i: (b, i, 0))
    col = lambda r: pl.BlockSpec((1, r, tm), lambda b, i: (b, 0, i))
    out_shape = (
        jax.ShapeDtypeStruct((bsz, s, d), F32),
        jax.ShapeDtypeStruct((bsz, s, d), F32),
        jax.ShapeDtypeStruct((bsz, s, d), BF16),
        jax.ShapeDtypeStruct((bsz, s, d), F32),
        jax.ShapeDtypeStruct((bsz, s, d), F32),
        jax.ShapeDtypeStruct((bsz, s, IDX_DIM), BF16),
        jax.ShapeDtypeStruct((bsz, d, s), BF16),
        jax.ShapeDtypeStruct((bsz, d, s), BF16),
        jax.ShapeDtypeStruct((bsz, d, s), BF16),
        jax.ShapeDtypeStruct((bsz, IDX_HEADS, s), F32),
    )
    return pl.pallas_call(
        _inproj_kernel,
        grid=(bsz, s // tm),
        in_specs=[row(d),
                  pl.BlockSpec((1, N_MOD, d), lambda b, i: (b, 0, 0)),
                  pl.BlockSpec((1, d), lambda b, i: (0, 0)),
                  const(wn.shape), const(wki.shape), const(wt.shape), const(wwi.shape)],
        out_specs=(row(d), row(d), row(d), row(d), row(d), row(IDX_DIM),
                   col(d), col(d), col(d), col(IDX_HEADS)),
        out_shape=out_shape,
        compiler_params=_cparams("arbitrary", "arbitrary"),
        name="in_proj",
    )(x, mod3, g0, wn, wki, wt, wwi)


def _gelu_tanh(x):
    return 0.5 * x * (1.0 + jnp.tanh(math.sqrt(2.0 / math.pi) * (x + 0.044715 * (x * x * x))))


def _rglru_kernel(u_ref, ug_ref, cw_ref, cb_ref, wax_ref, ba_ref, bx_ref, lam_ref, y_ref,
                  ext_ref, a_ref, b_ref, carry_ref):
    ts = u_ref.shape[1]
    d = D_MODEL

    @pl.when(pl.program_id(1) == 0)
    def _():
        ext_ref[0:SUBLANES, :] = jnp.zeros((SUBLANES, d), F32)
        carry_ref[...] = jnp.zeros_like(carry_ref)

    ext_ref[SUBLANES:SUBLANES + ts, :] = u_ref[0]
    xc = cb_ref[...] + cw_ref[CONV_W - 1:CONV_W, :] * ext_ref[SUBLANES:SUBLANES + ts, :]
    for k in range(CONV_W - 1):
        off = SUBLANES - (CONV_W - 1) + k
        xc = xc + cw_ref[k:k + 1, :] * ext_ref[off:off + ts, :]
    ext_ref[0:SUBLANES, :] = ext_ref[ts:ts + SUBLANES, :]

    nl = -lam_ref[...]
    sp = jnp.maximum(nl, 0.0) + jnp.log1p(jnp.exp(-jnp.abs(nl)))
    for n in range(RNN_BLOCKS):
        cs = slice(n * RNN_BW, (n + 1) * RNN_BW)
        xb = xc[:, cs]
        g = _bdot(xb, wax_ref[n])
        r = jax.nn.sigmoid(g[:, :RNN_BW] + ba_ref[:, cs])
        i = jax.nn.sigmoid(g[:, RNN_BW:] + bx_ref[:, cs])
        log_a = (-LRU_C) * r * sp[:, cs]
        a_ref[:, cs] = jnp.exp(log_a)
        th = jnp.tanh(log_a)
        b_ref[:, cs] = jnp.sqrt(-2.0 * th / (1.0 - th)) * (i * xb)

    row = lax.broadcasted_iota(jnp.int32, (SUBLANES, d), 0)

    def group(gi, hprev):
        r0 = pl.multiple_of(gi * SUBLANES, SUBLANES)
        a = a_ref[pl.ds(r0, SUBLANES), :]
        b = b_ref[pl.ds(r0, SUBLANES), :]
        for sh in (1, 2, 4):
            keep = row >= sh
            a_s = jnp.where(keep, pltpu.roll(a, sh, 0), 1.0)
            b_s = jnp.where(keep, pltpu.roll(b, sh, 0), 0.0)
            b = a * b_s + b
            a = a * a_s
        h = b + a * hprev
        b_ref[pl.ds(r0, SUBLANES), :] = h
        return jnp.broadcast_to(h[SUBLANES - 1:SUBLANES, :], (SUBLANES, d))

    carry_ref[...] = lax.fori_loop(0, ts // SUBLANES, group, carry_ref[...])
    y_ref[0] = (b_ref[...] * _gelu_tanh(ug_ref[0])).astype(BF16)


def _rglru(u_rnn, u_gate, conv_w, conv_b, w_rg_a, b_rg_a, w_rg_x, b_rg_x, lam):
    bsz, s, d = u_rnn.shape
    ts = SCAN_TILE
    wax = jnp.concatenate([w_rg_a, w_rg_x], axis=-1).astype(BF16)
    vec = lambda v: v.reshape(1, d)
    c2 = lambda shape: pl.BlockSpec(shape, lambda b, i: (0,) * len(shape))
    row = pl.BlockSpec((1, ts, d), lambda b, i: (b, i, 0))
    return pl.pallas_call(
        _rglru_kernel,
        grid=(bsz, s // ts),
        in_specs=[row, row, c2((CONV_W, d)), c2((1, d)), c2(wax.shape), c2((1, d)), c2((1, d)),
                  c2((1, d))],
        out_specs=row,
        out_shape=jax.ShapeDtypeStruct((bsz, s, d), BF16),
        scratch_shapes=[pltpu.VMEM((ts + SUBLANES, d), F32), pltpu.VMEM((ts, d), F32),
                        pltpu.VMEM((ts, d), F32), pltpu.VMEM((SUBLANES, d), F32)],
        compiler_params=_cparams("arbitrary", "arbitrary"),
        name="rglru",
    )(u_rnn, u_gate, conv_w, vec(conv_b), wax, vec(b_rg_a), vec(b_rg_x), vec(lam))


def _t5_bucket_np(dist):
    max_exact = N_BUCKETS // 2
    dd = np.maximum(dist, 0)
    df = np.maximum(dd, 1).astype(np.float32)
    large = max_exact + (np.log(df / np.float32(max_exact)) / np.float32(math.log(MAX_DIST / max_exact))
                         * np.float32(N_BUCKETS - max_exact)).astype(np.int32)
    large = np.minimum(large, N_BUCKETS - 1)
    return np.where(dd < max_exact, dd, large)


def _near_bucket_ids():
    r = np.arange(KEY_GROUP)[None, :, None]
    c = np.arange(Q_TILE)[None, None, :]
    o = np.arange(2)[:, None, None]
    return _t5_bucket_np(c - r - (o - 1) * KEY_GROUP).astype(np.int32)


def _attn_kernel(rb_ref, bkt_ref, qt_ref, qit_ref, wit_ref, k_ref, vt_ref, ki_ref, y_ref,
                 s_ref, tab_ref, acc_ref, m_ref, lg_ref):
    tq = Q_TILE
    ks = KEY_SUB
    kg = KEY_GROUP
    jq = pl.program_id(1)
    t0 = jq * tq
    ngrp = jq + 1
    lane_t = t0 + lax.broadcasted_iota(jnp.int32, (1, tq), 1)

    @pl.when((pl.program_id(0) == 0) & (jq == 0))
    def _():
        for o in range(2):
            for h in range(N_HEADS):
                tab_ref[h, o] = jnp.zeros((kg, tq), F32)

            def fill(b, c):
                hit = bkt_ref[o] == b
                for h in range(N_HEADS):
                    val = (rb_ref[b, h] - rb_ref[N_BUCKETS - 1, h]) * LOG2E
                    tab_ref[h, o] = jnp.where(hit, val, tab_ref[h, o])
                return c

            lax.fori_loop(0, N_BUCKETS - 1, fill, 0)

    wi = wit_ref[0] * (IDX_HEADS ** -0.5 * IDX_DIM ** -0.5)

    def score_sub(i, mnmx, masked):
        r0 = pl.multiple_of(i * ks, ks)
        kic = ki_ref[0, pl.ds(r0, ks), :]
        acc = jnp.zeros((ks, tq), F32)
        for h in range(IDX_HEADS):
            dts = jnp.dot(kic, qit_ref[0, h * IDX_DIM:(h + 1) * IDX_DIM, :],
                          preferred_element_type=F32)
            acc = acc + jnp.maximum(dts, 0.0) * wi[h:h + 1, :]
        lo_src = acc
        if masked:
            key_s = r0 + lax.broadcasted_iota(jnp.int32, (ks, tq), 0)
            causal = key_s <= lane_t
            acc = jnp.where(causal, acc, NEG_INF)
            lo_src = jnp.where(causal, acc, jnp.inf)
        s_ref[pl.ds(r0, ks), :] = acc
        mn, mx = mnmx
        mn = jnp.minimum(mn, jnp.min(lo_src.reshape(ks // SUBLANES, SUBLANES, tq), axis=0))
        mx = jnp.maximum(mx, jnp.max(acc.reshape(ks // SUBLANES, SUBLANES, tq), axis=0))
        return mn, mx

    mnmx = (jnp.full((SUBLANES, tq), jnp.inf, F32), jnp.full((SUBLANES, tq), NEG_INF, F32))
    mnmx = lax.fori_loop(0, ngrp - 1,
                         lambda g, c: score_sub(2 * g + 1, score_sub(2 * g, c, False), False), mnmx)
    mnmx = score_sub(2 * ngrp - 2, mnmx, True)
    mn8, mx8 = score_sub(2 * ngrp - 1, mnmx, True)
    smin = jnp.min(mn8, axis=0, keepdims=True)
    smax = jnp.max(mx8, axis=0, keepdims=True)

    n_causal = (lane_t + 1).astype(F32)
    k_eff = jnp.minimum(n_causal, float(TOPK_MAX))

    def count_rows(pred):
        part = 2 * SUBLANES

        def block(start, rows, c):
            r0 = pl.multiple_of(start, rows)
            key_s = (r0 + lax.broadcasted_iota(jnp.int32, (rows, tq), 0)).astype(F32)
            ind = jnp.where(pred(s_ref[pl.ds(r0, rows), :], key_s), 1.0, 0.0)
            return c + jnp.sum(ind.reshape(rows // part, part, tq), axis=0)

        c = lax.fori_loop(0, ngrp // 2, lambda g, c: block(g * (2 * kg), 2 * kg, c),
                          jnp.zeros((part, tq), F32))
        c = lax.cond(ngrp % 2 == 1, lambda c: block((ngrp - 1) * kg, kg, c), lambda c: c, c)
        return jnp.sum(c, axis=0, keepdims=True)

    def bis_step(st):
        it, lo, hi, c_lo, done = st
        first = (jnp.zeros((1, tq), F32) + jnp.where(it == 0, 1.0, 0.0)) > 0.0
        probe = jnp.where(first, smax, 0.5 * lo + 0.5 * hi)
        collapsed = ~first & ((probe <= lo) | (probe >= hi))
        cnt = count_rows(lambda blk, _: blk >= probe)
        ge = cnt >= k_eff
        upd = (done == 0.0) & ~collapsed
        lo_n = jnp.where(upd & ge, probe, lo)
        c_lo_n = jnp.where(upd & ge, cnt, c_lo)
        hi_n = jnp.where(upd & ~ge, probe, hi)
        fin = collapsed | (cnt == k_eff) | (first & ge)
        done_n = jnp.where(fin, 1.0, done)
        return it + 1, lo_n, hi_n, c_lo_n, done_n

    def n_open(st):
        return jnp.sum(1.0 - st[4])

    done0 = jnp.where(n_causal <= k_eff, 1.0, 0.0)
    st = lax.fori_loop(0, UNCHECKED_BISECT_STEPS, lambda _, s: bis_step(s),
                       (jnp.int32(0), smin, smax, n_causal, done0))
    st = lax.while_loop(lambda s: s[5] > 0.0,
                        lambda s: (lambda nxt: nxt + (n_open(nxt),))(bis_step(s[:5])),
                        st + (n_open(st),))
    thr, c_thr = st[1], st[3]

    tie_all = jnp.zeros((1, tq), F32) + (t0 + tq).astype(F32)

    def tie_limit():
        need = k_eff - count_rows(lambda blk, _: blk > thr)

        def body(_, st):
            lo, hi = st
            mid = jnp.floor(0.5 * (lo + hi))
            ok = count_rows(lambda blk, key_s: (blk == thr) & (key_s < mid)) >= need
            return jnp.where(ok, lo, mid), jnp.where(ok, mid, hi)

        n_steps = int(math.ceil(math.log2(s_ref.shape[0]))) + 1
        return lax.fori_loop(0, n_steps, body, (jnp.zeros((1, tq), F32), tie_all))[1]

    excess = jnp.sum(jnp.where(c_thr > k_eff, 1.0, 0.0))
    tie_lim = lax.cond(excess > 0.0, tie_limit, lambda: tie_all)

    def mask_body(g, c):
        r0 = pl.multiple_of(g * kg, kg)
        blk = s_ref[pl.ds(r0, kg), :]
        key_s = (r0 + lax.broadcasted_iota(jnp.int32, (kg, tq), 0)).astype(F32)
        sel = (blk > thr) | ((blk == thr) & (key_s < tie_lim))
        s_ref[pl.ds(r0, kg), :] = jnp.where(sel, 0.0, NEG_INF)
        return c

    lax.fori_loop(0, ngrp, mask_body, 0)

    m_ref[...] = jnp.full_like(m_ref, NEG_INF)
    acc_ref[...] = jnp.zeros_like(acc_ref)
    ones_rows = jnp.ones((DEN_ROWS, kg), BF16)

    def attend(g, near):
        r0 = pl.multiple_of(g * kg, kg)
        msk = s_ref[pl.ds(r0, kg), :]
        col_max = []
        for h in range(N_HEADS):
            hs = slice(h * HEAD_DIM, (h + 1) * HEAD_DIM)
            kh = k_ref[0, pl.ds(r0, kg), hs]
            lg = jnp.dot(kh, qt_ref[0, hs, :], preferred_element_type=F32) + msk
            if near is not None:
                lg = lg + tab_ref[h, near]
            lg_ref[h] = lg
            col_max.append(jnp.max(lg, axis=0, keepdims=True))
        for h in range(N_HEADS):
            hs = slice(h * HEAD_DIM, (h + 1) * HEAD_DIM)
            m_old = m_ref[h:h + 1, :]
            m_new = jnp.maximum(m_old, col_max[h])
            m_safe = jnp.where(m_new == NEG_INF, 0.0, m_new)
            p = jnp.exp2(lg_ref[h] - m_safe).astype(BF16)
            alpha = jnp.exp2(m_old - m_safe)
            m_ref[h:h + 1, :] = m_new
            vh = jnp.concatenate([vt_ref[0, hs, pl.ds(r0, kg)], ones_rows], axis=0)
            acc_ref[h] = alpha * acc_ref[h] + jnp.dot(vh, p, preferred_element_type=F32)

    n_far = jnp.maximum(ngrp - 2, 0)

    def _attend_pair(gp, c):
        attend(2 * gp, None)
        attend(2 * gp + 1, None)
        return c

    lax.fori_loop(0, n_far // 2, _attend_pair, 0)

    @pl.when(n_far % 2 == 1)
    def _():
        attend(n_far - 1, None)

    @pl.when(jq > 0)
    def _():
        attend(ngrp - 2, 0)

    attend(ngrp - 1, 1)

    for h in range(N_HEADS):
        o = acc_ref[h, 0:HEAD_DIM, :] / acc_ref[h, HEAD_DIM:HEAD_DIM + 1, :]
        y_ref[0, :, h * HEAD_DIM:(h + 1) * HEAD_DIM] = o.T.astype(BF16)


def _attention(qt, qit, wit, k, vt, ki, rel_bias):
    bsz, d, s = qt.shape
    tq = Q_TILE
    bkt = jnp.asarray(_near_bucket_ids())
    once = lambda shape: pl.BlockSpec(shape, lambda b, j: (b,) + (0,) * (len(shape) - 1),
                                      pipeline_mode=pl.Buffered(1))
    col = lambda r: pl.BlockSpec((1, r, tq), lambda b, j: (b, 0, j))
    return pl.pallas_call(
        _attn_kernel,
        grid=(bsz, s // tq),
        in_specs=[pl.BlockSpec(memory_space=pltpu.SMEM),
                  pl.BlockSpec(bkt.shape, lambda b, j: (0, 0, 0)),
                  col(d), col(d), col(IDX_HEADS),
                  once((1, s, d)), once((1, d, s)), once((1, s, IDX_DIM))],
        out_specs=pl.BlockSpec((1, tq, d), lambda b, j: (b, j, 0)),
        out_shape=jax.ShapeDtypeStruct((bsz, s, d), BF16),
        scratch_shapes=[pltpu.VMEM((s, tq), F32),
                        pltpu.VMEM((N_HEADS, 2, KEY_GROUP, tq), F32),
                        pltpu.VMEM((N_HEADS, HEAD_DIM + DEN_ROWS, tq), F32),
                        pltpu.VMEM((N_HEADS, tq), F32),
                        pltpu.VMEM((N_HEADS, KEY_GROUP, tq), F32)],
        compiler_params=_cparams("arbitrary", "arbitrary"),
        name="attn",
    )(rel_bias, bkt, qt, qit, wit, k, vt, ki)


def _pack_bf16_pair(x):
    c = x.shape[1] // 2
    lo = lax.bitcast_convert_type(x[:, :c].astype(BF16).astype(F32), jnp.uint32) >> 16
    hi = lax.bitcast_convert_type(x[:, c:].astype(BF16).astype(F32), jnp.uint32) & jnp.uint32(0xFFFF0000)
    return lo | hi


def _unpack_bf16_pair(p):
    a = lax.bitcast_convert_type(p << 16, F32)
    b = lax.bitcast_convert_type(p & jnp.uint32(0xFFFF0000), F32)
    return a, b


def _unpack_rows_bf16(p):
    a, b = _unpack_bf16_pair(p)
    return jnp.concatenate([a, b], axis=1).astype(BF16)


def _merge_kernel(yr_ref, ya_ref, glr_ref, gla_ref, x_ref, mod_ref, g_ref,
                  wr_ref, wa_ref, wo_ref, x1_ref, h2p_ref):
    merged = (jax.nn.sigmoid(glr_ref[0]) * jnp.dot(yr_ref[0], wr_ref[...], preferred_element_type=F32)
              + jax.nn.sigmoid(gla_ref[0]) * jnp.dot(ya_ref[0], wa_ref[...], preferred_element_type=F32))
    y = _bdot(merged, wo_ref[...])
    x1 = x_ref[0] + mod_ref[0, 2:3, :] * _rms(y, g_ref[1:2, :])
    x1_ref[0] = x1
    h2 = _rms(x1, g_ref[2:3, :]) * (1.0 + mod_ref[0, 4:5, :]) + mod_ref[0, 3:4, :]
    h2p_ref[0] = _pack_bf16_pair(h2)


def _merge(y_rnn, y_attn, gl_rnn, gl_attn, x, mod3, gains, w_br_rnn, w_br_attn, w_out):
    bsz, s, d = x.shape
    tm = ROW_TILE
    row = pl.BlockSpec((1, tm, d), lambda b, i: (b, i, 0))
    half = pl.BlockSpec((1, tm, d // 2), lambda b, i: (b, i, 0))
    c2 = lambda shape: pl.BlockSpec(shape, lambda b, i: (0,) * len(shape))
    return pl.pallas_call(
        _merge_kernel,
        grid=(bsz, s // tm),
        in_specs=[row, row, row, row, row,
                  pl.BlockSpec((1, N_MOD, d), lambda b, i: (b, 0, 0)),
                  c2(gains.shape), c2((d, d)), c2((d, d)), c2((d, d))],
        out_specs=(row, half),
        out_shape=(jax.ShapeDtypeStruct((bsz, s, d), F32),
                   jax.ShapeDtypeStruct((bsz, s, d // 2), jnp.uint32)),
        compiler_params=_cparams("arbitrary", "arbitrary"),
        name="merge",
    )(y_rnn, y_attn, gl_rnn, gl_attn, x, mod3, gains,
      w_br_rnn.astype(BF16), w_br_attn.astype(BF16), w_out.astype(BF16))


def _router_kernel(hp_ref, wr_ref, rb_ref, dest_ref, wk_ref, seg_ref):
    t = hp_ref.shape[0]
    gsz = N_EXPERTS // N_GROUPS
    h = _unpack_rows_bf16(hp_ref[...])
    s = jax.nn.sigmoid(_bdot_nt(wr_ref[...], h))
    s_sel = s + rb_ref[...]
    g3 = s_sel.reshape(N_GROUPS, gsz, t)
    e_in_g = lax.broadcasted_iota(jnp.int32, (N_GROUPS, gsz, t), 1)
    top1 = jnp.max(g3, axis=1, keepdims=True)
    first = jnp.min(jnp.where(g3 == top1, e_in_g, gsz), axis=1, keepdims=True)
    top2 = jnp.max(jnp.where(e_in_g == first, NEG_INF, g3), axis=1, keepdims=True)
    gscore = jnp.broadcast_to(top1 + top2, (N_GROUPS, gsz, t))
    gi = lax.broadcasted_iota(jnp.int32, (N_GROUPS, gsz, t), 0)
    gmask = jnp.zeros((N_GROUPS, gsz, t), F32)
    for _ in range(TOPK_GROUPS):
        mx = jnp.max(gscore, axis=0, keepdims=True)
        pick = jnp.min(jnp.where(gscore == mx, gi, N_GROUPS), axis=0, keepdims=True)
        hit = gi == pick
        gmask = jnp.where(hit, 1.0, gmask)
        gscore = jnp.where(hit, NEG_INF, gscore)
    cand = jnp.where(gmask.reshape(N_EXPERTS, t) > 0.0, s_sel, NEG_INF)
    ei = lax.broadcasted_iota(jnp.int32, (N_EXPERTS, t), 0)
    sel = jnp.zeros((N_EXPERTS, t), F32)
    picks = []
    for _ in range(TOP_K):
        mx = jnp.max(cand, axis=0, keepdims=True)
        pick = jnp.min(jnp.where(cand == mx, ei, N_EXPERTS), axis=0, keepdims=True)
        hit = ei == pick
        sel = jnp.where(hit, 1.0, sel)
        cand = jnp.where(hit, NEG_INF, cand)
        picks.append(pick)
    w = s * sel
    w = w / jnp.sum(w, axis=0, keepdims=True) * ROUTED_SCALE

    selb = sel.astype(BF16)
    tok_r = lax.broadcasted_iota(jnp.int32, (t, t), 0)
    tok_c = lax.broadcasted_iota(jnp.int32, (t, t), 1)
    rank = jnp.dot(selb, jnp.where(tok_r < tok_c, 1.0, 0.0).astype(BF16), preferred_element_type=F32)
    e_r = lax.broadcasted_iota(jnp.int32, (N_EXPERTS, N_EXPERTS), 0)
    e_c = lax.broadcasted_iota(jnp.int32, (N_EXPERTS, N_EXPERTS), 1)
    cnt_col = jnp.sum(sel, axis=1, keepdims=True)
    pad_col = jnp.floor((cnt_col + (SUBLANES - 1.0)) * (1.0 / SUBLANES))
    off_col = SUBLANES * jnp.dot(jnp.where(e_c < e_r, 1.0, 0.0).astype(BF16),
                                 jnp.broadcast_to(pad_col, (N_EXPERTS, LANES)).astype(BF16),
                                 preferred_element_type=F32)[:, 0:1]
    slot = rank + off_col
    dest_rows, w_rows = [], []
    for k in range(TOP_K):
        hit = ei == picks[k]
        dest_rows.append(jnp.sum(jnp.where(hit, slot, 0.0), axis=0, keepdims=True))
        w_rows.append(jnp.sum(jnp.where(hit, w, 0.0), axis=0, keepdims=True))
    dest_ref[0] = jnp.concatenate(dest_rows, axis=0).astype(jnp.int32)
    wk_ref[...] = jnp.concatenate(w_rows, axis=0)
    cnt_row = _bdot_nt(jnp.ones((SUBLANES, t), BF16), selb)
    pad_row = jnp.floor((cnt_row + (SUBLANES - 1.0)) * (1.0 / SUBLANES))
    off_row = SUBLANES * jnp.dot(pad_row.astype(BF16), jnp.where(e_r < e_c, 1.0, 0.0).astype(BF16),
                                 preferred_element_type=F32)
    seg_ref[0] = jnp.concatenate([off_row[0:1], cnt_row[0:1]], axis=1).astype(jnp.int32)


def _router(h2p, w_router, router_bias):
    n, dh = h2p.shape
    t = MOE_TILE
    nt = n // t
    return pl.pallas_call(
        _router_kernel,
        grid=(nt,),
        in_specs=[pl.BlockSpec((t, dh), lambda i: (i, 0)),
                  pl.BlockSpec((N_EXPERTS, 2 * dh), lambda i: (0, 0)),
                  pl.BlockSpec((N_EXPERTS, 1), lambda i: (0, 0))],
        out_specs=(pl.BlockSpec((1, TOP_K, t), lambda i: (i, 0, 0)),
                   pl.BlockSpec((TOP_K, t), lambda i: (0, i)),
                   pl.BlockSpec((1, 1, 2 * N_EXPERTS), lambda i: (i, 0, 0))),
        out_shape=(jax.ShapeDtypeStruct((nt, TOP_K, t), jnp.int32),
                   jax.ShapeDtypeStruct((TOP_K, n), F32),
                   jax.ShapeDtypeStruct((nt, 1, 2 * N_EXPERTS), jnp.int32)),
        compiler_params=_cparams("arbitrary"),
        name="router",
    )(h2p, w_router.T.astype(BF16), router_bias.reshape(N_EXPERTS, 1))


def _swiglu(x, wgu, wd):
    gu = jnp.dot(x, wgu, preferred_element_type=F32)
    g = gu[:, :D_EXPERT]
    a = (g * jax.nn.sigmoid(g)) * gu[:, D_EXPERT:]
    return jnp.dot(a.astype(BF16), wd, preferred_element_type=F32)


def _moe_kernel(seg_ref, hp_ref, dest_ref, wk_ref, wgu_ref, wd_ref, wsgu_ref, wsd_ref, x1_ref,
                mod_ref, g_ref, o_ref, xs_ref, acca_ref, accb_ref, dest_smem, w_smem, sem):
    i = pl.program_id(0)
    s = pl.program_id(1)
    n_s = pl.num_programs(1)
    t = MOE_TILE

    def slab(ref, row):
        return ref.at[pl.ds(pl.multiple_of(row * SLAB, SLAB), SLAB), :]

    def load_rows(ref, r0, n):
        return jnp.concatenate([ref[pl.ds(SLAB * r0 + c, n, stride=SLAB), :] for c in range(SLAB)],
                               axis=1)

    @pl.when((i == 0) & (s == 0))
    def _():
        xs_ref[...] = jnp.zeros_like(xs_ref)

    @pl.when(s == 0)
    def _():
        copies = (pltpu.make_async_copy(dest_ref, dest_smem, sem.at[0]),
                  pltpu.make_async_copy(wk_ref, w_smem, sem.at[1]))
        for cp in copies:
            cp.start()
        for cp in copies:
            cp.wait()

        def body(nb, c):
            n0 = nb * MOVE_TOKENS
            base = n0 * TOP_K
            for j in range(MOVE_TOKENS):
                row = slab(hp_ref, n0 + j)[...]
                for k in range(TOP_K):
                    slab(xs_ref, dest_smem[base + (j * TOP_K + k)])[...] = row
            return c

        lax.fori_loop(0, t // MOVE_TOKENS, body, 0)

    def put_rows(r0, end, y, old):
        new = _pack_bf16_pair(y)
        rows = r0 + lax.broadcasted_iota(jnp.int32, (FFN_ROWS, LANES), 0)
        for c in range(SLAB):
            cs = slice(c * LANES, (c + 1) * LANES)
            xs_ref[pl.ds(SLAB * r0 + c, FFN_ROWS, stride=SLAB), :] = jnp.where(rows < end, new[:, cs],
                                                                               old[:, cs])

    @pl.when(s < n_s - 1)
    def _():
        offs, ends = [], []
        for eb in range(EXPERTS_PER_STEP):
            e = s * EXPERTS_PER_STEP + eb
            offs.append(pl.multiple_of(seg_ref[i, e], SUBLANES))
            ends.append(offs[eb] + seg_ref[i, N_EXPERTS + e])
        olds = [load_rows(xs_ref, offs[eb], FFN_ROWS) for eb in range(EXPERTS_PER_STEP)]
        ys = [_swiglu(_unpack_rows_bf16(olds[eb]), wgu_ref[eb], wd_ref[eb])
              for eb in range(EXPERTS_PER_STEP)]
        for eb in range(EXPERTS_PER_STEP):
            put_rows(offs[eb], ends[eb], ys[eb], olds[eb])
        for eb in range(EXPERTS_PER_STEP):
            def chunk(c, carry, eb=eb):
                r0 = pl.multiple_of(offs[eb] + c * FFN_ROWS, SUBLANES)
                old = load_rows(xs_ref, r0, FFN_ROWS)
                put_rows(r0, ends[eb], _swiglu(_unpack_rows_bf16(old), wgu_ref[eb], wd_ref[eb]), old)
                return carry

            n_chunks = lax.div(ends[eb] - offs[eb] + (FFN_ROWS - 1), FFN_ROWS)
            lax.fori_loop(1, n_chunks, chunk, 0)

    @pl.when(s == n_s - 1)
    def _():
        o_ref[...] = _swiglu(_unpack_rows_bf16(load_rows(hp_ref, 0, t)), wsgu_ref[...], wsd_ref[...])

        def body(nb, c):
            n0 = pl.multiple_of(nb * SUBLANES, SUBLANES)
            base = n0 * TOP_K
            for j in range(SUBLANES):
                acc_a = jnp.zeros((SLAB, LANES), F32)
                acc_b = jnp.zeros((SLAB, LANES), F32)
                for k in range(TOP_K):
                    idx = base + (j * TOP_K + k)
                    a, b = _unpack_bf16_pair(slab(xs_ref, dest_smem[idx])[...])
                    w = w_smem[idx]
                    acc_a = acc_a + a * w
                    acc_b = acc_b + b * w
                acca_ref[j * SLAB:(j + 1) * SLAB, :] = acc_a
                accb_ref[j * SLAB:(j + 1) * SLAB, :] = acc_b
            lo = [acca_ref[pl.ds(c, SUBLANES, stride=SLAB), :] for c in range(SLAB)]
            hi = [accb_ref[pl.ds(c, SUBLANES, stride=SLAB), :] for c in range(SLAB)]
            o_ref[pl.ds(n0, SUBLANES), :] += jnp.concatenate(lo + hi, axis=1)
            return c

        lax.fori_loop(0, t // SUBLANES, body, 0)
        o_ref[...] = x1_ref[...] + mod_ref[0, 5:6, :] * _rms(o_ref[...], g_ref[...])


def _moe(h2p, dest, wk, seg, wgu, wd, wsgu, wsd, x1, mod3, g3, seq):
    n, dh = h2p.shape
    d = 2 * dh
    t = MOE_TILE
    n_steps = N_EXPERTS // EXPERTS_PER_STEP + 1
    tiles_per_seq = seq // t
    tok = lambda w: pl.BlockSpec((t, w), lambda i, s: (i, 0))
    table = pl.BlockSpec((TOP_K * t,), lambda i, s: (i,))
    wblk = lambda r, c: pl.BlockSpec((EXPERTS_PER_STEP, r, c),
                                     lambda i, s: (jnp.minimum(s, n_steps - 2), 0, 0))
    c2 = lambda shape: pl.BlockSpec(shape, lambda i, s: (0,) * len(shape))
    xs_rows = TOP_K * t + N_EXPERTS * SUBLANES + FFN_ROWS
    return pl.pallas_call(
        _moe_kernel,
        grid=(n // t, n_steps),
        in_specs=[pl.BlockSpec(memory_space=pltpu.SMEM),
                  pl.BlockSpec((SLAB * t, LANES), lambda i, s: (i, 0)),
                  table, table,
                  wblk(d, 2 * D_EXPERT), wblk(D_EXPERT, d),
                  c2((d, 2 * D_EXPERT)), c2((D_EXPERT, d)),
                  tok(d),
                  pl.BlockSpec((1, N_MOD, d), lambda i, s: (i // tiles_per_seq, 0, 0)),
                  c2((1, d))],
        out_specs=tok(d),
        out_shape=jax.ShapeDtypeStruct((n, d), F32),
        scratch_shapes=[pltpu.VMEM((SLAB * xs_rows, LANES), jnp.uint32),
                        pltpu.VMEM((SLAB * SUBLANES, LANES), F32),
                        pltpu.VMEM((SLAB * SUBLANES, LANES), F32),
                        pltpu.SMEM((TOP_K * t,), jnp.int32),
                        pltpu.SMEM((TOP_K * t,), F32),
                        pltpu.SemaphoreType.DMA((2,))],
        compiler_params=_cparams("arbitrary", "arbitrary"),
        name="moe",
    )(seg, h2p.reshape(n * SLAB, LANES), dest, wk, wgu, wd, wsgu, wsd, x1, mod3, g3)


def kernel(x, c, w_ada, b_ada, norm_gain, w_in, conv_w, conv_b, w_rg_a, b_rg_a, w_rg_x, b_rg_x,
           lru_lambda, w_br_rnn, w_br_attn, w_out, rel_bias, w_router, router_bias,
           w_exp_gate, w_exp_up, w_exp_down, w_sh_gate, w_sh_up, w_sh_down):
    bsz, s, d = x.shape
    depth = w_ada.shape[0]
    for l in range(depth):
        mod = _ada(c, w_ada[l], b_ada[l])
        mod3 = mod.reshape(bsz, N_MOD, d)
        gains = norm_gain[l]
        (u_rnn, u_gate, k, gl_rnn, gl_attn, ki, qt, vt, qit, wit) = _in_proj(x, mod3, gains[0:1], w_in[l])
        y_rnn = _rglru(u_rnn, u_gate, conv_w[l], conv_b[l], w_rg_a[l], b_rg_a[l], w_rg_x[l],
                       b_rg_x[l], lru_lambda[l])
        y_attn = _attention(qt, qit, wit, k, vt, ki, rel_bias)
        x1, h2p = _merge(y_rnn, y_attn, gl_rnn, gl_attn, x, mod3, gains, w_br_rnn[l], w_br_attn[l],
                         w_out[l])
        h2p = h2p.reshape(bsz * s, d // 2)
        dest, wk, seg = _router(h2p, w_router[l], router_bias[l])
        wgu = jnp.concatenate([w_exp_gate[l], w_exp_up[l]], axis=-1).astype(BF16)
        wsgu = jnp.concatenate([w_sh_gate[l], w_sh_up[l]], axis=-1).astype(BF16)
        dest = jnp.transpose(dest, (0, 2, 1)).reshape(-1)
        wk = jnp.transpose(wk).reshape(-1)
        x = _moe(h2p, dest, wk, seg.reshape(-1, 2 * N_EXPERTS), wgu, w_exp_down[l].astype(BF16),
                 wsgu, w_sh_down[l].astype(BF16), x1.reshape(bsz * s, d), mod3, gains[3:4],
                 s).reshape(bsz, s, d)
    return x
```

```python
import functools
import math

import jax
import jax.numpy as jnp
import numpy as np
from jax import lax
from jax.experimental import pallas as pl
from jax.experimental.pallas import tpu as pltpu

F32 = jnp.float32
BF16 = jnp.bfloat16

D_MODEL = 1024
RNN_BLOCKS = 8
RNN_BW = D_MODEL // RNN_BLOCKS
CONV_W = 4
LRU_C = 8.0
N_HEADS = 8
HEAD_DIM = 128
IDX_HEADS = 16
IDX_DIM = 64
TOPK_MAX = 256
N_BUCKETS = 32
MAX_DIST = 128
N_EXPERTS = 64
TOP_K = 8
N_GROUPS = 8
TOPK_GROUPS = 4
D_EXPERT = 256
ROUTED_SCALE = 2.5
N_MOD = 6
EPS = 1e-6

LANES = 128
SUBLANES = 8
VMEM_LIMIT_BYTES = 58 * 1024 * 1024

ROW_TILE = 512
UNCHECKED_BISECT_STEPS = 16
SCAN_TILE = 512
Q_TILE = 256
KEY_SUB = 128
KEY_GROUP = 256
DEN_ROWS = 16
MOE_TILE = 1024
EXPERTS_PER_STEP = 4
FFN_ROWS = 160
MOVE_TOKENS = 4
SLAB = 4

NEG_INF = float("-inf")
LOG2E = math.log2(math.e)


def _cparams(*sem):
    return pltpu.CompilerParams(dimension_semantics=sem, vmem_limit_bytes=VMEM_LIMIT_BYTES)


def _bdot(a, b):
    return jnp.dot(a.astype(BF16), b.astype(BF16), preferred_element_type=F32)


def _bdot_nt(a, b):
    return lax.dot_general(a.astype(BF16), b.astype(BF16), (((1,), (1,)), ((), ())),
                           preferred_element_type=F32)


def _rms(x, g):
    ms = jnp.mean(x * x, axis=-1, keepdims=True)
    return x * lax.rsqrt(ms + EPS) * g


def _ada_kernel(c_ref, w_ref, b_ref, o_ref):
    c = c_ref[...]
    cond = c * jax.nn.sigmoid(c)
    o_ref[...] = _bdot(cond, w_ref[...]) + b_ref[...]


def _ada(c, w_ada, b_ada):
    bsz, d = c.shape
    n = w_ada.shape[1]
    tn = 1024
    return pl.pallas_call(
        _ada_kernel,
        grid=(n // tn,),
        in_specs=[pl.BlockSpec((bsz, d), lambda j: (0, 0)),
                  pl.BlockSpec((d, tn), lambda j: (0, j)),
                  pl.BlockSpec((1, tn), lambda j: (0, j))],
        out_specs=pl.BlockSpec((bsz, tn), lambda j: (0, j)),
        out_shape=jax.ShapeDtypeStruct((bsz, n), F32),
        compiler_params=_cparams("arbitrary"),
        name="ada",
    )(c, w_ada, b_ada.reshape(1, n))


def _inproj_kernel(x_ref, mod_ref, g_ref, wn_ref, wki_ref, wt_ref, wwi_ref,
                   urnn_ref, ugate_ref, k_ref, glr_ref, gla_ref, ki_ref,
                   qt_ref, vt_ref, qit_ref, wit_ref):
    d = D_MODEL
    x = x_ref[0]
    h = _rms(x, g_ref[...]) * (1.0 + mod_ref[0, 1:2, :]) + mod_ref[0, 0:1, :]
    hb = h.astype(BF16)
    urnn_ref[0] = jnp.dot(hb, wn_ref[:, 0 * d:1 * d], preferred_element_type=F32)
    ugate_ref[0] = jnp.dot(hb, wn_ref[:, 1 * d:2 * d], preferred_element_type=F32)
    k_ref[0] = jnp.dot(hb, wn_ref[:, 2 * d:3 * d], preferred_element_type=F32).astype(BF16)
    glr_ref[0] = jnp.dot(hb, wn_ref[:, 3 * d:4 * d], preferred_element_type=F32)
    gla_ref[0] = jnp.dot(hb, wn_ref[:, 4 * d:5 * d], preferred_element_type=F32)
    ki_ref[0] = jnp.dot(hb, wki_ref[...], preferred_element_type=F32).astype(BF16)
    nt = (((1,), (1,)), ((), ()))
    qt_ref[0] = (lax.dot_general(wt_ref[0 * d:1 * d, :], hb, nt, preferred_element_type=F32)
                 * (HEAD_DIM ** -0.5 * LOG2E)).astype(BF16)
    vt_ref[0] = lax.dot_general(wt_ref[1 * d:2 * d, :], hb, nt, preferred_element_type=F32).astype(BF16)
    qit_ref[0] = lax.dot_general(wt_ref[2 * d:3 * d, :], hb, nt, preferred_element_type=F32).astype(BF16)
    wit_ref[0] = lax.dot_general(wwi_ref[...], hb, nt, preferred_element_type=F32)


def _in_proj(x, mod3, g0, w_in):
    bsz, s, d = x.shape
    tm = ROW_TILE
    offs = np.cumsum([0, d, d, d, d, d, IDX_HEADS * IDX_DIM, IDX_DIM, IDX_HEADS, d, d])
    seg = lambda i: w_in[:, int(offs[i]):int(offs[i + 1])]
    wn = jnp.concatenate([seg(0), seg(1), seg(3), seg(8), seg(9)], axis=1).astype(BF16)
    wki = seg(6).astype(BF16)
    wt = jnp.concatenate([seg(2), seg(4), seg(5)], axis=1).T.astype(BF16)
    wwi = seg(7).T.astype(BF16)
    const = lambda shape: pl.BlockSpec(shape, lambda b, i: (0,) * len(shape),
                                       pipeline_mode=pl.Buffered(1))
    row = lambda w: pl.BlockSpec((1, tm, w), lambda b, i: (b, i, 0))
    col = lambda r: pl.BlockSpec((1, r, tm), lambda b, i: (b, 0, i))
    out_shape = (
        jax.ShapeDtypeStruct((bsz, s, d), F32),
        jax.ShapeDtypeStruct((bsz, s, d), F32),
        jax.ShapeDtypeStruct((bsz, s, d), BF16),
        jax.ShapeDtypeStruct((bsz, s, d), F32),
        jax.ShapeDtypeStruct((bsz, s, d), F32),
        jax.ShapeDtypeStruct((bsz, s, IDX_DIM), BF16),
        jax.ShapeDtypeStruct((bsz, d, s), BF16),
        jax.ShapeDtypeStruct((bsz, d, s), BF16),
        jax.ShapeDtypeStruct((bsz, d, s), BF16),
        jax.ShapeDtypeStruct((bsz, IDX_HEADS, s), F32),
    )
    return pl.pallas_call(
        _inproj_kernel,
        grid=(bsz, s // tm),
        in_specs=[row(d),
                  pl.BlockSpec((1, N_MOD, d), lambda b, i: (b, 0, 0)),
                  pl.BlockSpec((1, d), lambda b, i: (0, 0)),
                  const(wn.shape), const(wki.shape), const(wt.shape), const(wwi.shape)],
        out_specs=(row(d), row(d), row(d), row(d), row(d), row(IDX_DIM),
                   col(d), col(d), col(d), col(IDX_HEADS)),
        out_shape=out_shape,
        compiler_params=_cparams("arbitrary", "arbitrary"),
        name="in_proj",
    )(x, mod3, g0, wn, wki, wt, wwi)


def _gelu_tanh(x):
    return 0.5 * x * (1.0 + jnp.tanh(math.sqrt(2.0 / math.pi) * (x + 0.044715 * (x * x * x))))


def _rglru_kernel(u_ref, ug_ref, cw_ref, cb_ref, wax_ref, ba_ref, bx_ref, lam_ref, y_ref,
                  ext_ref, a_ref, b_ref, carry_ref):
    ts = u_ref.shape[1]
    d = D_MODEL

    @pl.when(pl.program_id(1) == 0)
    def _():
        ext_ref[0:SUBLANES, :] = jnp.zeros((SUBLANES, d), F32)
        carry_ref[...] = jnp.zeros_like(carry_ref)

    ext_ref[SUBLANES:SUBLANES + ts, :] = u_ref[0]
    xc = cb_ref[...] + cw_ref[CONV_W - 1:CONV_W, :] * ext_ref[SUBLANES:SUBLANES + ts, :]
    for k in range(CONV_W - 1):
        off = SUBLANES - (CONV_W - 1) + k
        xc = xc + cw_ref[k:k + 1, :] * ext_ref[off:off + ts, :]
    ext_ref[0:SUBLANES, :] = ext_ref[ts:ts + SUBLANES, :]

    nl = -lam_ref[...]
    sp = jnp.maximum(nl, 0.0) + jnp.log1p(jnp.exp(-jnp.abs(nl)))
    for n in range(RNN_BLOCKS):
        cs = slice(n * RNN_BW, (n + 1) * RNN_BW)
        xb = xc[:, cs]
        g = _bdot(xb, wax_ref[n])
        r = jax.nn.sigmoid(g[:, :RNN_BW] + ba_ref[:, cs])
        i = jax.nn.sigmoid(g[:, RNN_BW:] + bx_ref[:, cs])
        log_a = (-LRU_C) * r * sp[:, cs]
        a_ref[:, cs] = jnp.exp(log_a)
        th = jnp.tanh(log_a)
        b_ref[:, cs] = jnp.sqrt(-2.0 * th / (1.0 - th)) * (i * xb)

    row = lax.broadcasted_iota(jnp.int32, (SUBLANES, d), 0)

    def group(gi, hprev):
        r0 = pl.multiple_of(gi * SUBLANES, SUBLANES)
        a = a_ref[pl.ds(r0, SUBLANES), :]
        b = b_ref[pl.ds(r0, SUBLANES), :]
        for sh in (1, 2, 4):
            keep = row >= sh
            a_s = jnp.where(keep, pltpu.roll(a, sh, 0), 1.0)
            b_s = jnp.where(keep, pltpu.roll(b, sh, 0), 0.0)
            b = a * b_s + b
            a = a * a_s
        h = b + a * hprev
        b_ref[pl.ds(r0, SUBLANES), :] = h
        return jnp.broadcast_to(h[SUBLANES - 1:SUBLANES, :], (SUBLANES, d))

    carry_ref[...] = lax.fori_loop(0, ts // SUBLANES, group, carry_ref[...])
    y_ref[0] = (b_ref[...] * _gelu_tanh(ug_ref[0])).astype(BF16)


def _rglru(u_rnn, u_gate, conv_w, conv_b, w_rg_a, b_rg_a, w_rg_x, b_rg_x, lam):
    bsz, s, d = u_rnn.shape
    ts = SCAN_TILE
    wax = jnp.concatenate([w_rg_a, w_rg_x], axis=-1).astype(BF16)
    vec = lambda v: v.reshape(1, d)
    c2 = lambda shape: pl.BlockSpec(shape, lambda b, i: (0,) * len(shape))
    row = pl.BlockSpec((1, ts, d), lambda b, i: (b, i, 0))
    return pl.pallas_call(
        _rglru_kernel,
        grid=(bsz, s // ts),
        in_specs=[row, row, c2((CONV_W, d)), c2((1, d)), c2(wax.shape), c2((1, d)), c2((1, d)),
                  c2((1, d))],
        out_specs=row,
        out_shape=jax.ShapeDtypeStruct((bsz, s, d), BF16),
        scratch_shapes=[pltpu.VMEM((ts + SUBLANES, d), F32), pltpu.VMEM((ts, d), F32),
                        pltpu.VMEM((ts, d), F32), pltpu.VMEM((SUBLANES, d), F32)],
        compiler_params=_cparams("arbitrary", "arbitrary"),
        name="rglru",
    )(u_rnn, u_gate, conv_w, vec(conv_b), wax, vec(b_rg_a), vec(b_rg_x), vec(lam))


def _t5_bucket_np(dist):
    max_exact = N_BUCKETS // 2
    dd = np.maximum(dist, 0)
    df = np.maximum(dd, 1).astype(np.float32)
    large = max_exact + (np.log(df / np.float32(max_exact)) / np.float32(math.log(MAX_DIST / max_exact))
                         * np.float32(N_BUCKETS - max_exact)).astype(np.int32)
    large = np.minimum(large, N_BUCKETS - 1)
    return np.where(dd < max_exact, dd, large)


def _near_bucket_ids():
    r = np.arange(KEY_GROUP)[None, :, None]
    c = np.arange(Q_TILE)[None, None, :]
    o = np.arange(2)[:, None, None]
    return _t5_bucket_np(c - r - (o - 1) * KEY_GROUP).astype(np.int32)


def _attn_kernel(rb_ref, bkt_ref, qt_ref, qit_ref, wit_ref, k_ref, vt_ref, ki_ref, y_ref,
                 s_ref, tab_ref, acc_ref, m_ref, lg_ref):
    tq = Q_TILE
    ks = KEY_SUB
    kg = KEY_GROUP
    jq = pl.program_id(1)
    t0 = jq * tq
    ngrp = jq + 1
    lane_t = t0 + lax.broadcasted_iota(jnp.int32, (1, tq), 1)

    @pl.when((pl.program_id(0) == 0) & (jq == 0))
    def _():
        for o in range(2):
            for h in range(N_HEADS):
                tab_ref[h, o] = jnp.zeros((kg, tq), F32)

            def fill(b, c):
                hit = bkt_ref[o] == b
                for h in range(N_HEADS):
                    val = (rb_ref[b, h] - rb_ref[N_BUCKETS - 1, h]) * LOG2E
                    tab_ref[h, o] = jnp.where(hit, val, tab_ref[h, o])
                return c

            lax.fori_loop(0, N_BUCKETS - 1, fill, 0)

    wi = wit_ref[0] * (IDX_HEADS ** -0.5 * IDX_DIM ** -0.5)

    def score_sub(i, mnmx, masked):
        r0 = pl.multiple_of(i * ks, ks)
        kic = ki_ref[0, pl.ds(r0, ks), :]
        acc = jnp.zeros((ks, tq), F32)
        for h in range(IDX_HEADS):
            dts = jnp.dot(kic, qit_ref[0, h * IDX_DIM:(h + 1) * IDX_DIM, :],
                          preferred_element_type=F32)
            acc = acc + jnp.maximum(dts, 0.0) * wi[h:h + 1, :]
        lo_src = acc
        if masked:
            key_s = r0 + lax.broadcasted_iota(jnp.int32, (ks, tq), 0)
            causal = key_s <= lane_t
            acc = jnp.where(causal, acc, NEG_INF)
            lo_src = jnp.where(causal, acc, jnp.inf)
        s_ref[pl.ds(r0, ks), :] = acc
        mn, mx = mnmx
        mn = jnp.minimum(mn, jnp.min(lo_src.reshape(ks // SUBLANES, SUBLANES, tq), axis=0))
        mx = jnp.maximum(mx, jnp.max(acc.reshape(ks // SUBLANES, SUBLANES, tq), axis=0))
        return mn, mx

    mnmx = (jnp.full((SUBLANES, tq), jnp.inf, F32), jnp.full((SUBLANES, tq), NEG_INF, F32))
    mnmx = lax.fori_loop(0, ngrp - 1,
                         lambda g, c: score_sub(2 * g + 1, score_sub(2 * g, c, False), False), mnmx)
    mnmx = score_sub(2 * ngrp - 2, mnmx, True)
    mn8, mx8 = score_sub(2 * ngrp - 1, mnmx, True)
    smin = jnp.min(mn8, axis=0, keepdims=True)
    smax = jnp.max(mx8, axis=0, keepdims=True)

    n_causal = (lane_t + 1).astype(F32)
    k_eff = jnp.minimum(n_causal, float(TOPK_MAX))

    def count_rows(pred):
        part = 2 * SUBLANES

        def block(start, rows, c):
            r0 = pl.multiple_of(start, rows)
            key_s = (r0 + lax.broadcasted_iota(jnp.int32, (rows, tq), 0)).astype(F32)
            ind = jnp.where(pred(s_ref[pl.ds(r0, rows), :], key_s), 1.0, 0.0)
            return c + jnp.sum(ind.reshape(rows // part, part, tq), axis=0)

        c = lax.fori_loop(0, ngrp // 2, lambda g, c: block(g * (2 * kg), 2 * kg, c),
                          jnp.zeros((part, tq), F32))
        c = lax.cond(ngrp % 2 == 1, lambda c: block((ngrp - 1) * kg, kg, c), lambda c: c, c)
        return jnp.sum(c, axis=0, keepdims=True)

    def bis_step(st):
        it, lo, hi, c_lo, done = st
        first = (jnp.zeros((1, tq), F32) + jnp.where(it == 0, 1.0, 0.0)) > 0.0
        probe = jnp.where(first, smax, 0.5 * lo + 0.5 * hi)
        collapsed = ~first & ((probe <= lo) | (probe >= hi))
        cnt = count_rows(lambda blk, _: blk >= probe)
        ge = cnt >= k_eff
        upd = (done == 0.0) & ~collapsed
        lo_n = jnp.where(upd & ge, probe, lo)
        c_lo_n = jnp.where(upd & ge, cnt, c_lo)
        hi_n = jnp.where(upd & ~ge, probe, hi)
        fin = collapsed | (cnt == k_eff) | (first & ge)
        done_n = jnp.where(fin, 1.0, done)
        return it + 1, lo_n, hi_n, c_lo_n, done_n

    def n_open(st):
        return jnp.sum(1.0 - st[4])

    done0 = jnp.where(n_causal <= k_eff, 1.0, 0.0)
    st = lax.fori_loop(0, UNCHECKED_BISECT_STEPS, lambda _, s: bis_step(s),
                       (jnp.int32(0), smin, smax, n_causal, done0))
    st = lax.while_loop(lambda s: s[5] > 0.0,
                        lambda s: (lambda nxt: nxt + (n_open(nxt),))(bis_step(s[:5])),
                        st + (n_open(st),))
    thr, c_thr = st[1], st[3]

    tie_all = jnp.zeros((1, tq), F32) + (t0 + tq).astype(F32)

    def tie_limit():
        need = k_eff - count_rows(lambda blk, _: blk > thr)

        def body(_, st):
            lo, hi = st
            mid = jnp.floor(0.5 * (lo + hi))
            ok = count_rows(lambda blk, key_s: (blk == thr) & (key_s < mid)) >= need
            return jnp.where(ok, lo, mid), jnp.where(ok, mid, hi)

        n_steps = int(math.ceil(math.log2(s_ref.shape[0]))) + 1
        return lax.fori_loop(0, n_steps, body, (jnp.zeros((1, tq), F32), tie_all))[1]

    excess = jnp.sum(jnp.where(c_thr > k_eff, 1.0, 0.0))
    tie_lim = lax.cond(excess > 0.0, tie_limit, lambda: tie_all)

    def mask_body(g, c):
        r0 = pl.multiple_of(g * kg, kg)
        blk = s_ref[pl.ds(r0, kg), :]
        key_s = (r0 + lax.broadcasted_iota(jnp.int32, (kg, tq), 0)).astype(F32)
        sel = (blk > thr) | ((blk == thr) & (key_s < tie_lim))
        s_ref[pl.ds(r0, kg), :] = jnp.where(sel, 0.0, NEG_INF)
        return c

    lax.fori_loop(0, ngrp, mask_body, 0)

    m_ref[...] = jnp.full_like(m_ref, NEG_INF)
    acc_ref[...] = jnp.zeros_like(acc_ref)
    ones_rows = jnp.ones((DEN_ROWS, kg), BF16)

    def attend(g, near):
        r0 = pl.multiple_of(g * kg, kg)
        msk = s_ref[pl.ds(r0, kg), :]
        col_max = []
        for h in range(N_HEADS):
            hs = slice(h * HEAD_DIM, (h + 1) * HEAD_DIM)
            kh = k_ref[0, pl.ds(r0, kg), hs]
            lg = jnp.dot(kh, qt_ref[0, hs, :], preferred_element_type=F32) + msk
            if near is not None:
                lg = lg + tab_ref[h, near]
            lg_ref[h] = lg
            col_max.append(jnp.max(lg, axis=0, keepdims=True))
        for h in range(N_HEADS):
            hs = slice(h * HEAD_DIM, (h + 1) * HEAD_DIM)
            m_old = m_ref[h:h + 1, :]
            m_new = jnp.maximum(m_old, col_max[h])
            m_safe = jnp.where(m_new == NEG_INF, 0.0, m_new)
            p = jnp.exp2(lg_ref[h] - m_safe).astype(BF16)
            alpha = jnp.exp2(m_old - m_safe)
            m_ref[h:h + 1, :] = m_new
            vh = jnp.concatenate([vt_ref[0, hs, pl.ds(r0, kg)], ones_rows], axis=0)
            acc_ref[h] = alpha * acc_ref[h] + jnp.dot(vh, p, preferred_element_type=F32)

    n_far = jnp.maximum(ngrp - 2, 0)

    def _attend_pair(gp, c):
        attend(2 * gp, None)
        attend(2 * gp + 1, None)
        return c

    lax.fori_loop(0, n_far // 2, _attend_pair, 0)

    @pl.when(n_far % 2 == 1)
    def _():
        attend(n_far - 1, None)

    @pl.when(jq > 0)
    def _():
        attend(ngrp - 2, 0)

    attend(ngrp - 1, 1)

    for h in range(N_HEADS):
        o = acc_ref[h, 0:HEAD_DIM, :] / acc_ref[h, HEAD_DIM:HEAD_DIM + 1, :]
        y_ref[0, :, h * HEAD_DIM:(h + 1) * HEAD_DIM] = o.T.astype(BF16)


def _attention(qt, qit, wit, k, vt, ki, rel_bias):
    bsz, d, s = qt.shape
    tq = Q_TILE
    bkt = jnp.asarray(_near_bucket_ids())
    once = lambda shape: pl.BlockSpec(shape, lambda b, j: (b,) + (0,) * (len(shape) - 1),
                                      pipeline_mode=pl.Buffered(1))
    col = lambda r: pl.BlockSpec((1, r, tq), lambda b, j: (b, 0, j))
    return pl.pallas_call(
        _attn_kernel,
        grid=(bsz, s // tq),
        in_specs=[pl.BlockSpec(memory_space=pltpu.SMEM),
                  pl.BlockSpec(bkt.shape, lambda b, j: (0, 0, 0)),
                  col(d), col(d), col(IDX_HEADS),
                  once((1, s, d)), once((1, d, s)), once((1, s, IDX_DIM))],
        out_specs=pl.BlockSpec((1, tq, d), lambda b, j: (b, j, 0)),
        out_shape=jax.ShapeDtypeStruct((bsz, s, d), BF16),
        scratch_shapes=[pltpu.VMEM((s, tq), F32),
                        pltpu.VMEM((N_HEADS, 2, KEY_GROUP, tq), F32),
                        pltpu.VMEM((N_HEADS, HEAD_DIM + DEN_ROWS, tq), F32),
                        pltpu.VMEM((N_HEADS, tq), F32),
                        pltpu.VMEM((N_HEADS, KEY_GROUP, tq), F32)],
        compiler_params=_cparams("arbitrary", "arbitrary"),
        name="attn",
    )(rel_bias, bkt, qt, qit, wit, k, vt, ki)


def _pack_bf16_pair(x):
    c = x.shape[1] // 2
    lo = lax.bitcast_convert_type(x[:, :c].astype(BF16).astype(F32), jnp.uint32) >> 16
    hi = lax.bitcast_convert_type(x[:, c:].astype(BF16).astype(F32), jnp.uint32) & jnp.uint32(0xFFFF0000)
    return lo | hi


def _unpack_bf16_pair(p):
    a = lax.bitcast_convert_type(p << 16, F32)
    b = lax.bitcast_convert_type(p & jnp.uint32(0xFFFF0000), F32)
    return a, b


def _unpack_rows_bf16(p):
    a, b = _unpack_bf16_pair(p)
    return jnp.concatenate([a, b], axis=1).astype(BF16)


def _merge_kernel(yr_ref, ya_ref, glr_ref, gla_ref, x_ref, mod_ref, g_ref,
                  wr_ref, wa_ref, wo_ref, x1_ref, h2p_ref):
    merged = (jax.nn.sigmoid(glr_ref[0]) * jnp.dot(yr_ref[0], wr_ref[...], preferred_element_type=F32)
              + jax.nn.sigmoid(gla_ref[0]) * jnp.dot(ya_ref[0], wa_ref[...], preferred_element_type=F32))
    y = _bdot(merged, wo_ref[...])
    x1 = x_ref[0] + mod_ref[0, 2:3, :] * _rms(y, g_ref[1:2, :])
    x1_ref[0] = x1
    h2 = _rms(x1, g_ref[2:3, :]) * (1.0 + mod_ref[0, 4:5, :]) + mod_ref[0, 3:4, :]
    h2p_ref[0] = _pack_bf16_pair(h2)


def _merge(y_rnn, y_attn, gl_rnn, gl_attn, x, mod3, gains, w_br_rnn, w_br_attn, w_out):
    bsz, s, d = x.shape
    tm = ROW_TILE
    row = pl.BlockSpec((1, tm, d), lambda b, i: (b, i, 0))
    half = pl.BlockSpec((1, tm, d // 2), lambda b, i: (b, i, 0))
    c2 = lambda shape: pl.BlockSpec(shape, lambda b, i: (0,) * len(shape))
    return pl.pallas_call(
        _merge_kernel,
        grid=(bsz, s // tm),
        in_specs=[row, row, row, row, row,
                  pl.BlockSpec((1, N_MOD, d), lambda b, i: (b, 0, 0)),
                  c2(gains.shape), c2((d, d)), c2((d, d)), c2((d, d))],
        out_specs=(row, half),
        out_shape=(jax.ShapeDtypeStruct((bsz, s, d), F32),
                   jax.ShapeDtypeStruct((bsz, s, d // 2), jnp.uint32)),
        compiler_params=_cparams("arbitrary", "arbitrary"),
        name="merge",
    )(y_rnn, y_attn, gl_rnn, gl_attn, x, mod3, gains,
      w_br_rnn.astype(BF16), w_br_attn.astype(BF16), w_out.astype(BF16))


def _router_kernel(hp_ref, wr_ref, rb_ref, dest_ref, wk_ref, seg_ref):
    t = hp_ref.shape[0]
    gsz = N_EXPERTS // N_GROUPS
    h = _unpack_rows_bf16(hp_ref[...])
    s = jax.nn.sigmoid(_bdot_nt(wr_ref[...], h))
    s_sel = s + rb_ref[...]
    g3 = s_sel.reshape(N_GROUPS, gsz, t)
    e_in_g = lax.broadcasted_iota(jnp.int32, (N_GROUPS, gsz, t), 1)
    top1 = jnp.max(g3, axis=1, keepdims=True)
    first = jnp.min(jnp.where(g3 == top1, e_in_g, gsz), axis=1, keepdims=True)
    top2 = jnp.max(jnp.where(e_in_g == first, NEG_INF, g3), axis=1, keepdims=True)
    gscore = jnp.broadcast_to(top1 + top2, (N_GROUPS, gsz, t))
    gi = lax.broadcasted_iota(jnp.int32, (N_GROUPS, gsz, t), 0)
    gmask = jnp.zeros((N_GROUPS, gsz, t), F32)
    for _ in range(TOPK_GROUPS):
        mx = jnp.max(gscore, axis=0, keepdims=True)
        pick = jnp.min(jnp.where(gscore == mx, gi, N_GROUPS), axis=0, keepdims=True)
        hit = gi == pick
        gmask = jnp.where(hit, 1.0, gmask)
        gscore = jnp.where(hit, NEG_INF, gscore)
    cand = jnp.where(gmask.reshape(N_EXPERTS, t) > 0.0, s_sel, NEG_INF)
    ei = lax.broadcasted_iota(jnp.int32, (N_EXPERTS, t), 0)
    sel = jnp.zeros((N_EXPERTS, t), F32)
    picks = []
    for _ in range(TOP_K):
        mx = jnp.max(cand, axis=0, keepdims=True)
        pick = jnp.min(jnp.where(cand == mx, ei, N_EXPERTS), axis=0, keepdims=True)
        hit = ei == pick
        sel = jnp.where(hit, 1.0, sel)
        cand = jnp.where(hit, NEG_INF, cand)
        picks.append(pick)
    w = s * sel
    w = w / jnp.sum(w, axis=0, keepdims=True) * ROUTED_SCALE

    selb = sel.astype(BF16)
    tok_r = lax.broadcasted_iota(jnp.int32, (t, t), 0)
    tok_c = lax.broadcasted_iota(jnp.int32, (t, t), 1)
    rank = jnp.dot(selb, jnp.where(tok_r < tok_c, 1.0, 0.0).astype(BF16), preferred_element_type=F32)
    e_r = lax.broadcasted_iota(jnp.int32, (N_EXPERTS, N_EXPERTS), 0)
    e_c = lax.broadcasted_iota(jnp.int32, (N_EXPERTS, N_EXPERTS), 1)
    cnt_col = jnp.sum(sel, axis=1, keepdims=True)
    pad_col = jnp.floor((cnt_col + (SUBLANES - 1.0)) * (1.0 / SUBLANES))
    off_col = SUBLANES * jnp.dot(jnp.where(e_c < e_r, 1.0, 0.0).astype(BF16),
                                 jnp.broadcast_to(pad_col, (N_EXPERTS, LANES)).astype(BF16),
                                 preferred_element_type=F32)[:, 0:1]
    slot = rank + off_col
    dest_rows, w_rows = [], []
    for k in range(TOP_K):
        hit = ei == picks[k]
        dest_rows.append(jnp.sum(jnp.where(hit, slot, 0.0), axis=0, keepdims=True))
        w_rows.append(jnp.sum(jnp.where(hit, w, 0.0), axis=0, keepdims=True))
    dest_ref[0] = jnp.concatenate(dest_rows, axis=0).astype(jnp.int32)
    wk_ref[...] = jnp.concatenate(w_rows, axis=0)
    cnt_row = _bdot_nt(jnp.ones((SUBLANES, t), BF16), selb)
    pad_row = jnp.floor((cnt_row + (SUBLANES - 1.0)) * (1.0 / SUBLANES))
    off_row = SUBLANES * jnp.dot(pad_row.astype(BF16), jnp.where(e_r < e_c, 1.0, 0.0).astype(BF16),
                                 preferred_element_type=F32)
    seg_ref[0] = jnp.concatenate([off_row[0:1], cnt_row[0:1]], axis=1).astype(jnp.int32)


def _router(h2p, w_router, router_bias):
    n, dh = h2p.shape
    t = MOE_TILE
    nt = n // t
    return pl.pallas_call(
        _router_kernel,
        grid=(nt,),
        in_specs=[pl.BlockSpec((t, dh), lambda i: (i, 0)),
                  pl.BlockSpec((N_EXPERTS, 2 * dh), lambda i: (0, 0)),
                  pl.BlockSpec((N_EXPERTS, 1), lambda i: (0, 0))],
        out_specs=(pl.BlockSpec((1, TOP_K, t), lambda i: (i, 0, 0)),
                   pl.BlockSpec((TOP_K, t), lambda i: (0, i)),
                   pl.BlockSpec((1, 1, 2 * N_EXPERTS), lambda i: (i, 0, 0))),
        out_shape=(jax.ShapeDtypeStruct((nt, TOP_K, t), jnp.int32),
                   jax.ShapeDtypeStruct((TOP_K, n), F32),
                   jax.ShapeDtypeStruct((nt, 1, 2 * N_EXPERTS), jnp.int32)),
        compiler_params=_cparams("arbitrary"),
        name="router",
    )(h2p, w_router.T.astype(BF16), router_bias.reshape(N_EXPERTS, 1))


def _swiglu(x, wgu, wd):
    gu = jnp.dot(x, wgu, preferred_element_type=F32)
    g = gu[:, :D_EXPERT]
    a = (g * jax.nn.sigmoid(g)) * gu[:, D_EXPERT:]
    return jnp.dot(a.astype(BF16), wd, preferred_element_type=F32)


def _expert_block(tile, step, n_blocks):
    sc = jnp.minimum(step, n_blocks - 1)
    return jnp.where(tile % 2 == 0, sc, n_blocks - 1 - sc)


def _moe_kernel(seg_ref, hp_ref, dest_ref, wk_ref, wgu_ref, wd_ref, wsgu_ref, wsd_ref, x1_ref,
                mod_ref, g_ref, o_ref, xs_ref, acca_ref, accb_ref, act_ref, dest_smem, w_smem, sem):
    i = pl.program_id(0)
    s = pl.program_id(1)
    n_s = pl.num_programs(1)
    t = MOE_TILE

    def slab(ref, row):
        return ref.at[pl.ds(pl.multiple_of(row * SLAB, SLAB), SLAB), :]

    def load_rows(ref, r0, n):
        return jnp.concatenate([ref[pl.ds(SLAB * r0 + c, n, stride=SLAB), :] for c in range(SLAB)],
                               axis=1)

    @pl.when((i == 0) & (s == 0))
    def _():
        xs_ref[...] = jnp.zeros_like(xs_ref)

    @pl.when(s == 0)
    def _():
        copies = (pltpu.make_async_copy(dest_ref, dest_smem, sem.at[0]),
                  pltpu.make_async_copy(wk_ref, w_smem, sem.at[1]))
        for cp in copies:
            cp.start()
        for cp in copies:
            cp.wait()

        def body(nb, c):
            n0 = nb * MOVE_TOKENS
            base = n0 * TOP_K
            for j in range(MOVE_TOKENS):
                row = slab(hp_ref, n0 + j)[...]
                for k in range(TOP_K):
                    slab(xs_ref, dest_smem[base + (j * TOP_K + k)])[...] = row
            return c

        lax.fori_loop(0, t // MOVE_TOKENS, body, 0)

    def put_rows(r0, end, y, old):
        new = _pack_bf16_pair(y)
        rows = r0 + lax.broadcasted_iota(jnp.int32, (FFN_ROWS, LANES), 0)
        for c in range(SLAB):
            cs = slice(c * LANES, (c + 1) * LANES)
            xs_ref[pl.ds(SLAB * r0 + c, FFN_ROWS, stride=SLAB), :] = jnp.where(rows < end, new[:, cs],
                                                                               old[:, cs])

    @pl.when(s < n_s - 1)
    def _():
        offs, ends = [], []
        for eb in range(EXPERTS_PER_STEP):
            e = _expert_block(i, s, n_s - 1) * EXPERTS_PER_STEP + eb
            offs.append(pl.multiple_of(seg_ref[i, e], SUBLANES))
            ends.append(offs[eb] + seg_ref[i, N_EXPERTS + e])
        olds = [load_rows(xs_ref, offs[eb], FFN_ROWS) for eb in range(EXPERTS_PER_STEP)]
        for eb in range(EXPERTS_PER_STEP):
            gu = jnp.dot(_unpack_rows_bf16(olds[eb]), wgu_ref[eb], preferred_element_type=F32)
            g = gu[:, :D_EXPERT]
            act_ref[eb] = ((g * jax.nn.sigmoid(g)) * gu[:, D_EXPERT:]).astype(BF16)
        ys = [jnp.dot(act_ref[eb], wd_ref[eb], preferred_element_type=F32)
              for eb in range(EXPERTS_PER_STEP)]
        for eb in range(EXPERTS_PER_STEP):
            put_rows(offs[eb], ends[eb], ys[eb], olds[eb])
        for eb in range(EXPERTS_PER_STEP):
            def chunk(c, carry, eb=eb):
                r0 = pl.multiple_of(offs[eb] + c * FFN_ROWS, SUBLANES)
                old = load_rows(xs_ref, r0, FFN_ROWS)
                put_rows(r0, ends[eb], _swiglu(_unpack_rows_bf16(old), wgu_ref[eb], wd_ref[eb]), old)
                return carry

            n_chunks = lax.div(ends[eb] - offs[eb] + (FFN_ROWS - 1), FFN_ROWS)
            lax.fori_loop(1, n_chunks, chunk, 0)

    @pl.when(s == n_s - 1)
    def _():
        o_ref[...] = _swiglu(_unpack_rows_bf16(load_rows(hp_ref, 0, t)), wsgu_ref[...], wsd_ref[...])

        def body(nb, c):
            n0 = pl.multiple_of(nb * SUBLANES, SUBLANES)
            base = n0 * TOP_K
            for j in range(SUBLANES):
                acc_a = jnp.zeros((SLAB, LANES), F32)
                acc_b = jnp.zeros((SLAB, LANES), F32)
                for k in range(TOP_K):
                    idx = base + (j * TOP_K + k)
                    a, b = _unpack_bf16_pair(slab(xs_ref, dest_smem[idx])[...])
                    w = w_smem[idx]
                    acc_a = acc_a + a * w
                    acc_b = acc_b + b * w
                acca_ref[j * SLAB:(j + 1) * SLAB, :] = acc_a
                accb_ref[j * SLAB:(j + 1) * SLAB, :] = acc_b
            lo = [acca_ref[pl.ds(c, SUBLANES, stride=SLAB), :] for c in range(SLAB)]
            hi = [accb_ref[pl.ds(c, SUBLANES, stride=SLAB), :] for c in range(SLAB)]
            o_ref[pl.ds(n0, SUBLANES), :] += jnp.concatenate(lo + hi, axis=1)
            return c

        lax.fori_loop(0, t // SUBLANES, body, 0)
        o_ref[...] = x1_ref[...] + mod_ref[0, 5:6, :] * _rms(o_ref[...], g_ref[...])


def _moe(h2p, dest, wk, seg, wgu, wd, wsgu, wsd, x1, mod3, g3, seq):
    n, dh = h2p.shape
    d = 2 * dh
    t = MOE_TILE
    n_steps = N_EXPERTS // EXPERTS_PER_STEP + 1
    tiles_per_seq = seq // t
    tok = lambda w: pl.BlockSpec((t, w), lambda i, s: (i, 0))
    table = pl.BlockSpec((TOP_K * t,), lambda i, s: (i,))
    wblk = lambda r, c: pl.BlockSpec((EXPERTS_PER_STEP, r, c),
                                     lambda i, s: (_expert_block(i, s, n_steps - 1), 0, 0))
    c2 = lambda shape: pl.BlockSpec(shape, lambda i, s: (0,) * len(shape))
    xs_rows = TOP_K * t + N_EXPERTS * SUBLANES + FFN_ROWS
    return pl.pallas_call(
        _moe_kernel,
        grid=(n // t, n_steps),
        in_specs=[pl.BlockSpec(memory_space=pltpu.SMEM),
                  pl.BlockSpec((SLAB * t, LANES), lambda i, s: (i, 0)),
                  table, table,
                  wblk(d, 2 * D_EXPERT), wblk(D_EXPERT, d),
                  c2((d, 2 * D_EXPERT)), c2((D_EXPERT, d)),
                  tok(d),
                  pl.BlockSpec((1, N_MOD, d), lambda i, s: (i // tiles_per_seq, 0, 0)),
                  c2((1, d))],
        out_specs=tok(d),
        out_shape=jax.ShapeDtypeStruct((n, d), F32),
        scratch_shapes=[pltpu.VMEM((SLAB * xs_rows, LANES), jnp.uint32),
                        pltpu.VMEM((SLAB * SUBLANES, LANES), F32),
                        pltpu.VMEM((SLAB * SUBLANES, LANES), F32),
                        pltpu.VMEM((EXPERTS_PER_STEP, FFN_ROWS, D_EXPERT), BF16),
                        pltpu.SMEM((TOP_K * t,), jnp.int32),
                        pltpu.SMEM((TOP_K * t,), F32),
                        pltpu.SemaphoreType.DMA((2,))],
        compiler_params=_cparams("arbitrary", "arbitrary"),
        name="moe",
    )(seg, h2p.reshape(n * SLAB, LANES), dest, wk, wgu, wd, wsgu, wsd, x1, mod3, g3)


def kernel(x, c, w_ada, b_ada, norm_gain, w_in, conv_w, conv_b, w_rg_a, b_rg_a, w_rg_x, b_rg_x,
           lru_lambda, w_br_rnn, w_br_attn, w_out, rel_bias, w_router, router_bias,
           w_exp_gate, w_exp_up, w_exp_down, w_sh_gate, w_sh_up, w_sh_down):
    bsz, s, d = x.shape
    depth = w_ada.shape[0]
    for l in range(depth):
        mod = _ada(c, w_ada[l], b_ada[l])
        mod3 = mod.reshape(bsz, N_MOD, d)
        gains = norm_gain[l]
        (u_rnn, u_gate, k, gl_rnn, gl_attn, ki, qt, vt, qit, wit) = _in_proj(x, mod3, gains[0:1], w_in[l])
        y_rnn = _rglru(u_rnn, u_gate, conv_w[l], conv_b[l], w_rg_a[l], b_rg_a[l], w_rg_x[l],
                       b_rg_x[l], lru_lambda[l])
        y_attn = _attention(qt, qit, wit, k, vt, ki, rel_bias)
        x1, h2p = _merge(y_rnn, y_attn, gl_rnn, gl_attn, x, mod3, gains, w_br_rnn[l], w_br_attn[l],
                         w_out[l])
        h2p = h2p.reshape(bsz * s, d // 2)
        dest, wk, seg = _router(h2p, w_router[l], router_bias[l])
        wgu = jnp.concatenate([w_exp_gate[l], w_exp_up[l]], axis=-1).astype(BF16)
        wsgu = jnp.concatenate([w_sh_gate[l], w_sh_up[l]], axis=-1).astype(BF16)
        dest = jnp.transpose(dest, (0, 2, 1)).reshape(-1)
        wk = jnp.transpose(wk).reshape(-1)
        x = _moe(h2p, dest, wk, seg.reshape(-1, 2 * N_EXPERTS), wgu, w_exp_down[l].astype(BF16),
                 wsgu, w_sh_down[l].astype(BF16), x1.reshape(bsz * s, d), mod3, gains[3:4],
                 s).reshape(bsz, s, d)
    return x
```

```python
import functools
import math

import jax
import jax.numpy as jnp
import numpy as np
from jax import lax
from jax.experimental import pallas as pl
from jax.experimental.pallas import tpu as pltpu

F32 = jnp.float32
BF16 = jnp.bfloat16

D_MODEL = 1024
RNN_BLOCKS = 8
RNN_BW = D_MODEL // RNN_BLOCKS
CONV_W = 4
LRU_C = 8.0
N_HEADS = 8
HEAD_DIM = 128
IDX_HEADS = 16
IDX_DIM = 64
TOPK_MAX = 256
N_BUCKETS = 32
MAX_DIST = 128
N_EXPERTS = 64
TOP_K = 8
N_GROUPS = 8
TOPK_GROUPS = 4
D_EXPERT = 256
ROUTED_SCALE = 2.5
N_MOD = 6
EPS = 1e-6

LANES = 128
SUBLANES = 8
VMEM_LIMIT_BYTES = 58 * 1024 * 1024

ROW_TILE = 512
UNCHECKED_BISECT_STEPS = 16
SCAN_TILE = 512
Q_TILE = 256
KEY_SUB = 128
KEY_GROUP = 256
DEN_ROWS = 16
MOE_TILE = 1024
EXPERTS_PER_STEP = 4
FFN_ROWS = 160
MOVE_TOKENS = 4
SLAB = 4

NEG_INF = float("-inf")
LOG2E = math.log2(math.e)


def _cparams(*sem):
    return pltpu.CompilerParams(dimension_semantics=sem, vmem_limit_bytes=VMEM_LIMIT_BYTES)


def _bdot(a, b):
    return jnp.dot(a.astype(BF16), b.astype(BF16), preferred_element_type=F32)


def _bdot_nt(a, b):
    return lax.dot_general(a.astype(BF16), b.astype(BF16), (((1,), (1,)), ((), ())),
                           preferred_element_type=F32)


def _rms(x, g):
    ms = jnp.mean(x * x, axis=-1, keepdims=True)
    return x * lax.rsqrt(ms + EPS) * g


def _ada_kernel(c_ref, w_ref, b_ref, o_ref):
    c = c_ref[...]
    cond = c * jax.nn.sigmoid(c)
    o_ref[...] = _bdot(cond, w_ref[...]) + b_ref[...]


def _ada(c, w_ada, b_ada):
    bsz, d = c.shape
    n = w_ada.shape[1]
    tn = 1024
    return pl.pallas_call(
        _ada_kernel,
        grid=(n // tn,),
        in_specs=[pl.BlockSpec((bsz, d), lambda j: (0, 0)),
                  pl.BlockSpec((d, tn), lambda j: (0, j)),
                  pl.BlockSpec((1, tn), lambda j: (0, j))],
        out_specs=pl.BlockSpec((bsz, tn), lambda j: (0, j)),
        out_shape=jax.ShapeDtypeStruct((bsz, n), F32),
        compiler_params=_cparams("arbitrary"),
        name="ada",
    )(c, w_ada, b_ada.reshape(1, n))


def _inproj_kernel(x_ref, mod_ref, g_ref, wn_ref, wki_ref, wt_ref, wwi_ref,
                   urnn_ref, ugate_ref, k_ref, glr_ref, gla_ref, ki_ref,
                   qt_ref, vt_ref, qit_ref, wit_ref):
    d = D_MODEL
    x = x_ref[0]
    h = _rms(x, g_ref[...]) * (1.0 + mod_ref[0, 1:2, :]) + mod_ref[0, 0:1, :]
    hb = h.astype(BF16)
    urnn_ref[0] = jnp.dot(hb, wn_ref[:, 0 * d:1 * d], preferred_element_type=F32)
    ugate_ref[0] = jnp.dot(hb, wn_ref[:, 1 * d:2 * d], preferred_element_type=F32)
    k_ref[0] = jnp.dot(hb, wn_ref[:, 2 * d:3 * d], preferred_element_type=F32).astype(BF16)
    glr_ref[0] = jnp.dot(hb, wn_ref[:, 3 * d:4 * d], preferred_element_type=F32)
    gla_ref[0] = jnp.dot(hb, wn_ref[:, 4 * d:5 * d], preferred_element_type=F32)
    ki_ref[0] = jnp.dot(hb, wki_ref[...], preferred_element_type=F32).astype(BF16)
    nt = (((1,), (1,)), ((), ()))
    qt_ref[0] = (lax.dot_general(wt_ref[0 * d:1 * d, :], hb, nt, preferred_element_type=F32)
                 * (HEAD_DIM ** -0.5 * LOG2E)).astype(BF16)
    vt_ref[0] = lax.dot_general(wt_ref[1 * d:2 * d, :], hb, nt, preferred_element_type=F32).astype(BF16)
    qit_ref[0] = lax.dot_general(wt_ref[2 * d:3 * d, :], hb, nt, preferred_element_type=F32).astype(BF16)
    wit_ref[0] = lax.dot_general(wwi_ref[...], hb, nt, preferred_element_type=F32)


def _in_proj(x, mod3, g0, w_in):
    bsz, s, d = x.shape
    tm = ROW_TILE
    offs = np.cumsum([0, d, d, d, d, d, IDX_HEADS * IDX_DIM, IDX_DIM, IDX_HEADS, d, d])
    seg = lambda i: w_in[:, int(offs[i]):int(offs[i + 1])]
    wn = jnp.concatenate([seg(0), seg(1), seg(3), seg(8), seg(9)], axis=1).astype(BF16)
    wki = seg(6).astype(BF16)
    wt = jnp.concatenate([seg(2), seg(4), seg(5)], axis=1).T.astype(BF16)
    wwi = seg(7).T.astype(BF16)
    const = lambda shape: pl.BlockSpec(shape, lambda b, i: (0,) * len(shape),
                                       pipeline_mode=pl.Buffered(1))
    row = lambda w: pl.BlockSpec((1, tm, w), lambda b, i: (b, i, 0))
    col = lambda r: pl.BlockSpec((1, r, tm), lambda b, i: (b, 0, i))
    out_shape = (
        jax.ShapeDtypeStruct((bsz, s, d), F32),
        jax.ShapeDtypeStruct((bsz, s, d), F32),
        jax.ShapeDtypeStruct((bsz, s, d), BF16),
        jax.ShapeDtypeStruct((bsz, s, d), F32),
        jax.ShapeDtypeStruct((bsz, s, d), F32),
        jax.ShapeDtypeStruct((bsz, s, IDX_DIM), BF16),
        jax.ShapeDtypeStruct((bsz, d, s), BF16),
        jax.ShapeDtypeStruct((bsz, d, s), BF16),
        jax.ShapeDtypeStruct((bsz, d, s), BF16),
        jax.ShapeDtypeStruct((bsz, IDX_HEADS, s), F32),
    )
    return pl.pallas_call(
        _inproj_kernel,
        grid=(bsz, s // tm),
        in_specs=[row(d),
                  pl.BlockSpec((1, N_MOD, d), lambda b, i: (b, 0, 0)),
                  pl.BlockSpec((1, d), lambda b, i: (0, 0)),
                  const(wn.shape), const(wki.shape), const(wt.shape), const(wwi.shape)],
        out_specs=(row(d), row(d), row(d), row(d), row(d), row(IDX_DIM),
                   col(d), col(d), col(d), col(IDX_HEADS)),
        out_shape=out_shape,
        compiler_params=_cparams("arbitrary", "arbitrary"),
        name="in_proj",
    )(x, mod3, g0, wn, wki, wt, wwi)


def _gelu_tanh(x):
    return 0.5 * x * (1.0 + jnp.tanh(math.sqrt(2.0 / math.pi) * (x + 0.044715 * (x * x * x))))


def _rglru_kernel(u_ref, ug_ref, cw_ref, cb_ref, wax_ref, ba_ref, bx_ref, lam_ref, y_ref,
                  ext_ref, a_ref, b_ref, carry_ref):
    ts = u_ref.shape[1]
    d = D_MODEL

    @pl.when(pl.program_id(1) == 0)
    def _():
        ext_ref[0:SUBLANES, :] = jnp.zeros((SUBLANES, d), F32)
        carry_ref[...] = jnp.zeros_like(carry_ref)

    ext_ref[SUBLANES:SUBLANES + ts, :] = u_ref[0]
    xc = cb_ref[...] + cw_ref[CONV_W - 1:CONV_W, :] * ext_ref[SUBLANES:SUBLANES + ts, :]
    for k in range(CONV_W - 1):
        off = SUBLANES - (CONV_W - 1) + k
        xc = xc + cw_ref[k:k + 1, :] * ext_ref[off:off + ts, :]
    ext_ref[0:SUBLANES, :] = ext_ref[ts:ts + SUBLANES, :]

    nl = -lam_ref[...]
    sp = jnp.maximum(nl, 0.0) + jnp.log1p(jnp.exp(-jnp.abs(nl)))
    for n in range(RNN_BLOCKS):
        cs = slice(n * RNN_BW, (n + 1) * RNN_BW)
        xb = xc[:, cs]
        g = _bdot(xb, wax_ref[n])
        r = jax.nn.sigmoid(g[:, :RNN_BW] + ba_ref[:, cs])
        i = jax.nn.sigmoid(g[:, RNN_BW:] + bx_ref[:, cs])
        log_a = (-LRU_C) * r * sp[:, cs]
        a_ref[:, cs] = jnp.exp(log_a)
        th = jnp.tanh(log_a)
        b_ref[:, cs] = jnp.sqrt(-2.0 * th / (1.0 - th)) * (i * xb)

    row = lax.broadcasted_iota(jnp.int32, (SUBLANES, d), 0)

    def group(gi, hprev):
        r0 = pl.multiple_of(gi * SUBLANES, SUBLANES)
        a = a_ref[pl.ds(r0, SUBLANES), :]
        b = b_ref[pl.ds(r0, SUBLANES), :]
        for sh in (1, 2, 4):
            keep = row >= sh
            a_s = jnp.where(keep, pltpu.roll(a, sh, 0), 1.0)
            b_s = jnp.where(keep, pltpu.roll(b, sh, 0), 0.0)
            b = a * b_s + b
            a = a * a_s
        h = b + a * hprev
        b_ref[pl.ds(r0, SUBLANES), :] = h
        return jnp.broadcast_to(h[SUBLANES - 1:SUBLANES, :], (SUBLANES, d))

    carry_ref[...] = lax.fori_loop(0, ts // SUBLANES, group, carry_ref[...])
    y_ref[0] = (b_ref[...] * _gelu_tanh(ug_ref[0])).astype(BF16)


def _rglru(u_rnn, u_gate, conv_w, conv_b, w_rg_a, b_rg_a, w_rg_x, b_rg_x, lam):
    bsz, s, d = u_rnn.shape
    ts = SCAN_TILE
    wax = jnp.concatenate([w_rg_a, w_rg_x], axis=-1).astype(BF16)
    vec = lambda v: v.reshape(1, d)
    c2 = lambda shape: pl.BlockSpec(shape, lambda b, i: (0,) * len(shape))
    row = pl.BlockSpec((1, ts, d), lambda b, i: (b, i, 0))
    return pl.pallas_call(
        _rglru_kernel,
        grid=(bsz, s // ts),
        in_specs=[row, row, c2((CONV_W, d)), c2((1, d)), c2(wax.shape), c2((1, d)), c2((1, d)),
                  c2((1, d))],
        out_specs=row,
        out_shape=jax.ShapeDtypeStruct((bsz, s, d), BF16),
        scratch_shapes=[pltpu.VMEM((ts + SUBLANES, d), F32), pltpu.VMEM((ts, d), F32),
                        pltpu.VMEM((ts, d), F32), pltpu.VMEM((SUBLANES, d), F32)],
        compiler_params=_cparams("arbitrary", "arbitrary"),
        name="rglru",
    )(u_rnn, u_gate, conv_w, vec(conv_b), wax, vec(b_rg_a), vec(b_rg_x), vec(lam))


def _t5_bucket_np(dist):
    max_exact = N_BUCKETS // 2
    dd = np.maximum(dist, 0)
    df = np.maximum(dd, 1).astype(np.float32)
    large = max_exact + (np.log(df / np.float32(max_exact)) / np.float32(math.log(MAX_DIST / max_exact))
                         * np.float32(N_BUCKETS - max_exact)).astype(np.int32)
    large = np.minimum(large, N_BUCKETS - 1)
    return np.where(dd < max_exact, dd, large)


def _near_bucket_ids():
    r = np.arange(KEY_GROUP)[None, :, None]
    c = np.arange(Q_TILE)[None, None, :]
    o = np.arange(2)[:, None, None]
    return _t5_bucket_np(c - r - (o - 1) * KEY_GROUP).astype(np.int32)


def _attn_kernel(rb_ref, bkt_ref, qt_ref, qit_ref, wit_ref, k_ref, vt_ref, ki_ref, y_ref,
                 s_ref, tab_ref, acc_ref, m_ref, lg_ref):
    tq = Q_TILE
    ks = KEY_SUB
    kg = KEY_GROUP
    jq = pl.program_id(1)
    t0 = jq * tq
    ngrp = jq + 1
    lane_t = t0 + lax.broadcasted_iota(jnp.int32, (1, tq), 1)

    @pl.when((pl.program_id(0) == 0) & (jq == 0))
    def _():
        for o in range(2):
            for h in range(N_HEADS):
                tab_ref[h, o] = jnp.zeros((kg, tq), F32)

            def fill(b, c):
                hit = bkt_ref[o] == b
                for h in range(N_HEADS):
                    val = (rb_ref[b, h] - rb_ref[N_BUCKETS - 1, h]) * LOG2E
                    tab_ref[h, o] = jnp.where(hit, val, tab_ref[h, o])
                return c

            lax.fori_loop(0, N_BUCKETS - 1, fill, 0)

    wi = wit_ref[0] * (IDX_HEADS ** -0.5 * IDX_DIM ** -0.5)

    def score_sub(i, mnmx, masked):
        r0 = pl.multiple_of(i * ks, ks)
        kic = ki_ref[0, pl.ds(r0, ks), :]
        acc = jnp.zeros((ks, tq), F32)
        for h in range(IDX_HEADS):
            dts = jnp.dot(kic, qit_ref[0, h * IDX_DIM:(h + 1) * IDX_DIM, :],
                          preferred_element_type=F32)
            acc = acc + jnp.maximum(dts, 0.0) * wi[h:h + 1, :]
        lo_src = acc
        if masked:
            key_s = r0 + lax.broadcasted_iota(jnp.int32, (ks, tq), 0)
            causal = key_s <= lane_t
            acc = jnp.where(causal, acc, NEG_INF)
            lo_src = jnp.where(causal, acc, jnp.inf)
        s_ref[pl.ds(r0, ks), :] = acc
        mn, mx = mnmx
        mn = jnp.minimum(mn, jnp.min(lo_src.reshape(ks // SUBLANES, SUBLANES, tq), axis=0))
        mx = jnp.maximum(mx, jnp.max(acc.reshape(ks // SUBLANES, SUBLANES, tq), axis=0))
        return mn, mx

    mnmx = (jnp.full((SUBLANES, tq), jnp.inf, F32), jnp.full((SUBLANES, tq), NEG_INF, F32))
    mnmx = lax.fori_loop(0, ngrp - 1,
                         lambda g, c: score_sub(2 * g + 1, score_sub(2 * g, c, False), False), mnmx)
    mnmx = score_sub(2 * ngrp - 2, mnmx, True)
    mn8, mx8 = score_sub(2 * ngrp - 1, mnmx, True)
    smin = jnp.min(mn8, axis=0, keepdims=True)
    smax = jnp.max(mx8, axis=0, keepdims=True)

    n_causal = (lane_t + 1).astype(F32)
    k_eff = jnp.minimum(n_causal, float(TOPK_MAX))

    def count_rows(pred):
        part = 2 * SUBLANES

        def block(start, rows, c):
            r0 = pl.multiple_of(start, rows)
            key_s = (r0 + lax.broadcasted_iota(jnp.int32, (rows, tq), 0)).astype(F32)
            ind = jnp.where(pred(s_ref[pl.ds(r0, rows), :], key_s), 1.0, 0.0)
            return c + jnp.sum(ind.reshape(rows // part, part, tq), axis=0)

        c = lax.fori_loop(0, ngrp // 2, lambda g, c: block(g * (2 * kg), 2 * kg, c),
                          jnp.zeros((part, tq), F32))
        c = lax.cond(ngrp % 2 == 1, lambda c: block((ngrp - 1) * kg, kg, c), lambda c: c, c)
        return jnp.sum(c, axis=0, keepdims=True)

    def bis_step(st):
        it, lo, hi, c_lo, done = st
        first = (jnp.zeros((1, tq), F32) + jnp.where(it == 0, 1.0, 0.0)) > 0.0
        probe = jnp.where(first, smax, 0.5 * lo + 0.5 * hi)
        collapsed = ~first & ((probe <= lo) | (probe >= hi))
        cnt = count_rows(lambda blk, _: blk >= probe)
        ge = cnt >= k_eff
        upd = (done == 0.0) & ~collapsed
        lo_n = jnp.where(upd & ge, probe, lo)
        c_lo_n = jnp.where(upd & ge, cnt, c_lo)
        hi_n = jnp.where(upd & ~ge, probe, hi)
        fin = collapsed | (cnt == k_eff) | (first & ge)
        done_n = jnp.where(fin, 1.0, done)
        return it + 1, lo_n, hi_n, c_lo_n, done_n

    def n_open(st):
        return jnp.sum(1.0 - st[4])

    done0 = jnp.where(n_causal <= k_eff, 1.0, 0.0)
    st = lax.fori_loop(0, UNCHECKED_BISECT_STEPS, lambda _, s: bis_step(s),
                       (jnp.int32(0), smin, smax, n_causal, done0))
    st = lax.while_loop(lambda s: s[5] > 0.0,
                        lambda s: (lambda nxt: nxt + (n_open(nxt),))(bis_step(s[:5])),
                        st + (n_open(st),))
    thr, c_thr = st[1], st[3]

    tie_all = jnp.zeros((1, tq), F32) + (t0 + tq).astype(F32)

    def tie_limit():
        need = k_eff - count_rows(lambda blk, _: blk > thr)

        def body(_, st):
            lo, hi = st
            mid = jnp.floor(0.5 * (lo + hi))
            ok = count_rows(lambda blk, key_s: (blk == thr) & (key_s < mid)) >= need
            return jnp.where(ok, lo, mid), jnp.where(ok, mid, hi)

        n_steps = int(math.ceil(math.log2(s_ref.shape[0]))) + 1
        return lax.fori_loop(0, n_steps, body, (jnp.zeros((1, tq), F32), tie_all))[1]

    excess = jnp.sum(jnp.where(c_thr > k_eff, 1.0, 0.0))
    tie_lim = lax.cond(excess > 0.0, tie_limit, lambda: tie_all)

    def mask_body(g, c):
        r0 = pl.multiple_of(g * kg, kg)
        blk = s_ref[pl.ds(r0, kg), :]
        key_s = (r0 + lax.broadcasted_iota(jnp.int32, (kg, tq), 0)).astype(F32)
        sel = (blk > thr) | ((blk == thr) & (key_s < tie_lim))
        s_ref[pl.ds(r0, kg), :] = jnp.where(sel, 0.0, NEG_INF)
        return c

    lax.fori_loop(0, ngrp, mask_body, 0)

    m_ref[...] = jnp.full_like(m_ref, NEG_INF)
    acc_ref[...] = jnp.zeros_like(acc_ref)
    ones_rows = jnp.ones((DEN_ROWS, kg), BF16)

    def attend(g, near):
        r0 = pl.multiple_of(g * kg, kg)
        msk = s_ref[pl.ds(r0, kg), :]
        col_max = []
        for h in range(N_HEADS):
            hs = slice(h * HEAD_DIM, (h + 1) * HEAD_DIM)
            kh = k_ref[0, pl.ds(r0, kg), hs]
            lg = jnp.dot(kh, qt_ref[0, hs, :], preferred_element_type=F32) + msk
            if near is not None:
                lg = lg + tab_ref[h, near]
            lg_ref[h] = lg
            col_max.append(jnp.max(lg, axis=0, keepdims=True))
        for h in range(N_HEADS):
            hs = slice(h * HEAD_DIM, (h + 1) * HEAD_DIM)
            m_old = m_ref[h:h + 1, :]
            m_new = jnp.maximum(m_old, col_max[h])
            m_safe = jnp.where(m_new == NEG_INF, 0.0, m_new)
            p = jnp.exp2(lg_ref[h] - m_safe).astype(BF16)
            alpha = jnp.exp2(m_old - m_safe)
            m_ref[h:h + 1, :] = m_new
            vh = jnp.concatenate([vt_ref[0, hs, pl.ds(r0, kg)], ones_rows], axis=0)
            acc_ref[h] = alpha * acc_ref[h] + jnp.dot(vh, p, preferred_element_type=F32)

    n_far = jnp.maximum(ngrp - 2, 0)

    def _attend_pair(gp, c):
        attend(2 * gp, None)
        attend(2 * gp + 1, None)
        return c

    lax.fori_loop(0, n_far // 2, _attend_pair, 0)

    @pl.when(n_far % 2 == 1)
    def _():
        attend(n_far - 1, None)

    @pl.when(jq > 0)
    def _():
        attend(ngrp - 2, 0)

    attend(ngrp - 1, 1)

    for h in range(N_HEADS):
        o = acc_ref[h, 0:HEAD_DIM, :] / acc_ref[h, HEAD_DIM:HEAD_DIM + 1, :]
        y_ref[0, :, h * HEAD_DIM:(h + 1) * HEAD_DIM] = o.T.astype(BF16)


def _attention(qt, qit, wit, k, vt, ki, rel_bias):
    bsz, d, s = qt.shape
    tq = Q_TILE
    bkt = jnp.asarray(_near_bucket_ids())
    once = lambda shape: pl.BlockSpec(shape, lambda b, j: (b,) + (0,) * (len(shape) - 1),
                                      pipeline_mode=pl.Buffered(1))
    col = lambda r: pl.BlockSpec((1, r, tq), lambda b, j: (b, 0, j))
    return pl.pallas_call(
        _attn_kernel,
        grid=(bsz, s // tq),
        in_specs=[pl.BlockSpec(memory_space=pltpu.SMEM),
                  pl.BlockSpec(bkt.shape, lambda b, j: (0, 0, 0)),
                  col(d), col(d), col(IDX_HEADS),
                  once((1, s, d)), once((1, d, s)), once((1, s, IDX_DIM))],
        out_specs=pl.BlockSpec((1, tq, d), lambda b, j: (b, j, 0)),
        out_shape=jax.ShapeDtypeStruct((bsz, s, d), BF16),
        scratch_shapes=[pltpu.VMEM((s, tq), F32),
                        pltpu.VMEM((N_HEADS, 2, KEY_GROUP, tq), F32),
                        pltpu.VMEM((N_HEADS, HEAD_DIM + DEN_ROWS, tq), F32),
                        pltpu.VMEM((N_HEADS, tq), F32),
                        pltpu.VMEM((N_HEADS, KEY_GROUP, tq), F32)],
        compiler_params=_cparams("arbitrary", "arbitrary"),
        name="attn",
    )(rel_bias, bkt, qt, qit, wit, k, vt, ki)


def _pack_bf16_pair(x):
    c = x.shape[1] // 2
    lo = lax.bitcast_convert_type(x[:, :c].astype(BF16).astype(F32), jnp.uint32) >> 16
    hi = lax.bitcast_convert_type(x[:, c:].astype(BF16).astype(F32), jnp.uint32) & jnp.uint32(0xFFFF0000)
    return lo | hi


def _unpack_bf16_pair(p):
    a = lax.bitcast_convert_type(p << 16, F32)
    b = lax.bitcast_convert_type(p & jnp.uint32(0xFFFF0000), F32)
    return a, b


def _unpack_rows_bf16(p):
    a, b = _unpack_bf16_pair(p)
    return jnp.concatenate([a, b], axis=1).astype(BF16)


def _merge_kernel(yr_ref, ya_ref, glr_ref, gla_ref, x_ref, mod_ref, g_ref,
                  wr_ref, wa_ref, wo_ref, x1_ref, h2p_ref):
    merged = (jax.nn.sigmoid(glr_ref[0]) * jnp.dot(yr_ref[0], wr_ref[...], preferred_element_type=F32)
              + jax.nn.sigmoid(gla_ref[0]) * jnp.dot(ya_ref[0], wa_ref[...], preferred_element_type=F32))
    y = _bdot(merged, wo_ref[...])
    x1 = x_ref[0] + mod_ref[0, 2:3, :] * _rms(y, g_ref[1:2, :])
    x1_ref[0] = x1
    h2 = _rms(x1, g_ref[2:3, :]) * (1.0 + mod_ref[0, 4:5, :]) + mod_ref[0, 3:4, :]
    h2p_ref[0] = _pack_bf16_pair(h2)


def _merge(y_rnn, y_attn, gl_rnn, gl_attn, x, mod3, gains, w_br_rnn, w_br_attn, w_out):
    bsz, s, d = x.shape
    tm = ROW_TILE
    row = pl.BlockSpec((1, tm, d), lambda b, i: (b, i, 0))
    half = pl.BlockSpec((1, tm, d // 2), lambda b, i: (b, i, 0))
    c2 = lambda shape: pl.BlockSpec(shape, lambda b, i: (0,) * len(shape))
    return pl.pallas_call(
        _merge_kernel,
        grid=(bsz, s // tm),
        in_specs=[row, row, row, row, row,
                  pl.BlockSpec((1, N_MOD, d), lambda b, i: (b, 0, 0)),
                  c2(gains.shape), c2((d, d)), c2((d, d)), c2((d, d))],
        out_specs=(row, half),
        out_shape=(jax.ShapeDtypeStruct((bsz, s, d), F32),
                   jax.ShapeDtypeStruct((bsz, s, d // 2), jnp.uint32)),
        compiler_params=_cparams("arbitrary", "arbitrary"),
        name="merge",
    )(y_rnn, y_attn, gl_rnn, gl_attn, x, mod3, gains,
      w_br_rnn.astype(BF16), w_br_attn.astype(BF16), w_out.astype(BF16))


def _router_kernel(hp_ref, wr_ref, rb_ref, dest_ref, wk_ref, seg_ref):
    t = hp_ref.shape[0]
    gsz = N_EXPERTS // N_GROUPS
    h = _unpack_rows_bf16(hp_ref[...])
    s = jax.nn.sigmoid(_bdot_nt(wr_ref[...], h))
    s_sel = s + rb_ref[...]
    g3 = s_sel.reshape(N_GROUPS, gsz, t)
    e_in_g = lax.broadcasted_iota(jnp.int32, (N_GROUPS, gsz, t), 1)
    top1 = jnp.max(g3, axis=1, keepdims=True)
    first = jnp.min(jnp.where(g3 == top1, e_in_g, gsz), axis=1, keepdims=True)
    top2 = jnp.max(jnp.where(e_in_g == first, NEG_INF, g3), axis=1, keepdims=True)
    gscore = jnp.broadcast_to(top1 + top2, (N_GROUPS, gsz, t))
    gi = lax.broadcasted_iota(jnp.int32, (N_GROUPS, gsz, t), 0)
    gmask = jnp.zeros((N_GROUPS, gsz, t), F32)
    for _ in range(TOPK_GROUPS):
        mx = jnp.max(gscore, axis=0, keepdims=True)
        pick = jnp.min(jnp.where(gscore == mx, gi, N_GROUPS), axis=0, keepdims=True)
        hit = gi == pick
        gmask = jnp.where(hit, 1.0, gmask)
        gscore = jnp.where(hit, NEG_INF, gscore)
    cand = jnp.where(gmask.reshape(N_EXPERTS, t) > 0.0, s_sel, NEG_INF)
    ei = lax.broadcasted_iota(jnp.int32, (N_EXPERTS, t), 0)
    sel = jnp.zeros((N_EXPERTS, t), F32)
    picks = []
    for _ in range(TOP_K):
        mx = jnp.max(cand, axis=0, keepdims=True)
        pick = jnp.min(jnp.where(cand == mx, ei, N_EXPERTS), axis=0, keepdims=True)
        hit = ei == pick
        sel = jnp.where(hit, 1.0, sel)
        cand = jnp.where(hit, NEG_INF, cand)
        picks.append(pick)
    w = s * sel
    w = w / jnp.sum(w, axis=0, keepdims=True) * ROUTED_SCALE

    selb = sel.astype(BF16)
    tok_r = lax.broadcasted_iota(jnp.int32, (t, t), 0)
    tok_c = lax.broadcasted_iota(jnp.int32, (t, t), 1)
    rank = jnp.dot(selb, jnp.where(tok_r < tok_c, 1.0, 0.0).astype(BF16), preferred_element_type=F32)
    e_r = lax.broadcasted_iota(jnp.int32, (N_EXPERTS, N_EXPERTS), 0)
    e_c = lax.broadcasted_iota(jnp.int32, (N_EXPERTS, N_EXPERTS), 1)
    cnt_col = jnp.sum(sel, axis=1, keepdims=True)
    pad_col = jnp.floor((cnt_col + (SUBLANES - 1.0)) * (1.0 / SUBLANES))
    off_col = SUBLANES * jnp.dot(jnp.where(e_c < e_r, 1.0, 0.0).astype(BF16),
                                 jnp.broadcast_to(pad_col, (N_EXPERTS, LANES)).astype(BF16),
                                 preferred_element_type=F32)[:, 0:1]
    slot = rank + off_col
    dest_rows, w_rows = [], []
    for k in range(TOP_K):
        hit = ei == picks[k]
        dest_rows.append(jnp.sum(jnp.where(hit, slot, 0.0), axis=0, keepdims=True))
        w_rows.append(jnp.sum(jnp.where(hit, w, 0.0), axis=0, keepdims=True))
    dest_ref[0] = (jnp.concatenate(dest_rows, axis=0) * SLAB).astype(jnp.int32)
    wk_ref[...] = jnp.concatenate(w_rows, axis=0)
    cnt_row = _bdot_nt(jnp.ones((SUBLANES, t), BF16), selb)
    pad_row = jnp.floor((cnt_row + (SUBLANES - 1.0)) * (1.0 / SUBLANES))
    off_row = SUBLANES * jnp.dot(pad_row.astype(BF16), jnp.where(e_r < e_c, 1.0, 0.0).astype(BF16),
                                 preferred_element_type=F32)
    seg_ref[0] = jnp.concatenate([off_row[0:1], cnt_row[0:1]], axis=1).astype(jnp.int32)


def _router(h2p, w_router, router_bias):
    n, dh = h2p.shape
    t = MOE_TILE
    nt = n // t
    return pl.pallas_call(
        _router_kernel,
        grid=(nt,),
        in_specs=[pl.BlockSpec((t, dh), lambda i: (i, 0)),
                  pl.BlockSpec((N_EXPERTS, 2 * dh), lambda i: (0, 0)),
                  pl.BlockSpec((N_EXPERTS, 1), lambda i: (0, 0))],
        out_specs=(pl.BlockSpec((1, TOP_K, t), lambda i: (i, 0, 0)),
                   pl.BlockSpec((TOP_K, t), lambda i: (0, i)),
                   pl.BlockSpec((1, 1, 2 * N_EXPERTS), lambda i: (i, 0, 0))),
        out_shape=(jax.ShapeDtypeStruct((nt, TOP_K, t), jnp.int32),
                   jax.ShapeDtypeStruct((TOP_K, n), F32),
                   jax.ShapeDtypeStruct((nt, 1, 2 * N_EXPERTS), jnp.int32)),
        compiler_params=_cparams("arbitrary"),
        name="router",
    )(h2p, w_router.T.astype(BF16), router_bias.reshape(N_EXPERTS, 1))


def _swiglu(x, wgu, wd):
    gu = jnp.dot(x, wgu, preferred_element_type=F32)
    g = gu[:, :D_EXPERT]
    a = (g * jax.nn.sigmoid(g)) * gu[:, D_EXPERT:]
    return jnp.dot(a.astype(BF16), wd, preferred_element_type=F32)


def _expert_block(tile, step, n_blocks):
    sc = jnp.minimum(step, n_blocks - 1)
    return jnp.where(tile % 2 == 0, sc, n_blocks - 1 - sc)


def _moe_kernel(seg_ref, hp_ref, dest_ref, wk_ref, wgu_ref, wd_ref, wsgu_ref, wsd_ref, x1_ref,
                mod_ref, g_ref, o_ref, xs_ref, acca_ref, accb_ref, act_ref, dest_smem, w_smem, sem):
    i = pl.program_id(0)
    s = pl.program_id(1)
    n_s = pl.num_programs(1)
    t = MOE_TILE

    def slab(ref, first):
        return ref.at[pl.ds(pl.multiple_of(first, SLAB), SLAB), :]

    def load_rows(ref, r0, n):
        return jnp.concatenate([ref[pl.ds(SLAB * r0 + c, n, stride=SLAB), :] for c in range(SLAB)],
                               axis=1)

    @pl.when((i == 0) & (s == 0))
    def _():
        xs_ref[...] = jnp.zeros_like(xs_ref)

    @pl.when(s == 0)
    def _():
        copies = (pltpu.make_async_copy(dest_ref, dest_smem, sem.at[0]),
                  pltpu.make_async_copy(wk_ref, w_smem, sem.at[1]))
        for cp in copies:
            cp.start()
        for cp in copies:
            cp.wait()

        def body(nb, c):
            n0 = nb * MOVE_TOKENS
            base = n0 * TOP_K
            for j in range(MOVE_TOKENS):
                row = slab(hp_ref, (n0 + j) * SLAB)[...]
                for k in range(TOP_K):
                    slab(xs_ref, dest_smem[base + (j * TOP_K + k)])[...] = row
            return c

        lax.fori_loop(0, t // MOVE_TOKENS, body, 0)

    def put_rows(r0, end, y, old):
        new = _pack_bf16_pair(y)
        rows = r0 + lax.broadcasted_iota(jnp.int32, (FFN_ROWS, LANES), 0)
        for c in range(SLAB):
            cs = slice(c * LANES, (c + 1) * LANES)
            xs_ref[pl.ds(SLAB * r0 + c, FFN_ROWS, stride=SLAB), :] = jnp.where(rows < end, new[:, cs],
                                                                               old[:, cs])

    @pl.when(s < n_s - 1)
    def _():
        offs, ends = [], []
        for eb in range(EXPERTS_PER_STEP):
            e = _expert_block(i, s, n_s - 1) * EXPERTS_PER_STEP + eb
            offs.append(pl.multiple_of(seg_ref[i, e], SUBLANES))
            ends.append(offs[eb] + seg_ref[i, N_EXPERTS + e])
        olds = [load_rows(xs_ref, offs[eb], FFN_ROWS) for eb in range(EXPERTS_PER_STEP)]
        for eb in range(EXPERTS_PER_STEP):
            gu = jnp.dot(_unpack_rows_bf16(olds[eb]), wgu_ref[eb], preferred_element_type=F32)
            g = gu[:, :D_EXPERT]
            act_ref[eb] = ((g * jax.nn.sigmoid(g)) * gu[:, D_EXPERT:]).astype(BF16)
        ys = [jnp.dot(act_ref[eb], wd_ref[eb], preferred_element_type=F32)
              for eb in range(EXPERTS_PER_STEP)]
        for eb in range(EXPERTS_PER_STEP):
            put_rows(offs[eb], ends[eb], ys[eb], olds[eb])
        for eb in range(EXPERTS_PER_STEP):
            def chunk(c, carry, eb=eb):
                r0 = pl.multiple_of(offs[eb] + c * FFN_ROWS, SUBLANES)
                old = load_rows(xs_ref, r0, FFN_ROWS)
                put_rows(r0, ends[eb], _swiglu(_unpack_rows_bf16(old), wgu_ref[eb], wd_ref[eb]), old)
                return carry

            n_chunks = lax.div(ends[eb] - offs[eb] + (FFN_ROWS - 1), FFN_ROWS)
            lax.fori_loop(1, n_chunks, chunk, 0)

    @pl.when(s == n_s - 1)
    def _():
        o_ref[...] = _swiglu(_unpack_rows_bf16(load_rows(hp_ref, 0, t)), wsgu_ref[...], wsd_ref[...])

        def body(nb, c):
            n0 = pl.multiple_of(nb * SUBLANES, SUBLANES)
            base = n0 * TOP_K
            for j in range(SUBLANES):
                acc_a = jnp.zeros((SLAB, LANES), F32)
                acc_b = jnp.zeros((SLAB, LANES), F32)
                for k in range(TOP_K):
                    idx = base + (j * TOP_K + k)
                    a, b = _unpack_bf16_pair(slab(xs_ref, dest_smem[idx])[...])
                    w = w_smem[idx]
                    acc_a = acc_a + a * w
                    acc_b = acc_b + b * w
                acca_ref[j * SLAB:(j + 1) * SLAB, :] = acc_a
                accb_ref[j * SLAB:(j + 1) * SLAB, :] = acc_b
            lo = [acca_ref[pl.ds(c, SUBLANES, stride=SLAB), :] for c in range(SLAB)]
            hi = [accb_ref[pl.ds(c, SUBLANES, stride=SLAB), :] for c in range(SLAB)]
            o_ref[pl.ds(n0, SUBLANES), :] += jnp.concatenate(lo + hi, axis=1)
            return c

        lax.fori_loop(0, t // SUBLANES, body, 0)
        o_ref[...] = x1_ref[...] + mod_ref[0, 5:6, :] * _rms(o_ref[...], g_ref[...])


def _moe(h2p, dest, wk, seg, wgu, wd, wsgu, wsd, x1, mod3, g3, seq):
    n, dh = h2p.shape
    d = 2 * dh
    t = MOE_TILE
    n_steps = N_EXPERTS // EXPERTS_PER_STEP + 1
    tiles_per_seq = seq // t
    tok = lambda w: pl.BlockSpec((t, w), lambda i, s: (i, 0))
    table = pl.BlockSpec((TOP_K * t,), lambda i, s: (i,))
    wblk = lambda r, c: pl.BlockSpec((EXPERTS_PER_STEP, r, c),
                                     lambda i, s: (_expert_block(i, s, n_steps - 1), 0, 0))
    c2 = lambda shape: pl.BlockSpec(shape, lambda i, s: (0,) * len(shape))
    xs_rows = TOP_K * t + N_EXPERTS * SUBLANES + FFN_ROWS
    return pl.pallas_call(
        _moe_kernel,
        grid=(n // t, n_steps),
        in_specs=[pl.BlockSpec(memory_space=pltpu.SMEM),
                  pl.BlockSpec((SLAB * t, LANES), lambda i, s: (i, 0)),
                  table, table,
                  wblk(d, 2 * D_EXPERT), wblk(D_EXPERT, d),
                  c2((d, 2 * D_EXPERT)), c2((D_EXPERT, d)),
                  tok(d),
                  pl.BlockSpec((1, N_MOD, d), lambda i, s: (i // tiles_per_seq, 0, 0)),
                  c2((1, d))],
        out_specs=tok(d),
        out_shape=jax.ShapeDtypeStruct((n, d), F32),
        scratch_shapes=[pltpu.VMEM((SLAB * xs_rows, LANES), jnp.uint32),
                        pltpu.VMEM((SLAB * SUBLANES, LANES), F32),
                        pltpu.VMEM((SLAB * SUBLANES, LANES), F32),
                        pltpu.VMEM((EXPERTS_PER_STEP, FFN_ROWS, D_EXPERT), BF16),
                        pltpu.SMEM((TOP_K * t,), jnp.int32),
                        pltpu.SMEM((TOP_K * t,), F32),
                        pltpu.SemaphoreType.DMA((2,))],
        compiler_params=_cparams("arbitrary", "arbitrary"),
        name="moe",
    )(seg, h2p.reshape(n * SLAB, LANES), dest, wk, wgu, wd, wsgu, wsd, x1, mod3, g3)


def kernel(x, c, w_ada, b_ada, norm_gain, w_in, conv_w, conv_b, w_rg_a, b_rg_a, w_rg_x, b_rg_x,
           lru_lambda, w_br_rnn, w_br_attn, w_out, rel_bias, w_router, router_bias,
           w_exp_gate, w_exp_up, w_exp_down, w_sh_gate, w_sh_up, w_sh_down):
    bsz, s, d = x.shape
    depth = w_ada.shape[0]
    for l in range(depth):
        mod = _ada(c, w_ada[l], b_ada[l])
        mod3 = mod.reshape(bsz, N_MOD, d)
        gains = norm_gain[l]
        (u_rnn, u_gate, k, gl_rnn, gl_attn, ki, qt, vt, qit, wit) = _in_proj(x, mod3, gains[0:1], w_in[l])
        y_rnn = _rglru(u_rnn, u_gate, conv_w[l], conv_b[l], w_rg_a[l], b_rg_a[l], w_rg_x[l],
                       b_rg_x[l], lru_lambda[l])
        y_attn = _attention(qt, qit, wit, k, vt, ki, rel_bias)
        x1, h2p = _merge(y_rnn, y_attn, gl_rnn, gl_attn, x, mod3, gains, w_br_rnn[l], w_br_attn[l],
                         w_out[l])
        h2p = h2p.reshape(bsz * s, d // 2)
        dest, wk, seg = _router(h2p, w_router[l], router_bias[l])
        wgu = jnp.concatenate([w_exp_gate[l], w_exp_up[l]], axis=-1).astype(BF16)
        wsgu = jnp.concatenate([w_sh_gate[l], w_sh_up[l]], axis=-1).astype(BF16)
        dest = jnp.transpose(dest, (0, 2, 1)).reshape(-1)
        wk = jnp.transpose(wk).reshape(-1)
        x = _moe(h2p, dest, wk, seg.reshape(-1, 2 * N_EXPERTS), wgu, w_exp_down[l].astype(BF16),
                 wsgu, w_sh_down[l].astype(BF16), x1.reshape(bsz * s, d), mod3, gains[3:4],
                 s).reshape(bsz, s, d)
    return x
```

```python
import math

import jax
import jax.numpy as jnp
import numpy as np
from jax import lax
from jax.experimental import pallas as pl
from jax.experimental.pallas import tpu as pltpu

F32 = jnp.float32
BF16 = jnp.bfloat16

D_MODEL = 1024
RNN_BLOCKS = 8
RNN_BW = D_MODEL // RNN_BLOCKS
CONV_W = 4
LRU_C = 8.0
N_HEADS = 8
HEAD_DIM = 128
IDX_HEADS = 16
IDX_DIM = 64
TOPK_MAX = 256
N_BUCKETS = 32
MAX_DIST = 128
N_EXPERTS = 64
TOP_K = 8
N_GROUPS = 8
TOPK_GROUPS = 4
D_EXPERT = 256
ROUTED_SCALE = 2.5
N_MOD = 6
EPS = 1e-6

LANES = 128
SUBLANES = 8
VMEM_LIMIT_BYTES = 58 * 1024 * 1024

ROW_TILE = 512
UNCHECKED_BISECT_STEPS = 16
SCAN_TILE = 512
Q_TILE = 256
KEY_SUB = 128
KEY_GROUP = 256
DEN_ROWS = 16
MOE_TILE = 1024
EXPERTS_PER_STEP = 4
FFN_ROWS = 160
MOVE_TOKENS = 4
SLAB = 4

NEG_INF = float("-inf")
LOG2E = math.log2(math.e)


def _cparams(*sem):
    return pltpu.CompilerParams(dimension_semantics=sem, vmem_limit_bytes=VMEM_LIMIT_BYTES)


def _bdot(a, b):
    return jnp.dot(a.astype(BF16), b.astype(BF16), preferred_element_type=F32)


def _bdot_nt(a, b):
    return lax.dot_general(a.astype(BF16), b.astype(BF16), (((1,), (1,)), ((), ())),
                           preferred_element_type=F32)


def _rms(x, g):
    ms = jnp.mean(x * x, axis=-1, keepdims=True)
    return x * lax.rsqrt(ms + EPS) * g


def _ada_kernel(c_ref, w_ref, b_ref, o_ref):
    c = c_ref[...]
    cond = c * jax.nn.sigmoid(c)
    o_ref[...] = _bdot(cond, w_ref[...]) + b_ref[...]


def _ada(c, w_ada, b_ada):
    bsz, d = c.shape
    n = w_ada.shape[1]
    tn = 1024
    return pl.pallas_call(
        _ada_kernel,
        grid=(n // tn,),
        in_specs=[pl.BlockSpec((bsz, d), lambda j: (0, 0)),
                  pl.BlockSpec((d, tn), lambda j: (0, j)),
                  pl.BlockSpec((1, tn), lambda j: (0, j))],
        out_specs=pl.BlockSpec((bsz, tn), lambda j: (0, j)),
        out_shape=jax.ShapeDtypeStruct((bsz, n), F32),
        compiler_params=_cparams("arbitrary"),
        name="ada",
    )(c, w_ada, b_ada.reshape(1, n))


def _inproj_kernel(x_ref, mod_ref, g_ref, wn_ref, wki_ref, wt_ref, wwi_ref,
                   urnn_ref, ugate_ref, k_ref, glr_ref, gla_ref, ki_ref,
                   qt_ref, vt_ref, qit_ref, wit_ref):
    d = D_MODEL
    x = x_ref[0]
    h = _rms(x, g_ref[...]) * (1.0 + mod_ref[0, 1:2, :]) + mod_ref[0, 0:1, :]
    hb = h.astype(BF16)
    urnn_ref[0] = jnp.dot(hb, wn_ref[:, 0 * d:1 * d], preferred_element_type=F32)
    ugate_ref[0] = jnp.dot(hb, wn_ref[:, 1 * d:2 * d], preferred_element_type=F32)
    k_ref[0] = jnp.dot(hb, wn_ref[:, 2 * d:3 * d], preferred_element_type=F32).astype(BF16)
    glr_ref[0] = jnp.dot(hb, wn_ref[:, 3 * d:4 * d], preferred_element_type=F32)
    gla_ref[0] = jnp.dot(hb, wn_ref[:, 4 * d:5 * d], preferred_element_type=F32)
    ki_ref[0] = jnp.dot(hb, wki_ref[...], preferred_element_type=F32).astype(BF16)
    nt = (((1,), (1,)), ((), ()))
    qt_ref[0] = (lax.dot_general(wt_ref[0 * d:1 * d, :], hb, nt, preferred_element_type=F32)
                 * (HEAD_DIM ** -0.5 * LOG2E)).astype(BF16)
    vt_ref[0] = lax.dot_general(wt_ref[1 * d:2 * d, :], hb, nt, preferred_element_type=F32).astype(BF16)
    qit_ref[0] = lax.dot_general(wt_ref[2 * d:3 * d, :], hb, nt, preferred_element_type=F32).astype(BF16)
    wit_ref[0] = lax.dot_general(wwi_ref[...], hb, nt, preferred_element_type=F32)


def _in_proj(x, mod3, g0, w_in):
    bsz, s, d = x.shape
    tm = ROW_TILE
    offs = np.cumsum([0, d, d, d, d, d, IDX_HEADS * IDX_DIM, IDX_DIM, IDX_HEADS, d, d])
    seg = lambda i: w_in[:, int(offs[i]):int(offs[i + 1])]
    wn = jnp.concatenate([seg(0), seg(1), seg(3), seg(8), seg(9)], axis=1).astype(BF16)
    wki = seg(6).astype(BF16)
    wt = jnp.concatenate([seg(2), seg(4), seg(5)], axis=1).T.astype(BF16)
    wwi = seg(7).T.astype(BF16)
    const = lambda shape: pl.BlockSpec(shape, lambda b, i: (0,) * len(shape),
                                       pipeline_mode=pl.Buffered(1))
    row = lambda w: pl.BlockSpec((1, tm, w), lambda b, i: (b, i, 0))
    col = lambda r: pl.BlockSpec((1, r, tm), lambda b, i: (b, 0, i))
    out_shape = (
        jax.ShapeDtypeStruct((bsz, s, d), F32),
        jax.ShapeDtypeStruct((bsz, s, d), F32),
        jax.ShapeDtypeStruct((bsz, s, d), BF16),
        jax.ShapeDtypeStruct((bsz, s, d), F32),
        jax.ShapeDtypeStruct((bsz, s, d), F32),
        jax.ShapeDtypeStruct((bsz, s, IDX_DIM), BF16),
        jax.ShapeDtypeStruct((bsz, d, s), BF16),
        jax.ShapeDtypeStruct((bsz, d, s), BF16),
        jax.ShapeDtypeStruct((bsz, d, s), BF16),
        jax.ShapeDtypeStruct((bsz, IDX_HEADS, s), F32),
    )
    return pl.pallas_call(
        _inproj_kernel,
        grid=(bsz, s // tm),
        in_specs=[row(d),
                  pl.BlockSpec((1, N_MOD, d), lambda b, i: (b, 0, 0)),
                  pl.BlockSpec((1, d), lambda b, i: (0, 0)),
                  const(wn.shape), const(wki.shape), const(wt.shape), const(wwi.shape)],
        out_specs=(row(d), row(d), row(d), row(d), row(d), row(IDX_DIM),
                   col(d), col(d), col(d), col(IDX_HEADS)),
        out_shape=out_shape,
        compiler_params=_cparams("arbitrary", "arbitrary"),
        name="in_proj",
    )(x, mod3, g0, wn, wki, wt, wwi)


def _gelu_tanh(x):
    return 0.5 * x * (1.0 + jnp.tanh(math.sqrt(2.0 / math.pi) * (x + 0.044715 * (x * x * x))))


def _rglru_kernel(u_ref, ug_ref, cw_ref, cb_ref, wax_ref, ba_ref, bx_ref, lam_ref, y_ref,
                  ext_ref, a_ref, b_ref, carry_ref):
    ts = u_ref.shape[1]
    d = D_MODEL

    @pl.when(pl.program_id(1) == 0)
    def _():
        ext_ref[0:SUBLANES, :] = jnp.zeros((SUBLANES, d), F32)
        carry_ref[...] = jnp.zeros_like(carry_ref)

    ext_ref[SUBLANES:SUBLANES + ts, :] = u_ref[0]
    xc = cb_ref[...] + cw_ref[CONV_W - 1:CONV_W, :] * ext_ref[SUBLANES:SUBLANES + ts, :]
    for k in range(CONV_W - 1):
        off = SUBLANES - (CONV_W - 1) + k
        xc = xc + cw_ref[k:k + 1, :] * ext_ref[off:off + ts, :]
    ext_ref[0:SUBLANES, :] = ext_ref[ts:ts + SUBLANES, :]

    nl = -lam_ref[...]
    sp = jnp.maximum(nl, 0.0) + jnp.log1p(jnp.exp(-jnp.abs(nl)))
    for n in range(RNN_BLOCKS):
        cs = slice(n * RNN_BW, (n + 1) * RNN_BW)
        xb = xc[:, cs]
        g = _bdot(xb, wax_ref[n])
        r = jax.nn.sigmoid(g[:, :RNN_BW] + ba_ref[:, cs])
        i = jax.nn.sigmoid(g[:, RNN_BW:] + bx_ref[:, cs])
        log_a = (-LRU_C) * r * sp[:, cs]
        a_ref[:, cs] = jnp.exp(log_a)
        th = jnp.tanh(log_a)
        b_ref[:, cs] = jnp.sqrt(-2.0 * th / (1.0 - th)) * (i * xb)

    row = lax.broadcasted_iota(jnp.int32, (SUBLANES, d), 0)

    def group(gi, hprev):
        r0 = pl.multiple_of(gi * SUBLANES, SUBLANES)
        a = a_ref[pl.ds(r0, SUBLANES), :]
        b = b_ref[pl.ds(r0, SUBLANES), :]
        for sh in (1, 2, 4):
            keep = row >= sh
            a_s = jnp.where(keep, pltpu.roll(a, sh, 0), 1.0)
            b_s = jnp.where(keep, pltpu.roll(b, sh, 0), 0.0)
            b = a * b_s + b
            a = a * a_s
        h = b + a * hprev
        b_ref[pl.ds(r0, SUBLANES), :] = h
        return jnp.broadcast_to(h[SUBLANES - 1:SUBLANES, :], (SUBLANES, d))

    carry_ref[...] = lax.fori_loop(0, ts // SUBLANES, group, carry_ref[...])
    y_ref[0] = (b_ref[...] * _gelu_tanh(ug_ref[0])).astype(BF16)


def _rglru(u_rnn, u_gate, conv_w, conv_b, w_rg_a, b_rg_a, w_rg_x, b_rg_x, lam):
    bsz, s, d = u_rnn.shape
    ts = SCAN_TILE
    wax = jnp.concatenate([w_rg_a, w_rg_x], axis=-1).astype(BF16)
    vec = lambda v: v.reshape(1, d)
    c2 = lambda shape: pl.BlockSpec(shape, lambda b, i: (0,) * len(shape))
    row = pl.BlockSpec((1, ts, d), lambda b, i: (b, i, 0))
    return pl.pallas_call(
        _rglru_kernel,
        grid=(bsz, s // ts),
        in_specs=[row, row, c2((CONV_W, d)), c2((1, d)), c2(wax.shape), c2((1, d)), c2((1, d)),
                  c2((1, d))],
        out_specs=row,
        out_shape=jax.ShapeDtypeStruct((bsz, s, d), BF16),
        scratch_shapes=[pltpu.VMEM((ts + SUBLANES, d), F32), pltpu.VMEM((ts, d), F32),
                        pltpu.VMEM((ts, d), F32), pltpu.VMEM((SUBLANES, d), F32)],
        compiler_params=_cparams("arbitrary", "arbitrary"),
        name="rglru",
    )(u_rnn, u_gate, conv_w, vec(conv_b), wax, vec(b_rg_a), vec(b_rg_x), vec(lam))


def _t5_bucket_np(dist):
    max_exact = N_BUCKETS // 2
    dd = np.maximum(dist, 0)
    df = np.maximum(dd, 1).astype(np.float32)
    large = max_exact + (np.log(df / np.float32(max_exact)) / np.float32(math.log(MAX_DIST / max_exact))
                         * np.float32(N_BUCKETS - max_exact)).astype(np.int32)
    large = np.minimum(large, N_BUCKETS - 1)
    return np.where(dd < max_exact, dd, large)


def _near_bucket_ids():
    r = np.arange(KEY_GROUP)[None, :, None]
    c = np.arange(Q_TILE)[None, None, :]
    o = np.arange(2)[:, None, None]
    return _t5_bucket_np(c - r - (o - 1) * KEY_GROUP).astype(np.int32)


def _attn_kernel(rb_ref, bkt_ref, qt_ref, qit_ref, wit_ref, k_ref, vt_ref, ki_ref, y_ref,
                 s_ref, tab_ref, acc_ref, m_ref, lg_ref):
    tq = Q_TILE
    ks = KEY_SUB
    kg = KEY_GROUP
    jq = pl.program_id(1)
    t0 = jq * tq
    ngrp = jq + 1
    lane_t = t0 + lax.broadcasted_iota(jnp.int32, (1, tq), 1)

    @pl.when((pl.program_id(0) == 0) & (jq == 0))
    def _():
        for o in range(2):
            for h in range(N_HEADS):
                tab_ref[h, o] = jnp.zeros((kg, tq), F32)

            def fill(b, c):
                hit = bkt_ref[o] == b
                for h in range(N_HEADS):
                    val = (rb_ref[b, h] - rb_ref[N_BUCKETS - 1, h]) * LOG2E
                    tab_ref[h, o] = jnp.where(hit, val, tab_ref[h, o])
                return c

            lax.fori_loop(0, N_BUCKETS - 1, fill, 0)

    wi = wit_ref[0] * (IDX_HEADS ** -0.5 * IDX_DIM ** -0.5)

    def score_sub(i, mnmx, masked):
        r0 = pl.multiple_of(i * ks, ks)
        kic = ki_ref[0, pl.ds(r0, ks), :]
        acc = jnp.zeros((ks, tq), F32)
        for h in range(IDX_HEADS):
            dts = jnp.dot(kic, qit_ref[0, h * IDX_DIM:(h + 1) * IDX_DIM, :],
                          preferred_element_type=F32)
            acc = acc + jnp.maximum(dts, 0.0) * wi[h:h + 1, :]
        lo_src = acc
        if masked:
            key_s = r0 + lax.broadcasted_iota(jnp.int32, (ks, tq), 0)
            causal = key_s <= lane_t
            acc = jnp.where(causal, acc, NEG_INF)
            lo_src = jnp.where(causal, acc, jnp.inf)
        s_ref[pl.ds(r0, ks), :] = acc
        mn, mx = mnmx
        mn = jnp.minimum(mn, jnp.min(lo_src.reshape(ks // SUBLANES, SUBLANES, tq), axis=0))
        mx = jnp.maximum(mx, jnp.max(acc.reshape(ks // SUBLANES, SUBLANES, tq), axis=0))
        return mn, mx

    mnmx = (jnp.full((SUBLANES, tq), jnp.inf, F32), jnp.full((SUBLANES, tq), NEG_INF, F32))
    def _score4(q, c):
        for u in range(4):
            c = score_sub(4 * q + u, c, False)
        return c

    n_plain = 2 * ngrp - 2
    mnmx = lax.fori_loop(0, n_plain // 4, _score4, mnmx)
    mnmx = lax.cond(n_plain % 4 == 2,
                    lambda c: score_sub(n_plain - 1, score_sub(n_plain - 2, c, False), False),
                    lambda c: c, mnmx)
    mnmx = score_sub(2 * ngrp - 2, mnmx, True)
    mn8, mx8 = score_sub(2 * ngrp - 1, mnmx, True)
    smin = jnp.min(mn8, axis=0, keepdims=True)
    smax = jnp.max(mx8, axis=0, keepdims=True)

    n_causal = (lane_t + 1).astype(F32)
    k_eff = jnp.minimum(n_causal, float(TOPK_MAX))

    def count_rows(pred):
        part = 2 * SUBLANES

        def block(start, rows, c):
            r0 = pl.multiple_of(start, rows)
            key_s = (r0 + lax.broadcasted_iota(jnp.int32, (rows, tq), 0)).astype(F32)
            ind = jnp.where(pred(s_ref[pl.ds(r0, rows), :], key_s), 1.0, 0.0)
            return c + jnp.sum(ind.reshape(rows // part, part, tq), axis=0)

        c = lax.fori_loop(0, ngrp // 2, lambda g, c: block(g * (2 * kg), 2 * kg, c),
                          jnp.zeros((part, tq), F32))
        c = lax.cond(ngrp % 2 == 1, lambda c: block((ngrp - 1) * kg, kg, c), lambda c: c, c)
        return jnp.sum(c, axis=0, keepdims=True)

    def bis_step(st):
        it, lo, hi, c_lo, done = st
        first = (jnp.zeros((1, tq), F32) + jnp.where(it == 0, 1.0, 0.0)) > 0.0
        probe = jnp.where(first, smax, 0.5 * lo + 0.5 * hi)
        collapsed = ~first & ((probe <= lo) | (probe >= hi))
        cnt = count_rows(lambda blk, _: blk >= probe)
        ge = cnt >= k_eff
        upd = (done == 0.0) & ~collapsed
        lo_n = jnp.where(upd & ge, probe, lo)
        c_lo_n = jnp.where(upd & ge, cnt, c_lo)
        hi_n = jnp.where(upd & ~ge, probe, hi)
        fin = collapsed | (cnt == k_eff) | (first & ge)
        done_n = jnp.where(fin, 1.0, done)
        return it + 1, lo_n, hi_n, c_lo_n, done_n

    def n_open(st):
        return jnp.sum(1.0 - st[4])

    done0 = jnp.where(n_causal <= k_eff, 1.0, 0.0)
    st = lax.fori_loop(0, UNCHECKED_BISECT_STEPS, lambda _, s: bis_step(s),
                       (jnp.int32(0), smin, smax, n_causal, done0))
    st = lax.while_loop(lambda s: s[5] > 0.0,
                        lambda s: (lambda nxt: nxt + (n_open(nxt),))(bis_step(s[:5])),
                        st + (n_open(st),))
    thr, c_thr = st[1], st[3]

    tie_all = jnp.zeros((1, tq), F32) + (t0 + tq).astype(F32)

    def tie_limit():
        need = k_eff - count_rows(lambda blk, _: blk > thr)

        def body(_, st):
            lo, hi = st
            mid = jnp.floor(0.5 * (lo + hi))
            ok = count_rows(lambda blk, key_s: (blk == thr) & (key_s < mid)) >= need
            return jnp.where(ok, lo, mid), jnp.where(ok, mid, hi)

        n_steps = int(math.ceil(math.log2(s_ref.shape[0]))) + 1
        return lax.fori_loop(0, n_steps, body, (jnp.zeros((1, tq), F32), tie_all))[1]

    excess = jnp.sum(jnp.where(c_thr > k_eff, 1.0, 0.0))
    tie_lim = lax.cond(excess > 0.0, tie_limit, lambda: tie_all)

    def mask_body(g, c):
        r0 = pl.multiple_of(g * kg, kg)
        blk = s_ref[pl.ds(r0, kg), :]
        key_s = (r0 + lax.broadcasted_iota(jnp.int32, (kg, tq), 0)).astype(F32)
        sel = (blk > thr) | ((blk == thr) & (key_s < tie_lim))
        s_ref[pl.ds(r0, kg), :] = jnp.where(sel, 0.0, NEG_INF)
        return c

    lax.fori_loop(0, ngrp, mask_body, 0)

    m_ref[...] = jnp.full_like(m_ref, NEG_INF)
    acc_ref[...] = jnp.zeros_like(acc_ref)
    ones_rows = jnp.ones((DEN_ROWS, kg), BF16)

    def attend(g, near):
        r0 = pl.multiple_of(g * kg, kg)
        msk = s_ref[pl.ds(r0, kg), :]
        col_max = []
        for h in range(N_HEADS):
            hs = slice(h * HEAD_DIM, (h + 1) * HEAD_DIM)
            kh = k_ref[0, pl.ds(r0, kg), hs]
            lg = jnp.dot(kh, qt_ref[0, hs, :], preferred_element_type=F32) + msk
            if near is not None:
                lg = lg + tab_ref[h, near]
            lg_ref[h] = lg
            col_max.append(jnp.max(lg, axis=0, keepdims=True))
        for h in range(N_HEADS):
            hs = slice(h * HEAD_DIM, (h + 1) * HEAD_DIM)
            m_old = m_ref[h:h + 1, :]
            m_new = jnp.maximum(m_old, col_max[h])
            m_safe = jnp.where(m_new == NEG_INF, 0.0, m_new)
            p = jnp.exp2(lg_ref[h] - m_safe).astype(BF16)
            alpha = jnp.exp2(m_old - m_safe)
            m_ref[h:h + 1, :] = m_new
            vh = jnp.concatenate([vt_ref[0, hs, pl.ds(r0, kg)], ones_rows], axis=0)
            acc_ref[h] = alpha * acc_ref[h] + jnp.dot(vh, p, preferred_element_type=F32)

    n_far = jnp.maximum(ngrp - 2, 0)

    def _attend_quad(gq, c):
        for u in range(4):
            attend(4 * gq + u, None)
        return c

    lax.fori_loop(0, n_far // 4, _attend_quad, 0)

    @pl.when(n_far % 4 >= 2)
    def _():
        attend(n_far // 4 * 4, None)
        attend(n_far // 4 * 4 + 1, None)

    @pl.when(n_far % 2 == 1)
    def _():
        attend(n_far - 1, None)

    @pl.when(jq > 0)
    def _():
        attend(ngrp - 2, 0)

    attend(ngrp - 1, 1)

    for h in range(N_HEADS):
        o = acc_ref[h, 0:HEAD_DIM, :] / acc_ref[h, HEAD_DIM:HEAD_DIM + 1, :]
        y_ref[0, :, h * HEAD_DIM:(h + 1) * HEAD_DIM] = o.T.astype(BF16)


def _attention(qt, qit, wit, k, vt, ki, rel_bias):
    bsz, d, s = qt.shape
    tq = Q_TILE
    bkt = jnp.asarray(_near_bucket_ids())
    once = lambda shape: pl.BlockSpec(shape, lambda b, j: (b,) + (0,) * (len(shape) - 1),
                                      pipeline_mode=pl.Buffered(1))
    col = lambda r: pl.BlockSpec((1, r, tq), lambda b, j: (b, 0, j))
    return pl.pallas_call(
        _attn_kernel,
        grid=(bsz, s // tq),
        in_specs=[pl.BlockSpec(memory_space=pltpu.SMEM),
                  pl.BlockSpec(bkt.shape, lambda b, j: (0, 0, 0)),
                  col(d), col(d), col(IDX_HEADS),
                  once((1, s, d)), once((1, d, s)), once((1, s, IDX_DIM))],
        out_specs=pl.BlockSpec((1, tq, d), lambda b, j: (b, j, 0)),
        out_shape=jax.ShapeDtypeStruct((bsz, s, d), BF16),
        scratch_shapes=[pltpu.VMEM((s, tq), F32),
                        pltpu.VMEM((N_HEADS, 2, KEY_GROUP, tq), F32),
                        pltpu.VMEM((N_HEADS, HEAD_DIM + DEN_ROWS, tq), F32),
                        pltpu.VMEM((N_HEADS, tq), F32),
                        pltpu.VMEM((N_HEADS, KEY_GROUP, tq), F32)],
        compiler_params=_cparams("arbitrary", "arbitrary"),
        name="attn",
    )(rel_bias, bkt, qt, qit, wit, k, vt, ki)


def _pack_bf16_pair(x):
    c = x.shape[1] // 2
    lo = lax.bitcast_convert_type(x[:, :c].astype(BF16).astype(F32), jnp.uint32) >> 16
    hi = lax.bitcast_convert_type(x[:, c:].astype(BF16).astype(F32), jnp.uint32) & jnp.uint32(0xFFFF0000)
    return lo | hi


def _unpack_bf16_pair(p):
    a = lax.bitcast_convert_type(p << 16, F32)
    b = lax.bitcast_convert_type(p & jnp.uint32(0xFFFF0000), F32)
    return a, b


def _unpack_rows_bf16(p):
    a, b = _unpack_bf16_pair(p)
    return jnp.concatenate([a, b], axis=1).astype(BF16)


def _merge_kernel(yr_ref, ya_ref, glr_ref, gla_ref, x_ref, mod_ref, g_ref,
                  wr_ref, wa_ref, wo_ref, x1_ref, h2p_ref):
    merged = (jax.nn.sigmoid(glr_ref[0]) * jnp.dot(yr_ref[0], wr_ref[...], preferred_element_type=F32)
              + jax.nn.sigmoid(gla_ref[0]) * jnp.dot(ya_ref[0], wa_ref[...], preferred_element_type=F32))
    y = _bdot(merged, wo_ref[...])
    x1 = x_ref[0] + mod_ref[0, 2:3, :] * _rms(y, g_ref[1:2, :])
    x1_ref[0] = x1
    h2 = _rms(x1, g_ref[2:3, :]) * (1.0 + mod_ref[0, 4:5, :]) + mod_ref[0, 3:4, :]
    h2p_ref[0] = _pack_bf16_pair(h2)


def _merge(y_rnn, y_attn, gl_rnn, gl_attn, x, mod3, gains, w_br_rnn, w_br_attn, w_out):
    bsz, s, d = x.shape
    tm = ROW_TILE
    row = pl.BlockSpec((1, tm, d), lambda b, i: (b, i, 0))
    half = pl.BlockSpec((1, tm, d // 2), lambda b, i: (b, i, 0))
    c2 = lambda shape: pl.BlockSpec(shape, lambda b, i: (0,) * len(shape))
    return pl.pallas_call(
        _merge_kernel,
        grid=(bsz, s // tm),
        in_specs=[row, row, row, row, row,
                  pl.BlockSpec((1, N_MOD, d), lambda b, i: (b, 0, 0)),
                  c2(gains.shape), c2((d, d)), c2((d, d)), c2((d, d))],
        out_specs=(row, half),
        out_shape=(jax.ShapeDtypeStruct((bsz, s, d), F32),
                   jax.ShapeDtypeStruct((bsz, s, d // 2), jnp.uint32)),
        compiler_params=_cparams("arbitrary", "arbitrary"),
        name="merge",
    )(y_rnn, y_attn, gl_rnn, gl_attn, x, mod3, gains,
      w_br_rnn.astype(BF16), w_br_attn.astype(BF16), w_out.astype(BF16))


def _router_kernel(hp_ref, wr_ref, rb_ref, dest_ref, wk_ref, seg_ref):
    t = hp_ref.shape[0]
    gsz = N_EXPERTS // N_GROUPS
    h = _unpack_rows_bf16(hp_ref[...])
    s = jax.nn.sigmoid(_bdot_nt(wr_ref[...], h))
    s_sel = s + rb_ref[...]
    g3 = s_sel.reshape(N_GROUPS, gsz, t)
    e_in_g = lax.broadcasted_iota(jnp.int32, (N_GROUPS, gsz, t), 1)
    top1 = jnp.max(g3, axis=1, keepdims=True)
    first = jnp.min(jnp.where(g3 == top1, e_in_g, gsz), axis=1, keepdims=True)
    top2 = jnp.max(jnp.where(e_in_g == first, NEG_INF, g3), axis=1, keepdims=True)
    gscore = jnp.broadcast_to(top1 + top2, (N_GROUPS, gsz, t))
    gi = lax.broadcasted_iota(jnp.int32, (N_GROUPS, gsz, t), 0)
    gmask = jnp.zeros((N_GROUPS, gsz, t), F32)
    for _ in range(TOPK_GROUPS):
        mx = jnp.max(gscore, axis=0, keepdims=True)
        pick = jnp.min(jnp.where(gscore == mx, gi, N_GROUPS), axis=0, keepdims=True)
        hit = gi == pick
        gmask = jnp.where(hit, 1.0, gmask)
        gscore = jnp.where(hit, NEG_INF, gscore)
    cand = jnp.where(gmask.reshape(N_EXPERTS, t) > 0.0, s_sel, NEG_INF)
    ei = lax.broadcasted_iota(jnp.int32, (N_EXPERTS, t), 0)
    sel = jnp.zeros((N_EXPERTS, t), F32)
    picks = []
    for _ in range(TOP_K):
        mx = jnp.max(cand, axis=0, keepdims=True)
        pick = jnp.min(jnp.where(cand == mx, ei, N_EXPERTS), axis=0, keepdims=True)
        hit = ei == pick
        sel = jnp.where(hit, 1.0, sel)
        cand = jnp.where(hit, NEG_INF, cand)
        picks.append(pick)
    w = s * sel
    w = w / jnp.sum(w, axis=0, keepdims=True) * ROUTED_SCALE

    selb = sel.astype(BF16)
    tok_r = lax.broadcasted_iota(jnp.int32, (t, t), 0)
    tok_c = lax.broadcasted_iota(jnp.int32, (t, t), 1)
    rank = jnp.dot(selb, jnp.where(tok_r < tok_c, 1.0, 0.0).astype(BF16), preferred_element_type=F32)
    e_r = lax.broadcasted_iota(jnp.int32, (N_EXPERTS, N_EXPERTS), 0)
    e_c = lax.broadcasted_iota(jnp.int32, (N_EXPERTS, N_EXPERTS), 1)
    cnt_col = jnp.sum(sel, axis=1, keepdims=True)
    pad_col = jnp.floor((cnt_col + (SUBLANES - 1.0)) * (1.0 / SUBLANES))
    off_col = SUBLANES * jnp.dot(jnp.where(e_c < e_r, 1.0, 0.0).astype(BF16),
                                 jnp.broadcast_to(pad_col, (N_EXPERTS, LANES)).astype(BF16),
                                 preferred_element_type=F32)[:, 0:1]
    slot = rank + off_col
    dest_rows, w_rows = [], []
    for k in range(TOP_K):
        hit = ei == picks[k]
        dest_rows.append(jnp.sum(jnp.where(hit, slot, 0.0), axis=0, keepdims=True))
        w_rows.append(jnp.sum(jnp.where(hit, w, 0.0), axis=0, keepdims=True))
    dest_ref[0] = (jnp.concatenate(dest_rows, axis=0) * SLAB).astype(jnp.int32)
    wk_ref[...] = jnp.concatenate(w_rows, axis=0)
    cnt_row = _bdot_nt(jnp.ones((SUBLANES, t), BF16), selb)
    pad_row = jnp.floor((cnt_row + (SUBLANES - 1.0)) * (1.0 / SUBLANES))
    off_row = SUBLANES * jnp.dot(pad_row.astype(BF16), jnp.where(e_r < e_c, 1.0, 0.0).astype(BF16),
                                 preferred_element_type=F32)
    seg_ref[0] = jnp.concatenate([off_row[0:1], cnt_row[0:1]], axis=1).astype(jnp.int32)


def _router(h2p, w_router, router_bias):
    n, dh = h2p.shape
    t = MOE_TILE
    nt = n // t
    return pl.pallas_call(
        _router_kernel,
        grid=(nt,),
        in_specs=[pl.BlockSpec((t, dh), lambda i: (i, 0)),
                  pl.BlockSpec((N_EXPERTS, 2 * dh), lambda i: (0, 0)),
                  pl.BlockSpec((N_EXPERTS, 1), lambda i: (0, 0))],
        out_specs=(pl.BlockSpec((1, TOP_K, t), lambda i: (i, 0, 0)),
                   pl.BlockSpec((TOP_K, t), lambda i: (0, i)),
                   pl.BlockSpec((1, 1, 2 * N_EXPERTS), lambda i: (i, 0, 0))),
        out_shape=(jax.ShapeDtypeStruct((nt, TOP_K, t), jnp.int32),
                   jax.ShapeDtypeStruct((TOP_K, n), F32),
                   jax.ShapeDtypeStruct((nt, 1, 2 * N_EXPERTS), jnp.int32)),
        compiler_params=_cparams("arbitrary"),
        name="router",
    )(h2p, w_router.T.astype(BF16), router_bias.reshape(N_EXPERTS, 1))


def _swiglu(x, wgu, wd):
    gu = jnp.dot(x, wgu, preferred_element_type=F32)
    g = gu[:, :D_EXPERT]
    a = (g * jax.nn.sigmoid(g)) * gu[:, D_EXPERT:]
    return jnp.dot(a.astype(BF16), wd, preferred_element_type=F32)


def _expert_block(tile, step, n_blocks):
    sc = jnp.minimum(step, n_blocks - 1)
    return jnp.where(tile % 2 == 0, sc, n_blocks - 1 - sc)


def _moe_kernel(seg_ref, hp_ref, dest_ref, wk_ref, wgu_ref, wd_ref, wsgu_ref, wsd_ref, x1_ref,
                mod_ref, g_ref, o_ref, xs_ref, acca_ref, accb_ref, act_ref, dest_smem, w_smem, sem):
    i = pl.program_id(0)
    s = pl.program_id(1)
    n_s = pl.num_programs(1)
    t = MOE_TILE

    def slab(ref, first):
        return ref.at[pl.ds(pl.multiple_of(first, SLAB), SLAB), :]

    def load_rows(ref, r0, n):
        return jnp.concatenate([ref[pl.ds(SLAB * r0 + c, n, stride=SLAB), :] for c in range(SLAB)],
                               axis=1)

    @pl.when((i == 0) & (s == 0))
    def _():
        xs_ref[...] = jnp.zeros_like(xs_ref)

    @pl.when(s == 0)
    def _():
        copies = (pltpu.make_async_copy(dest_ref, dest_smem, sem.at[0]),
                  pltpu.make_async_copy(wk_ref, w_smem, sem.at[1]))
        for cp in copies:
            cp.start()
        for cp in copies:
            cp.wait()

        def body(nb, c):
            n0 = nb * MOVE_TOKENS
            base = n0 * TOP_K
            for j in range(MOVE_TOKENS):
                row = slab(hp_ref, (n0 + j) * SLAB)[...]
                for k in range(TOP_K):
                    slab(xs_ref, dest_smem[base + (j * TOP_K + k)])[...] = row
            return c

        lax.fori_loop(0, t // MOVE_TOKENS, body, 0)

    def put_rows(r0, end, y, old):
        new = _pack_bf16_pair(y)
        rows = r0 + lax.broadcasted_iota(jnp.int32, (FFN_ROWS, LANES), 0)
        for c in range(SLAB):
            cs = slice(c * LANES, (c + 1) * LANES)
            xs_ref[pl.ds(SLAB * r0 + c, FFN_ROWS, stride=SLAB), :] = jnp.where(rows < end, new[:, cs],
                                                                               old[:, cs])

    @pl.when(s < n_s - 1)
    def _():
        offs, ends = [], []
        for eb in range(EXPERTS_PER_STEP):
            e = _expert_block(i, s, n_s - 1) * EXPERTS_PER_STEP + eb
            offs.append(pl.multiple_of(seg_ref[i, e], SUBLANES))
            ends.append(offs[eb] + seg_ref[i, N_EXPERTS + e])
        olds = [load_rows(xs_ref, offs[eb], FFN_ROWS) for eb in range(EXPERTS_PER_STEP)]
        for eb in range(EXPERTS_PER_STEP):
            gu = jnp.dot(_unpack_rows_bf16(olds[eb]), wgu_ref[eb], preferred_element_type=F32)
            g = gu[:, :D_EXPERT]
            act_ref[eb] = ((g * jax.nn.sigmoid(g)) * gu[:, D_EXPERT:]).astype(BF16)
        ys = [jnp.dot(act_ref[eb], wd_ref[eb], preferred_element_type=F32)
              for eb in range(EXPERTS_PER_STEP)]
        for eb in range(EXPERTS_PER_STEP):
            put_rows(offs[eb], ends[eb], ys[eb], olds[eb])
        for eb in range(EXPERTS_PER_STEP):
            def chunk(c, carry, eb=eb):
                r0 = pl.multiple_of(offs[eb] + c * FFN_ROWS, SUBLANES)
                old = load_rows(xs_ref, r0, FFN_ROWS)
                put_rows(r0, ends[eb], _swiglu(_unpack_rows_bf16(old), wgu_ref[eb], wd_ref[eb]), old)
                return carry

            n_chunks = lax.div(ends[eb] - offs[eb] + (FFN_ROWS - 1), FFN_ROWS)
            lax.fori_loop(1, n_chunks, chunk, 0)

    @pl.when(s == n_s - 1)
    def _():
        o_ref[...] = _swiglu(_unpack_rows_bf16(load_rows(hp_ref, 0, t)), wsgu_ref[...], wsd_ref[...])

        def body(nb, c):
            n0 = pl.multiple_of(nb * SUBLANES, SUBLANES)
            base = n0 * TOP_K
            for j in range(SUBLANES):
                acc_a = jnp.zeros((SLAB, LANES), F32)
                acc_b = jnp.zeros((SLAB, LANES), F32)
                for k in range(TOP_K):
                    idx = base + (j * TOP_K + k)
                    a, b = _unpack_bf16_pair(slab(xs_ref, dest_smem[idx])[...])
                    w = w_smem[idx]
                    acc_a = acc_a + a * w
                    acc_b = acc_b + b * w
                acca_ref[j * SLAB:(j + 1) * SLAB, :] = acc_a
                accb_ref[j * SLAB:(j + 1) * SLAB, :] = acc_b
            lo = [acca_ref[pl.ds(c, SUBLANES, stride=SLAB), :] for c in range(SLAB)]
            hi = [accb_ref[pl.ds(c, SUBLANES, stride=SLAB), :] for c in range(SLAB)]
            o_ref[pl.ds(n0, SUBLANES), :] += jnp.concatenate(lo + hi, axis=1)
            return c

        lax.fori_loop(0, t // SUBLANES, body, 0)
        o_ref[...] = x1_ref[...] + mod_ref[0, 5:6, :] * _rms(o_ref[...], g_ref[...])


def _moe(h2p, dest, wk, seg, wgu, wd, wsgu, wsd, x1, mod3, g3, seq):
    n, dh = h2p.shape
    d = 2 * dh
    t = MOE_TILE
    n_steps = N_EXPERTS // EXPERTS_PER_STEP + 1
    tiles_per_seq = seq // t
    tok = lambda w: pl.BlockSpec((t, w), lambda i, s: (i, 0))
    table = pl.BlockSpec((TOP_K * t,), lambda i, s: (i,))
    wblk = lambda r, c: pl.BlockSpec((EXPERTS_PER_STEP, r, c),
                                     lambda i, s: (_expert_block(i, s, n_steps - 1), 0, 0))
    c2 = lambda shape: pl.BlockSpec(shape, lambda i, s: (0,) * len(shape))
    xs_rows = TOP_K * t + N_EXPERTS * SUBLANES + FFN_ROWS
    return pl.pallas_call(
        _moe_kernel,
        grid=(n // t, n_steps),
        in_specs=[pl.BlockSpec(memory_space=pltpu.SMEM),
                  pl.BlockSpec((SLAB * t, LANES), lambda i, s: (i, 0)),
                  table, table,
                  wblk(d, 2 * D_EXPERT), wblk(D_EXPERT, d),
                  c2((d, 2 * D_EXPERT)), c2((D_EXPERT, d)),
                  tok(d),
                  pl.BlockSpec((1, N_MOD, d), lambda i, s: (i // tiles_per_seq, 0, 0)),
                  c2((1, d))],
        out_specs=tok(d),
        out_shape=jax.ShapeDtypeStruct((n, d), F32),
        scratch_shapes=[pltpu.VMEM((SLAB * xs_rows, LANES), jnp.uint32),
                        pltpu.VMEM((SLAB * SUBLANES, LANES), F32),
                        pltpu.VMEM((SLAB * SUBLANES, LANES), F32),
                        pltpu.VMEM((EXPERTS_PER_STEP, FFN_ROWS, D_EXPERT), BF16),
                        pltpu.SMEM((TOP_K * t,), jnp.int32),
                        pltpu.SMEM((TOP_K * t,), F32),
                        pltpu.SemaphoreType.DMA((2,))],
        compiler_params=_cparams("arbitrary", "arbitrary"),
        name="moe",
    )(seg, h2p.reshape(n * SLAB, LANES), dest, wk, wgu, wd, wsgu, wsd, x1, mod3, g3)


def kernel(x, c, w_ada, b_ada, norm_gain, w_in, conv_w, conv_b, w_rg_a, b_rg_a, w_rg_x, b_rg_x,
           lru_lambda, w_br_rnn, w_br_attn, w_out, rel_bias, w_router, router_bias,
           w_exp_gate, w_exp_up, w_exp_down, w_sh_gate, w_sh_up, w_sh_down):
    bsz, s, d = x.shape
    depth = w_ada.shape[0]
    assert d == D_MODEL and all(s % t == 0 for t in (ROW_TILE, SCAN_TILE, Q_TILE, MOE_TILE))
    for l in range(depth):
        mod = _ada(c, w_ada[l], b_ada[l])
        mod3 = mod.reshape(bsz, N_MOD, d)
        gains = norm_gain[l]
        (u_rnn, u_gate, k, gl_rnn, gl_attn, ki, qt, vt, qit, wit) = _in_proj(x, mod3, gains[0:1], w_in[l])
        y_rnn = _rglru(u_rnn, u_gate, conv_w[l], conv_b[l], w_rg_a[l], b_rg_a[l], w_rg_x[l],
                       b_rg_x[l], lru_lambda[l])
        y_attn = _attention(qt, qit, wit, k, vt, ki, rel_bias)
        x1, h2p = _merge(y_rnn, y_attn, gl_rnn, gl_attn, x, mod3, gains, w_br_rnn[l], w_br_attn[l],
                         w_out[l])
        h2p = h2p.reshape(bsz * s, d // 2)
        dest, wk, seg = _router(h2p, w_router[l], router_bias[l])
        wgu = jnp.concatenate([w_exp_gate[l], w_exp_up[l]], axis=-1).astype(BF16)
        wsgu = jnp.concatenate([w_sh_gate[l], w_sh_up[l]], axis=-1).astype(BF16)
        dest = jnp.transpose(dest, (0, 2, 1)).reshape(-1)
        wk = jnp.transpose(wk).reshape(-1)
        x = _moe(h2p, dest, wk, seg.reshape(-1, 2 * N_EXPERTS), wgu, w_exp_down[l].astype(BF16),
                 wsgu, w_sh_down[l].astype(BF16), x1.reshape(bsz * s, d), mod3, gains[3:4],
                 s).reshape(bsz, s, d)
    return x
```

```python
import math

import jax
import jax.numpy as jnp
import numpy as np
from jax import lax
from jax.experimental import pallas as pl
from jax.experimental.pallas import tpu as pltpu

F32 = jnp.float32
BF16 = jnp.bfloat16

D_MODEL = 1024
RNN_BLOCKS = 8
RNN_BW = D_MODEL // RNN_BLOCKS
CONV_W = 4
LRU_C = 8.0
N_HEADS = 8
HEAD_DIM = 128
IDX_HEADS = 16
IDX_DIM = 64
TOPK_MAX = 256
N_BUCKETS = 32
MAX_DIST = 128
N_EXPERTS = 64
TOP_K = 8
N_GROUPS = 8
TOPK_GROUPS = 4
D_EXPERT = 256
ROUTED_SCALE = 2.5
N_MOD = 6
EPS = 1e-6

LANES = 128
SUBLANES = 8
VMEM_LIMIT_BYTES = 58 * 1024 * 1024

ROW_TILE = 512
UNCHECKED_BISECT_STEPS = 16
SCAN_TILE = 512
Q_TILE = 256
KEY_SUB = 128
KEY_GROUP = 256
DEN_ROWS = 16
MOE_TILE = 1024
EXPERTS_PER_STEP = 4
FFN_ROWS = 160
MOVE_TOKENS = 4
SLAB = 4

NEG_INF = float("-inf")
LOG2E = math.log2(math.e)


def _cparams(*sem):
    return pltpu.CompilerParams(dimension_semantics=sem, vmem_limit_bytes=VMEM_LIMIT_BYTES)


def _bdot(a, b):
    return jnp.dot(a.astype(BF16), b.astype(BF16), preferred_element_type=F32)


def _bdot_nt(a, b):
    return lax.dot_general(a.astype(BF16), b.astype(BF16), (((1,), (1,)), ((), ())),
                           preferred_element_type=F32)


def _rms(x, g):
    ms = jnp.mean(x * x, axis=-1, keepdims=True)
    return x * lax.rsqrt(ms + EPS) * g


def _ada_kernel(c_ref, w_ref, b_ref, o_ref):
    c = c_ref[...]
    cond = c * jax.nn.sigmoid(c)
    o_ref[...] = _bdot(cond, w_ref[...]) + b_ref[...]


def _ada(c, w_ada, b_ada):
    bsz, d = c.shape
    n = w_ada.shape[1]
    tn = 1024
    return pl.pallas_call(
        _ada_kernel,
        grid=(n // tn,),
        in_specs=[pl.BlockSpec((bsz, d), lambda j: (0, 0)),
                  pl.BlockSpec((d, tn), lambda j: (0, j)),
                  pl.BlockSpec((1, tn), lambda j: (0, j))],
        out_specs=pl.BlockSpec((bsz, tn), lambda j: (0, j)),
        out_shape=jax.ShapeDtypeStruct((bsz, n), F32),
        compiler_params=_cparams("arbitrary"),
        name="ada",
    )(c, w_ada, b_ada.reshape(1, n))


def _inproj_kernel(x_ref, mod_ref, g_ref, wn_ref, wki_ref, wt_ref, wwi_ref,
                   urnn_ref, ugate_ref, k_ref, glr_ref, gla_ref, ki_ref,
                   qt_ref, vt_ref, qit_ref, wit_ref):
    d = D_MODEL
    x = x_ref[0]
    h = _rms(x, g_ref[...]) * (1.0 + mod_ref[0, 1:2, :]) + mod_ref[0, 0:1, :]
    hb = h.astype(BF16)
    urnn_ref[0] = jnp.dot(hb, wn_ref[:, 0 * d:1 * d], preferred_element_type=F32)
    ugate_ref[0] = jnp.dot(hb, wn_ref[:, 1 * d:2 * d], preferred_element_type=F32)
    k_ref[0] = jnp.dot(hb, wn_ref[:, 2 * d:3 * d], preferred_element_type=F32).astype(BF16)
    glr_ref[0] = jnp.dot(hb, wn_ref[:, 3 * d:4 * d], preferred_element_type=F32)
    gla_ref[0] = jnp.dot(hb, wn_ref[:, 4 * d:5 * d], preferred_element_type=F32)
    ki_ref[0] = jnp.dot(hb, wki_ref[...], preferred_element_type=F32).astype(BF16)
    nt = (((1,), (1,)), ((), ()))
    qt_ref[0] = (lax.dot_general(wt_ref[0 * d:1 * d, :], hb, nt, preferred_element_type=F32)
                 * (HEAD_DIM ** -0.5 * LOG2E)).astype(BF16)
    vt_ref[0] = lax.dot_general(wt_ref[1 * d:2 * d, :], hb, nt, preferred_element_type=F32).astype(BF16)
    qit_ref[0] = lax.dot_general(wt_ref[2 * d:3 * d, :], hb, nt, preferred_element_type=F32).astype(BF16)
    wit_ref[0] = lax.dot_general(wwi_ref[...], hb, nt, preferred_element_type=F32)


def _in_proj(x, mod3, g0, w_in):
    bsz, s, d = x.shape
    tm = ROW_TILE
    offs = np.cumsum([0, d, d, d, d, d, IDX_HEADS * IDX_DIM, IDX_DIM, IDX_HEADS, d, d])
    seg = lambda i: w_in[:, int(offs[i]):int(offs[i + 1])]
    wn = jnp.concatenate([seg(0), seg(1), seg(3), seg(8), seg(9)], axis=1).astype(BF16)
    wki = seg(6).astype(BF16)
    wt = jnp.concatenate([seg(2), seg(4), seg(5)], axis=1).T.astype(BF16)
    wwi = seg(7).T.astype(BF16)
    const = lambda shape: pl.BlockSpec(shape, lambda b, i: (0,) * len(shape),
                                       pipeline_mode=pl.Buffered(1))
    row = lambda w: pl.BlockSpec((1, tm, w), lambda b, i: (b, i, 0))
    col = lambda r: pl.BlockSpec((1, r, tm), lambda b, i: (b, 0, i))
    out_shape = (
        jax.ShapeDtypeStruct((bsz, s, d), F32),
        jax.ShapeDtypeStruct((bsz, s, d), F32),
        jax.ShapeDtypeStruct((bsz, s, d), BF16),
        jax.ShapeDtypeStruct((bsz, s, d), F32),
        jax.ShapeDtypeStruct((bsz, s, d), F32),
        jax.ShapeDtypeStruct((bsz, s, IDX_DIM), BF16),
        jax.ShapeDtypeStruct((bsz, d, s), BF16),
        jax.ShapeDtypeStruct((bsz, d, s), BF16),
        jax.ShapeDtypeStruct((bsz, d, s), BF16),
        jax.ShapeDtypeStruct((bsz, IDX_HEADS, s), F32),
    )
    return pl.pallas_call(
        _inproj_kernel,
        grid=(bsz, s // tm),
        in_specs=[row(d),
                  pl.BlockSpec((1, N_MOD, d), lambda b, i: (b, 0, 0)),
                  pl.BlockSpec((1, d), lambda b, i: (0, 0)),
                  const(wn.shape), const(wki.shape), const(wt.shape), const(wwi.shape)],
        out_specs=(row(d), row(d), row(d), row(d), row(d), row(IDX_DIM),
                   col(d), col(d), col(d), col(IDX_HEADS)),
        out_shape=out_shape,
        compiler_params=_cparams("arbitrary", "arbitrary"),
        name="in_proj",
    )(x, mod3, g0, wn, wki, wt, wwi)


def _gelu_tanh(x):
    return 0.5 * x * (1.0 + jnp.tanh(math.sqrt(2.0 / math.pi) * (x + 0.044715 * (x * x * x))))


def _rglru_kernel(u_ref, ug_ref, cw_ref, cb_ref, wax_ref, ba_ref, bx_ref, lam_ref, y_ref,
                  ext_ref, a_ref, b_ref, carry_ref):
    ts = u_ref.shape[1]
    d = D_MODEL

    @pl.when(pl.program_id(1) == 0)
    def _():
        ext_ref[0:SUBLANES, :] = jnp.zeros((SUBLANES, d), F32)
        carry_ref[...] = jnp.zeros_like(carry_ref)

    ext_ref[SUBLANES:SUBLANES + ts, :] = u_ref[0]
    xc = cb_ref[...] + cw_ref[CONV_W - 1:CONV_W, :] * ext_ref[SUBLANES:SUBLANES + ts, :]
    for k in range(CONV_W - 1):
        off = SUBLANES - (CONV_W - 1) + k
        xc = xc + cw_ref[k:k + 1, :] * ext_ref[off:off + ts, :]
    ext_ref[0:SUBLANES, :] = ext_ref[ts:ts + SUBLANES, :]

    nl = -lam_ref[...]
    sp = jnp.maximum(nl, 0.0) + jnp.log1p(jnp.exp(-jnp.abs(nl)))
    for n in range(RNN_BLOCKS):
        cs = slice(n * RNN_BW, (n + 1) * RNN_BW)
        xb = xc[:, cs]
        g = _bdot(xb, wax_ref[n])
        r = jax.nn.sigmoid(g[:, :RNN_BW] + ba_ref[:, cs])
        i = jax.nn.sigmoid(g[:, RNN_BW:] + bx_ref[:, cs])
        log_a = (-LRU_C) * r * sp[:, cs]
        a_ref[:, cs] = jnp.exp(log_a)
        th = jnp.tanh(log_a)
        b_ref[:, cs] = jnp.sqrt(-2.0 * th / (1.0 - th)) * (i * xb)

    row = lax.broadcasted_iota(jnp.int32, (SUBLANES, d), 0)

    def group(gi, hprev):
        r0 = pl.multiple_of(gi * SUBLANES, SUBLANES)
        a = a_ref[pl.ds(r0, SUBLANES), :]
        b = b_ref[pl.ds(r0, SUBLANES), :]
        for sh in (1, 2, 4):
            keep = row >= sh
            a_s = jnp.where(keep, pltpu.roll(a, sh, 0), 1.0)
            b_s = jnp.where(keep, pltpu.roll(b, sh, 0), 0.0)
            b = a * b_s + b
            a = a * a_s
        h = b + a * hprev
        b_ref[pl.ds(r0, SUBLANES), :] = h
        return jnp.broadcast_to(h[SUBLANES - 1:SUBLANES, :], (SUBLANES, d))

    carry_ref[...] = lax.fori_loop(0, ts // SUBLANES, group, carry_ref[...])
    y_ref[0] = (b_ref[...] * _gelu_tanh(ug_ref[0])).astype(BF16)


def _rglru(u_rnn, u_gate, conv_w, conv_b, w_rg_a, b_rg_a, w_rg_x, b_rg_x, lam):
    bsz, s, d = u_rnn.shape
    ts = SCAN_TILE
    wax = jnp.concatenate([w_rg_a, w_rg_x], axis=-1).astype(BF16)
    vec = lambda v: v.reshape(1, d)
    c2 = lambda shape: pl.BlockSpec(shape, lambda b, i: (0,) * len(shape))
    row = pl.BlockSpec((1, ts, d), lambda b, i: (b, i, 0))
    return pl.pallas_call(
        _rglru_kernel,
        grid=(bsz, s // ts),
        in_specs=[row, row, c2((CONV_W, d)), c2((1, d)), c2(wax.shape), c2((1, d)), c2((1, d)),
                  c2((1, d))],
        out_specs=row,
        out_shape=jax.ShapeDtypeStruct((bsz, s, d), BF16),
        scratch_shapes=[pltpu.VMEM((ts + SUBLANES, d), F32), pltpu.VMEM((ts, d), F32),
                        pltpu.VMEM((ts, d), F32), pltpu.VMEM((SUBLANES, d), F32)],
        compiler_params=_cparams("arbitrary", "arbitrary"),
        name="rglru",
    )(u_rnn, u_gate, conv_w, vec(conv_b), wax, vec(b_rg_a), vec(b_rg_x), vec(lam))


def _t5_bucket_np(dist):
    max_exact = N_BUCKETS // 2
    dd = np.maximum(dist, 0)
    df = np.maximum(dd, 1).astype(np.float32)
    large = max_exact + (np.log(df / np.float32(max_exact)) / np.float32(math.log(MAX_DIST / max_exact))
                         * np.float32(N_BUCKETS - max_exact)).astype(np.int32)
    large = np.minimum(large, N_BUCKETS - 1)
    return np.where(dd < max_exact, dd, large)


def _near_bucket_ids():
    r = np.arange(KEY_GROUP)[None, :, None]
    c = np.arange(Q_TILE)[None, None, :]
    o = np.arange(2)[:, None, None]
    return _t5_bucket_np(c - r - (o - 1) * KEY_GROUP).astype(np.int32)


def _attn_kernel(rb_ref, bkt_ref, qt_ref, qit_ref, wit_ref, k_ref, vt_ref, ki_ref, y_ref,
                 s_ref, tab_ref, acc_ref, m_ref, lg_ref):
    tq = Q_TILE
    ks = KEY_SUB
    kg = KEY_GROUP
    jq = pl.program_id(1)
    t0 = jq * tq
    ngrp = jq + 1
    lane_t = t0 + lax.broadcasted_iota(jnp.int32, (1, tq), 1)

    @pl.when((pl.program_id(0) == 0) & (jq == 0))
    def _():
        for o in range(2):
            for h in range(N_HEADS):
                tab_ref[h, o] = jnp.zeros((kg, tq), F32)

            def fill(b, c):
                hit = bkt_ref[o] == b
                for h in range(N_HEADS):
                    val = (rb_ref[b, h] - rb_ref[N_BUCKETS - 1, h]) * LOG2E
                    tab_ref[h, o] = jnp.where(hit, val, tab_ref[h, o])
                return c

            lax.fori_loop(0, N_BUCKETS - 1, fill, 0)

    wi = wit_ref[0] * (IDX_HEADS ** -0.5 * IDX_DIM ** -0.5)

    def score_sub(i, mnmx, masked):
        r0 = pl.multiple_of(i * ks, ks)
        kic = ki_ref[0, pl.ds(r0, ks), :]
        acc = jnp.zeros((ks, tq), F32)
        for h in range(IDX_HEADS):
            dts = jnp.dot(kic, qit_ref[0, h * IDX_DIM:(h + 1) * IDX_DIM, :],
                          preferred_element_type=F32)
            acc = acc + jnp.maximum(dts, 0.0) * wi[h:h + 1, :]
        lo_src = acc
        if masked:
            key_s = r0 + lax.broadcasted_iota(jnp.int32, (ks, tq), 0)
            causal = key_s <= lane_t
            acc = jnp.where(causal, acc, NEG_INF)
            lo_src = jnp.where(causal, acc, jnp.inf)
        s_ref[pl.ds(r0, ks), :] = acc
        mn, mx = mnmx
        mn = jnp.minimum(mn, jnp.min(lo_src.reshape(ks // SUBLANES, SUBLANES, tq), axis=0))
        mx = jnp.maximum(mx, jnp.max(acc.reshape(ks // SUBLANES, SUBLANES, tq), axis=0))
        return mn, mx

    mnmx = (jnp.full((SUBLANES, tq), jnp.inf, F32), jnp.full((SUBLANES, tq), NEG_INF, F32))
    def _score_run(first, n, c):
        for u in range(n):
            c = score_sub(first + u, c, False)
        return c

    n_plain = 2 * ngrp - 2
    mnmx = lax.fori_loop(0, n_plain // 8, lambda q, c: _score_run(8 * q, 8, c), mnmx)
    mnmx = lax.cond(n_plain % 8 >= 4, lambda c: _score_run(n_plain // 8 * 8, 4, c), lambda c: c, mnmx)
    mnmx = lax.cond(n_plain % 4 == 2, lambda c: _score_run(n_plain - 2, 2, c), lambda c: c, mnmx)
    mnmx = score_sub(2 * ngrp - 2, mnmx, True)
    mn8, mx8 = score_sub(2 * ngrp - 1, mnmx, True)
    smin = jnp.min(mn8, axis=0, keepdims=True)
    smax = jnp.max(mx8, axis=0, keepdims=True)

    n_causal = (lane_t + 1).astype(F32)
    k_eff = jnp.minimum(n_causal, float(TOPK_MAX))

    def count_rows(pred):
        part = 2 * SUBLANES

        def block(start, rows, c):
            r0 = pl.multiple_of(start, rows)
            key_s = (r0 + lax.broadcasted_iota(jnp.int32, (rows, tq), 0)).astype(F32)
            ind = jnp.where(pred(s_ref[pl.ds(r0, rows), :], key_s), 1.0, 0.0)
            return c + jnp.sum(ind.reshape(rows // part, part, tq), axis=0)

        c = lax.fori_loop(0, ngrp // 2, lambda g, c: block(g * (2 * kg), 2 * kg, c),
                          jnp.zeros((part, tq), F32))
        c = lax.cond(ngrp % 2 == 1, lambda c: block((ngrp - 1) * kg, kg, c), lambda c: c, c)
        return jnp.sum(c, axis=0, keepdims=True)

    def bis_step(st):
        it, lo, hi, c_lo, done = st
        first = (jnp.zeros((1, tq), F32) + jnp.where(it == 0, 1.0, 0.0)) > 0.0
        probe = jnp.where(first, smax, 0.5 * lo + 0.5 * hi)
        collapsed = ~first & ((probe <= lo) | (probe >= hi))
        cnt = count_rows(lambda blk, _: blk >= probe)
        ge = cnt >= k_eff
        upd = (done == 0.0) & ~collapsed
        lo_n = jnp.where(upd & ge, probe, lo)
        c_lo_n = jnp.where(upd & ge, cnt, c_lo)
        hi_n = jnp.where(upd & ~ge, probe, hi)
        fin = collapsed | (cnt == k_eff) | (first & ge)
        done_n = jnp.where(fin, 1.0, done)
        return it + 1, lo_n, hi_n, c_lo_n, done_n

    def n_open(st):
        return jnp.sum(1.0 - st[4])

    done0 = jnp.where(n_causal <= k_eff, 1.0, 0.0)
    st = lax.fori_loop(0, UNCHECKED_BISECT_STEPS, lambda _, s: bis_step(s),
                       (jnp.int32(0), smin, smax, n_causal, done0))
    st = lax.while_loop(lambda s: s[5] > 0.0,
                        lambda s: (lambda nxt: nxt + (n_open(nxt),))(bis_step(s[:5])),
                        st + (n_open(st),))
    thr, c_thr = st[1], st[3]

    tie_all = jnp.zeros((1, tq), F32) + (t0 + tq).astype(F32)

    def tie_limit():
        need = k_eff - count_rows(lambda blk, _: blk > thr)

        def body(_, st):
            lo, hi = st
            mid = jnp.floor(0.5 * (lo + hi))
            ok = count_rows(lambda blk, key_s: (blk == thr) & (key_s < mid)) >= need
            return jnp.where(ok, lo, mid), jnp.where(ok, mid, hi)

        n_steps = int(math.ceil(math.log2(s_ref.shape[0]))) + 1
        return lax.fori_loop(0, n_steps, body, (jnp.zeros((1, tq), F32), tie_all))[1]

    excess = jnp.sum(jnp.where(c_thr > k_eff, 1.0, 0.0))
    tie_lim = lax.cond(excess > 0.0, tie_limit, lambda: tie_all)

    def mask_body(g, c):
        r0 = pl.multiple_of(g * kg, kg)
        blk = s_ref[pl.ds(r0, kg), :]
        key_s = (r0 + lax.broadcasted_iota(jnp.int32, (kg, tq), 0)).astype(F32)
        sel = (blk > thr) | ((blk == thr) & (key_s < tie_lim))
        s_ref[pl.ds(r0, kg), :] = jnp.where(sel, 0.0, NEG_INF)
        return c

    lax.fori_loop(0, ngrp, mask_body, 0)

    m_ref[...] = jnp.full_like(m_ref, NEG_INF)
    acc_ref[...] = jnp.zeros_like(acc_ref)
    ones_rows = jnp.ones((DEN_ROWS, kg), BF16)

    def attend(g, near):
        r0 = pl.multiple_of(g * kg, kg)
        msk = s_ref[pl.ds(r0, kg), :]
        col_max = []
        for h in range(N_HEADS):
            hs = slice(h * HEAD_DIM, (h + 1) * HEAD_DIM)
            kh = k_ref[0, pl.ds(r0, kg), hs]
            lg = jnp.dot(kh, qt_ref[0, hs, :], preferred_element_type=F32) + msk
            if near is not None:
                lg = lg + tab_ref[h, near]
            lg_ref[h] = lg
            col_max.append(jnp.max(lg, axis=0, keepdims=True))
        for h in range(N_HEADS):
            hs = slice(h * HEAD_DIM, (h + 1) * HEAD_DIM)
            m_old = m_ref[h:h + 1, :]
            m_new = jnp.maximum(m_old, col_max[h])
            m_safe = jnp.where(m_new == NEG_INF, 0.0, m_new)
            p = jnp.exp2(lg_ref[h] - m_safe).astype(BF16)
            alpha = jnp.exp2(m_old - m_safe)
            m_ref[h:h + 1, :] = m_new
            vh = jnp.concatenate([vt_ref[0, hs, pl.ds(r0, kg)], ones_rows], axis=0)
            acc_ref[h] = alpha * acc_ref[h] + jnp.dot(vh, p, preferred_element_type=F32)

    n_far = jnp.maximum(ngrp - 2, 0)

    def _attend_quad(gq, c):
        for u in range(4):
            attend(4 * gq + u, None)
        return c

    lax.fori_loop(0, n_far // 4, _attend_quad, 0)

    @pl.when(n_far % 4 >= 2)
    def _():
        attend(n_far // 4 * 4, None)
        attend(n_far // 4 * 4 + 1, None)

    @pl.when(n_far % 2 == 1)
    def _():
        attend(n_far - 1, None)

    @pl.when(jq > 0)
    def _():
        attend(ngrp - 2, 0)

    attend(ngrp - 1, 1)

    for h in range(N_HEADS):
        o = acc_ref[h, 0:HEAD_DIM, :] / acc_ref[h, HEAD_DIM:HEAD_DIM + 1, :]
        y_ref[0, :, h * HEAD_DIM:(h + 1) * HEAD_DIM] = o.T.astype(BF16)


def _attention(qt, qit, wit, k, vt, ki, rel_bias):
    bsz, d, s = qt.shape
    tq = Q_TILE
    bkt = jnp.asarray(_near_bucket_ids())
    once = lambda shape: pl.BlockSpec(shape, lambda b, j: (b,) + (0,) * (len(shape) - 1),
                                      pipeline_mode=pl.Buffered(1))
    col = lambda r: pl.BlockSpec((1, r, tq), lambda b, j: (b, 0, j))
    return pl.pallas_call(
        _attn_kernel,
        grid=(bsz, s // tq),
        in_specs=[pl.BlockSpec(memory_space=pltpu.SMEM),
                  pl.BlockSpec(bkt.shape, lambda b, j: (0, 0, 0)),
                  col(d), col(d), col(IDX_HEADS),
                  once((1, s, d)), once((1, d, s)), once((1, s, IDX_DIM))],
        out_specs=pl.BlockSpec((1, tq, d), lambda b, j: (b, j, 0)),
        out_shape=jax.ShapeDtypeStruct((bsz, s, d), BF16),
        scratch_shapes=[pltpu.VMEM((s, tq), F32),
                        pltpu.VMEM((N_HEADS, 2, KEY_GROUP, tq), F32),
                        pltpu.VMEM((N_HEADS, HEAD_DIM + DEN_ROWS, tq), F32),
                        pltpu.VMEM((N_HEADS, tq), F32),
                        pltpu.VMEM((N_HEADS, KEY_GROUP, tq), F32)],
        compiler_params=_cparams("arbitrary", "arbitrary"),
        name="attn",
    )(rel_bias, bkt, qt, qit, wit, k, vt, ki)


def _pack_bf16_pair(x):
    c = x.shape[1] // 2
    lo = lax.bitcast_convert_type(x[:, :c].astype(BF16).astype(F32), jnp.uint32) >> 16
    hi = lax.bitcast_convert_type(x[:, c:].astype(BF16).astype(F32), jnp.uint32) & jnp.uint32(0xFFFF0000)
    return lo | hi


def _unpack_bf16_pair(p):
    a = lax.bitcast_convert_type(p << 16, F32)
    b = lax.bitcast_convert_type(p & jnp.uint32(0xFFFF0000), F32)
    return a, b


def _unpack_rows_bf16(p):
    a, b = _unpack_bf16_pair(p)
    return jnp.concatenate([a, b], axis=1).astype(BF16)


def _merge_kernel(yr_ref, ya_ref, glr_ref, gla_ref, x_ref, mod_ref, g_ref,
                  wr_ref, wa_ref, wo_ref, x1_ref, h2p_ref):
    merged = (jax.nn.sigmoid(glr_ref[0]) * jnp.dot(yr_ref[0], wr_ref[...], preferred_element_type=F32)
              + jax.nn.sigmoid(gla_ref[0]) * jnp.dot(ya_ref[0], wa_ref[...], preferred_element_type=F32))
    y = _bdot(merged, wo_ref[...])
    x1 = x_ref[0] + mod_ref[0, 2:3, :] * _rms(y, g_ref[1:2, :])
    x1_ref[0] = x1
    h2 = _rms(x1, g_ref[2:3, :]) * (1.0 + mod_ref[0, 4:5, :]) + mod_ref[0, 3:4, :]
    h2p_ref[0] = _pack_bf16_pair(h2)


def _merge(y_rnn, y_attn, gl_rnn, gl_attn, x, mod3, gains, w_br_rnn, w_br_attn, w_out):
    bsz, s, d = x.shape
    tm = ROW_TILE
    row = pl.BlockSpec((1, tm, d), lambda b, i: (b, i, 0))
    half = pl.BlockSpec((1, tm, d // 2), lambda b, i: (b, i, 0))
    c2 = lambda shape: pl.BlockSpec(shape, lambda b, i: (0,) * len(shape))
    return pl.pallas_call(
        _merge_kernel,
        grid=(bsz, s // tm),
        in_specs=[row, row, row, row, row,
                  pl.BlockSpec((1, N_MOD, d), lambda b, i: (b, 0, 0)),
                  c2(gains.shape), c2((d, d)), c2((d, d)), c2((d, d))],
        out_specs=(row, half),
        out_shape=(jax.ShapeDtypeStruct((bsz, s, d), F32),
                   jax.ShapeDtypeStruct((bsz, s, d // 2), jnp.uint32)),
        compiler_params=_cparams("arbitrary", "arbitrary"),
        name="merge",
    )(y_rnn, y_attn, gl_rnn, gl_attn, x, mod3, gains,
      w_br_rnn.astype(BF16), w_br_attn.astype(BF16), w_out.astype(BF16))


def _router_kernel(hp_ref, wr_ref, rb_ref, dest_ref, wk_ref, seg_ref):
    t = hp_ref.shape[0]
    gsz = N_EXPERTS // N_GROUPS
    h = _unpack_rows_bf16(hp_ref[...])
    s = jax.nn.sigmoid(_bdot_nt(wr_ref[...], h))
    s_sel = s + rb_ref[...]
    g3 = s_sel.reshape(N_GROUPS, gsz, t)
    e_in_g = lax.broadcasted_iota(jnp.int32, (N_GROUPS, gsz, t), 1)
    top1 = jnp.max(g3, axis=1, keepdims=True)
    first = jnp.min(jnp.where(g3 == top1, e_in_g, gsz), axis=1, keepdims=True)
    top2 = jnp.max(jnp.where(e_in_g == first, NEG_INF, g3), axis=1, keepdims=True)
    gscore = jnp.broadcast_to(top1 + top2, (N_GROUPS, gsz, t))
    gi = lax.broadcasted_iota(jnp.int32, (N_GROUPS, gsz, t), 0)
    gmask = jnp.zeros((N_GROUPS, gsz, t), F32)
    for _ in range(TOPK_GROUPS):
        mx = jnp.max(gscore, axis=0, keepdims=True)
        pick = jnp.min(jnp.where(gscore == mx, gi, N_GROUPS), axis=0, keepdims=True)
        hit = gi == pick
        gmask = jnp.where(hit, 1.0, gmask)
        gscore = jnp.where(hit, NEG_INF, gscore)
    cand = jnp.where(gmask.reshape(N_EXPERTS, t) > 0.0, s_sel, NEG_INF)
    ei = lax.broadcasted_iota(jnp.int32, (N_EXPERTS, t), 0)
    sel = jnp.zeros((N_EXPERTS, t), F32)
    picks = []
    for _ in range(TOP_K):
        mx = jnp.max(cand, axis=0, keepdims=True)
        pick = jnp.min(jnp.where(cand == mx, ei, N_EXPERTS), axis=0, keepdims=True)
        hit = ei == pick
        sel = jnp.where(hit, 1.0, sel)
        cand = jnp.where(hit, NEG_INF, cand)
        picks.append(pick)
    w = s * sel
    w = w / jnp.sum(w, axis=0, keepdims=True) * ROUTED_SCALE

    selb = sel.astype(BF16)
    tok_r = lax.broadcasted_iota(jnp.int32, (t, t), 0)
    tok_c = lax.broadcasted_iota(jnp.int32, (t, t), 1)
    rank = jnp.dot(selb, jnp.where(tok_r < tok_c, 1.0, 0.0).astype(BF16), preferred_element_type=F32)
    e_r = lax.broadcasted_iota(jnp.int32, (N_EXPERTS, N_EXPERTS), 0)
    e_c = lax.broadcasted_iota(jnp.int32, (N_EXPERTS, N_EXPERTS), 1)
    cnt_col = jnp.sum(sel, axis=1, keepdims=True)
    pad_col = jnp.floor((cnt_col + (SUBLANES - 1.0)) * (1.0 / SUBLANES))
    off_col = SUBLANES * jnp.dot(jnp.where(e_c < e_r, 1.0, 0.0).astype(BF16),
                                 jnp.broadcast_to(pad_col, (N_EXPERTS, LANES)).astype(BF16),
                                 preferred_element_type=F32)[:, 0:1]
    slot = rank + off_col
    dest_rows, w_rows = [], []
    for k in range(TOP_K):
        hit = ei == picks[k]
        dest_rows.append(jnp.sum(jnp.where(hit, slot, 0.0), axis=0, keepdims=True))
        w_rows.append(jnp.sum(jnp.where(hit, w, 0.0), axis=0, keepdims=True))
    dest_ref[0] = (jnp.concatenate(dest_rows, axis=0) * SLAB).astype(jnp.int32)
    wk_ref[...] = jnp.concatenate(w_rows, axis=0)
    cnt_row = _bdot_nt(jnp.ones((SUBLANES, t), BF16), selb)
    pad_row = jnp.floor((cnt_row + (SUBLANES - 1.0)) * (1.0 / SUBLANES))
    off_row = SUBLANES * jnp.dot(pad_row.astype(BF16), jnp.where(e_r < e_c, 1.0, 0.0).astype(BF16),
                                 preferred_element_type=F32)
    seg_ref[0] = jnp.concatenate([off_row[0:1], cnt_row[0:1]], axis=1).astype(jnp.int32)


def _router(h2p, w_router, router_bias):
    n, dh = h2p.shape
    t = MOE_TILE
    nt = n // t
    return pl.pallas_call(
        _router_kernel,
        grid=(nt,),
        in_specs=[pl.BlockSpec((t, dh), lambda i: (i, 0)),
                  pl.BlockSpec((N_EXPERTS, 2 * dh), lambda i: (0, 0)),
                  pl.BlockSpec((N_EXPERTS, 1), lambda i: (0, 0))],
        out_specs=(pl.BlockSpec((1, TOP_K, t), lambda i: (i, 0, 0)),
                   pl.BlockSpec((TOP_K, t), lambda i: (0, i)),
                   pl.BlockSpec((1, 1, 2 * N_EXPERTS), lambda i: (i, 0, 0))),
        out_shape=(jax.ShapeDtypeStruct((nt, TOP_K, t), jnp.int32),
                   jax.ShapeDtypeStruct((TOP_K, n), F32),
                   jax.ShapeDtypeStruct((nt, 1, 2 * N_EXPERTS), jnp.int32)),
        compiler_params=_cparams("arbitrary"),
        name="router",
    )(h2p, w_router.T.astype(BF16), router_bias.reshape(N_EXPERTS, 1))


def _swiglu(x, wgu, wd):
    gu = jnp.dot(x, wgu, preferred_element_type=F32)
    g = gu[:, :D_EXPERT]
    a = (g * jax.nn.sigmoid(g)) * gu[:, D_EXPERT:]
    return jnp.dot(a.astype(BF16), wd, preferred_element_type=F32)


def _expert_block(tile, step, n_blocks):
    sc = jnp.minimum(step, n_blocks - 1)
    return jnp.where(tile % 2 == 0, sc, n_blocks - 1 - sc)


def _moe_kernel(seg_ref, hp_ref, dest_ref, wk_ref, wgu_ref, wd_ref, wsgu_ref, wsd_ref, x1_ref,
                mod_ref, g_ref, o_ref, xs_ref, acca_ref, accb_ref, act_ref, dest_smem, w_smem, sem):
    i = pl.program_id(0)
    s = pl.program_id(1)
    n_s = pl.num_programs(1)
    t = MOE_TILE

    def slab(ref, first):
        return ref.at[pl.ds(pl.multiple_of(first, SLAB), SLAB), :]

    def load_rows(ref, r0, n):
        return jnp.concatenate([ref[pl.ds(SLAB * r0 + c, n, stride=SLAB), :] for c in range(SLAB)],
                               axis=1)

    @pl.when((i == 0) & (s == 0))
    def _():
        xs_ref[...] = jnp.zeros_like(xs_ref)

    @pl.when(s == 0)
    def _():
        copies = (pltpu.make_async_copy(dest_ref, dest_smem, sem.at[0]),
                  pltpu.make_async_copy(wk_ref, w_smem, sem.at[1]))
        for cp in copies:
            cp.start()
        for cp in copies:
            cp.wait()

        def body(nb, c):
            n0 = nb * MOVE_TOKENS
            base = n0 * TOP_K
            for j in range(MOVE_TOKENS):
                row = slab(hp_ref, (n0 + j) * SLAB)[...]
                for k in range(TOP_K):
                    slab(xs_ref, dest_smem[base + (j * TOP_K + k)])[...] = row
            return c

        lax.fori_loop(0, t // MOVE_TOKENS, body, 0)

    def put_rows(r0, end, y, old):
        new = _pack_bf16_pair(y)
        rows = r0 + lax.broadcasted_iota(jnp.int32, (FFN_ROWS, LANES), 0)
        for c in range(SLAB):
            cs = slice(c * LANES, (c + 1) * LANES)
            xs_ref[pl.ds(SLAB * r0 + c, FFN_ROWS, stride=SLAB), :] = jnp.where(rows < end, new[:, cs],
                                                                               old[:, cs])

    @pl.when(s < n_s - 1)
    def _():
        offs, ends = [], []
        for eb in range(EXPERTS_PER_STEP):
            e = _expert_block(i, s, n_s - 1) * EXPERTS_PER_STEP + eb
            offs.append(pl.multiple_of(seg_ref[i, e], SUBLANES))
            ends.append(offs[eb] + seg_ref[i, N_EXPERTS + e])
        olds = [load_rows(xs_ref, offs[eb], FFN_ROWS) for eb in range(EXPERTS_PER_STEP)]
        for eb in range(EXPERTS_PER_STEP):
            gu = jnp.dot(_unpack_rows_bf16(olds[eb]), wgu_ref[eb], preferred_element_type=F32)
            g = gu[:, :D_EXPERT]
            act_ref[eb] = ((g * jax.nn.sigmoid(g)) * gu[:, D_EXPERT:]).astype(BF16)
        ys = [jnp.dot(act_ref[eb], wd_ref[eb], preferred_element_type=F32)
              for eb in range(EXPERTS_PER_STEP)]
        for eb in range(EXPERTS_PER_STEP):
            put_rows(offs[eb], ends[eb], ys[eb], olds[eb])
        for eb in range(EXPERTS_PER_STEP):
            def chunk(c, carry, eb=eb):
                r0 = pl.multiple_of(offs[eb] + c * FFN_ROWS, SUBLANES)
                old = load_rows(xs_ref, r0, FFN_ROWS)
                put_rows(r0, ends[eb], _swiglu(_unpack_rows_bf16(old), wgu_ref[eb], wd_ref[eb]), old)
                return carry

            n_chunks = lax.div(ends[eb] - offs[eb] + (FFN_ROWS - 1), FFN_ROWS)
            lax.fori_loop(1, n_chunks, chunk, 0)

    @pl.when(s == n_s - 1)
    def _():
        o_ref[...] = _swiglu(_unpack_rows_bf16(load_rows(hp_ref, 0, t)), wsgu_ref[...], wsd_ref[...])

        def body(nb, c):
            n0 = pl.multiple_of(nb * SUBLANES, SUBLANES)
            base = n0 * TOP_K
            for j in range(SUBLANES):
                acc_a = jnp.zeros((SLAB, LANES), F32)
                acc_b = jnp.zeros((SLAB, LANES), F32)
                for k in range(TOP_K):
                    idx = base + (j * TOP_K + k)
                    a, b = _unpack_bf16_pair(slab(xs_ref, dest_smem[idx])[...])
                    w = w_smem[idx]
                    acc_a = acc_a + a * w
                    acc_b = acc_b + b * w
                acca_ref[j * SLAB:(j + 1) * SLAB, :] = acc_a
                accb_ref[j * SLAB:(j + 1) * SLAB, :] = acc_b
            lo = [acca_ref[pl.ds(c, SUBLANES, stride=SLAB), :] for c in range(SLAB)]
            hi = [accb_ref[pl.ds(c, SUBLANES, stride=SLAB), :] for c in range(SLAB)]
            o_ref[pl.ds(n0, SUBLANES), :] += jnp.concatenate(lo + hi, axis=1)
            return c

        lax.fori_loop(0, t // SUBLANES, body, 0)
        o_ref[...] = x1_ref[...] + mod_ref[0, 5:6, :] * _rms(o_ref[...], g_ref[...])


def _moe(h2p, dest, wk, seg, wgu, wd, wsgu, wsd, x1, mod3, g3, seq):
    n, dh = h2p.shape
    d = 2 * dh
    t = MOE_TILE
    n_steps = N_EXPERTS // EXPERTS_PER_STEP + 1
    tiles_per_seq = seq // t
    tok = lambda w: pl.BlockSpec((t, w), lambda i, s: (i, 0))
    table = pl.BlockSpec((TOP_K * t,), lambda i, s: (i,))
    wblk = lambda r, c: pl.BlockSpec((EXPERTS_PER_STEP, r, c),
                                     lambda i, s: (_expert_block(i, s, n_steps - 1), 0, 0))
    c2 = lambda shape: pl.BlockSpec(shape, lambda i, s: (0,) * len(shape))
    xs_rows = TOP_K * t + N_EXPERTS * SUBLANES + FFN_ROWS
    return pl.pallas_call(
        _moe_kernel,
        grid=(n // t, n_steps),
        in_specs=[pl.BlockSpec(memory_space=pltpu.SMEM),
                  pl.BlockSpec((SLAB * t, LANES), lambda i, s: (i, 0)),
                  table, table,
                  wblk(d, 2 * D_EXPERT), wblk(D_EXPERT, d),
                  c2((d, 2 * D_EXPERT)), c2((D_EXPERT, d)),
                  tok(d),
                  pl.BlockSpec((1, N_MOD, d), lambda i, s: (i // tiles_per_seq, 0, 0)),
                  c2((1, d))],
        out_specs=tok(d),
        out_shape=jax.ShapeDtypeStruct((n, d), F32),
        scratch_shapes=[pltpu.VMEM((SLAB * xs_rows, LANES), jnp.uint32),
                        pltpu.VMEM((SLAB * SUBLANES, LANES), F32),
                        pltpu.VMEM((SLAB * SUBLANES, LANES), F32),
                        pltpu.VMEM((EXPERTS_PER_STEP, FFN_ROWS, D_EXPERT), BF16),
                        pltpu.SMEM((TOP_K * t,), jnp.int32),
                        pltpu.SMEM((TOP_K * t,), F32),
                        pltpu.SemaphoreType.DMA((2,))],
        compiler_params=_cparams("arbitrary", "arbitrary"),
        name="moe",
    )(seg, h2p.reshape(n * SLAB, LANES), dest, wk, wgu, wd, wsgu, wsd, x1, mod3, g3)


def kernel(x, c, w_ada, b_ada, norm_gain, w_in, conv_w, conv_b, w_rg_a, b_rg_a, w_rg_x, b_rg_x,
           lru_lambda, w_br_rnn, w_br_attn, w_out, rel_bias, w_router, router_bias,
           w_exp_gate, w_exp_up, w_exp_down, w_sh_gate, w_sh_up, w_sh_down):
    bsz, s, d = x.shape
    depth = w_ada.shape[0]
    assert d == D_MODEL and all(s % t == 0 for t in (ROW_TILE, SCAN_TILE, Q_TILE, MOE_TILE))
    for l in range(depth):
        mod = _ada(c, w_ada[l], b_ada[l])
        mod3 = mod.reshape(bsz, N_MOD, d)
        gains = norm_gain[l]
        (u_rnn, u_gate, k, gl_rnn, gl_attn, ki, qt, vt, qit, wit) = _in_proj(x, mod3, gains[0:1], w_in[l])
        y_rnn = _rglru(u_rnn, u_gate, conv_w[l], conv_b[l], w_rg_a[l], b_rg_a[l], w_rg_x[l],
                       b_rg_x[l], lru_lambda[l])
        y_attn = _attention(qt, qit, wit, k, vt, ki, rel_bias)
        x1, h2p = _merge(y_rnn, y_attn, gl_rnn, gl_attn, x, mod3, gains, w_br_rnn[l], w_br_attn[l],
                         w_out[l])
        h2p = h2p.reshape(bsz * s, d // 2)
        dest, wk, seg = _router(h2p, w_router[l], router_bias[l])
        wgu = jnp.concatenate([w_exp_gate[l], w_exp_up[l]], axis=-1).astype(BF16)
        wsgu = jnp.concatenate([w_sh_gate[l], w_sh_up[l]], axis=-1).astype(BF16)
        dest = jnp.transpose(dest, (0, 2, 1)).reshape(-1)
        wk = jnp.transpose(wk).reshape(-1)
        x = _moe(h2p, dest, wk, seg.reshape(-1, 2 * N_EXPERTS), wgu, w_exp_down[l].astype(BF16),
                 wsgu, w_sh_down[l].astype(BF16), x1.reshape(bsz * s, d), mod3, gains[3:4],
                 s).reshape(bsz, s, d)
    return x
```

```python
import math

import jax
import jax.numpy as jnp
import numpy as np
from jax import lax
from jax.experimental import pallas as pl
from jax.experimental.pallas import tpu as pltpu

F32 = jnp.float32
BF16 = jnp.bfloat16

D_MODEL = 1024
RNN_BLOCKS = 8
RNN_BW = D_MODEL // RNN_BLOCKS
CONV_W = 4
LRU_C = 8.0
N_HEADS = 8
HEAD_DIM = 128
IDX_HEADS = 16
IDX_DIM = 64
TOPK_MAX = 256
N_BUCKETS = 32
MAX_DIST = 128
N_EXPERTS = 64
TOP_K = 8
N_GROUPS = 8
TOPK_GROUPS = 4
D_EXPERT = 256
ROUTED_SCALE = 2.5
N_MOD = 6
EPS = 1e-6

LANES = 128
SUBLANES = 8
VMEM_LIMIT_BYTES = 58 * 1024 * 1024

ROW_TILE = 512
UNCHECKED_BISECT_STEPS = 16
SCAN_TILE = 512
Q_TILE = 256
KEY_SUB = 128
KEY_GROUP = 256
DEN_ROWS = 16
MOE_TILE = 1024
EXPERTS_PER_STEP = 4
FFN_ROWS = 160
MOVE_TOKENS = 4
SLAB = 4

NEG_INF = float("-inf")
LOG2E = math.log2(math.e)


def _cparams(*sem):
    return pltpu.CompilerParams(dimension_semantics=sem, vmem_limit_bytes=VMEM_LIMIT_BYTES)


def _bdot(a, b):
    return jnp.dot(a.astype(BF16), b.astype(BF16), preferred_element_type=F32)


def _bdot_nt(a, b):
    return lax.dot_general(a.astype(BF16), b.astype(BF16), (((1,), (1,)), ((), ())),
                           preferred_element_type=F32)


def _rms(x, g):
    ms = jnp.mean(x * x, axis=-1, keepdims=True)
    return x * lax.rsqrt(ms + EPS) * g


def _ada_kernel(c_ref, w_ref, b_ref, o_ref):
    c = c_ref[...]
    cond = c * jax.nn.sigmoid(c)
    o_ref[...] = _bdot(cond, w_ref[...]) + b_ref[...]


def _ada(c, w_ada, b_ada):
    bsz, d = c.shape
    n = w_ada.shape[1]
    tn = 1024
    return pl.pallas_call(
        _ada_kernel,
        grid=(n // tn,),
        in_specs=[pl.BlockSpec((bsz, d), lambda j: (0, 0)),
                  pl.BlockSpec((d, tn), lambda j: (0, j)),
                  pl.BlockSpec((1, tn), lambda j: (0, j))],
        out_specs=pl.BlockSpec((bsz, tn), lambda j: (0, j)),
        out_shape=jax.ShapeDtypeStruct((bsz, n), F32),
        compiler_params=_cparams("arbitrary"),
        name="ada",
    )(c, w_ada, b_ada.reshape(1, n))


def _inproj_kernel(x_ref, mod_ref, g_ref, wn_ref, wki_ref, wt_ref, wwi_ref,
                   urnn_ref, ugate_ref, k_ref, glr_ref, gla_ref, ki_ref,
                   qt_ref, vt_ref, qit_ref, wit_ref):
    d = D_MODEL
    x = x_ref[0]
    h = _rms(x, g_ref[...]) * (1.0 + mod_ref[0, 1:2, :]) + mod_ref[0, 0:1, :]
    hb = h.astype(BF16)
    urnn_ref[0] = jnp.dot(hb, wn_ref[:, 0 * d:1 * d], preferred_element_type=F32)
    ugate_ref[0] = jnp.dot(hb, wn_ref[:, 1 * d:2 * d], preferred_element_type=F32)
    k_ref[0] = jnp.dot(hb, wn_ref[:, 2 * d:3 * d], preferred_element_type=F32).astype(BF16)
    glr_ref[0] = jnp.dot(hb, wn_ref[:, 3 * d:4 * d], preferred_element_type=F32)
    gla_ref[0] = jnp.dot(hb, wn_ref[:, 4 * d:5 * d], preferred_element_type=F32)
    ki_ref[0] = jnp.dot(hb, wki_ref[...], preferred_element_type=F32).astype(BF16)
    nt = (((1,), (1,)), ((), ()))
    qt_ref[0] = (lax.dot_general(wt_ref[0 * d:1 * d, :], hb, nt, preferred_element_type=F32)
                 * (HEAD_DIM ** -0.5 * LOG2E)).astype(BF16)
    vt_ref[0] = lax.dot_general(wt_ref[1 * d:2 * d, :], hb, nt, preferred_element_type=F32).astype(BF16)
    qit_ref[0] = lax.dot_general(wt_ref[2 * d:3 * d, :], hb, nt, preferred_element_type=F32).astype(BF16)
    wit_ref[0] = lax.dot_general(wwi_ref[...], hb, nt, preferred_element_type=F32)


def _in_proj(x, mod3, g0, w_in):
    bsz, s, d = x.shape
    tm = ROW_TILE
    offs = np.cumsum([0, d, d, d, d, d, IDX_HEADS * IDX_DIM, IDX_DIM, IDX_HEADS, d, d])
    seg = lambda i: w_in[:, int(offs[i]):int(offs[i + 1])]
    wn = jnp.concatenate([seg(0), seg(1), seg(3), seg(8), seg(9)], axis=1).astype(BF16)
    wki = seg(6).astype(BF16)
    wt = jnp.concatenate([seg(2), seg(4), seg(5)], axis=1).T.astype(BF16)
    wwi = seg(7).T.astype(BF16)
    const = lambda shape: pl.BlockSpec(shape, lambda b, i: (0,) * len(shape),
                                       pipeline_mode=pl.Buffered(1))
    row = lambda w: pl.BlockSpec((1, tm, w), lambda b, i: (b, i, 0))
    col = lambda r: pl.BlockSpec((1, r, tm), lambda b, i: (b, 0, i))
    out_shape = (
        jax.ShapeDtypeStruct((bsz, s, d), F32),
        jax.ShapeDtypeStruct((bsz, s, d), F32),
        jax.ShapeDtypeStruct((bsz, s, d), BF16),
        jax.ShapeDtypeStruct((bsz, s, d), F32),
        jax.ShapeDtypeStruct((bsz, s, d), F32),
        jax.ShapeDtypeStruct((bsz, s, IDX_DIM), BF16),
        jax.ShapeDtypeStruct((bsz, d, s), BF16),
        jax.ShapeDtypeStruct((bsz, d, s), BF16),
        jax.ShapeDtypeStruct((bsz, d, s), BF16),
        jax.ShapeDtypeStruct((bsz, IDX_HEADS, s), F32),
    )
    return pl.pallas_call(
        _inproj_kernel,
        grid=(bsz, s // tm),
        in_specs=[row(d),
                  pl.BlockSpec((1, N_MOD, d), lambda b, i: (b, 0, 0)),
                  pl.BlockSpec((1, d), lambda b, i: (0, 0)),
                  const(wn.shape), const(wki.shape), const(wt.shape), const(wwi.shape)],
        out_specs=(row(d), row(d), row(d), row(d), row(d), row(IDX_DIM),
                   col(d), col(d), col(d), col(IDX_HEADS)),
        out_shape=out_shape,
        compiler_params=_cparams("arbitrary", "arbitrary"),
        name="in_proj",
    )(x, mod3, g0, wn, wki, wt, wwi)


def _gelu_tanh(x):
    return 0.5 * x * (1.0 + jnp.tanh(math.sqrt(2.0 / math.pi) * (x + 0.044715 * (x * x * x))))


def _rglru_kernel(u_ref, ug_ref, cw_ref, cb_ref, wax_ref, ba_ref, bx_ref, lam_ref, y_ref,
                  ext_ref, a_ref, b_ref, carry_ref):
    ts = u_ref.shape[1]
    d = D_MODEL

    @pl.when(pl.program_id(1) == 0)
    def _():
        ext_ref[0:SUBLANES, :] = jnp.zeros((SUBLANES, d), F32)
        carry_ref[...] = jnp.zeros_like(carry_ref)

    ext_ref[SUBLANES:SUBLANES + ts, :] = u_ref[0]
    xc = cb_ref[...] + cw_ref[CONV_W - 1:CONV_W, :] * ext_ref[SUBLANES:SUBLANES + ts, :]
    for k in range(CONV_W - 1):
        off = SUBLANES - (CONV_W - 1) + k
        xc = xc + cw_ref[k:k + 1, :] * ext_ref[off:off + ts, :]
    ext_ref[0:SUBLANES, :] = ext_ref[ts:ts + SUBLANES, :]

    nl = -lam_ref[...]
    sp = jnp.maximum(nl, 0.0) + jnp.log1p(jnp.exp(-jnp.abs(nl)))
    for n in range(RNN_BLOCKS):
        cs = slice(n * RNN_BW, (n + 1) * RNN_BW)
        xb = xc[:, cs]
        g = _bdot(xb, wax_ref[n])
        r = jax.nn.sigmoid(g[:, :RNN_BW] + ba_ref[:, cs])
        i = jax.nn.sigmoid(g[:, RNN_BW:] + bx_ref[:, cs])
        log_a = (-LRU_C) * r * sp[:, cs]
        a_ref[:, cs] = jnp.exp(log_a)
        th = jnp.tanh(log_a)
        b_ref[:, cs] = jnp.sqrt(-2.0 * th / (1.0 - th)) * (i * xb)

    row = lax.broadcasted_iota(jnp.int32, (SUBLANES, d), 0)

    def group(gi, hprev):
        r0 = pl.multiple_of(gi * SUBLANES, SUBLANES)
        a = a_ref[pl.ds(r0, SUBLANES), :]
        b = b_ref[pl.ds(r0, SUBLANES), :]
        for sh in (1, 2, 4):
            keep = row >= sh
            a_s = jnp.where(keep, pltpu.roll(a, sh, 0), 1.0)
            b_s = jnp.where(keep, pltpu.roll(b, sh, 0), 0.0)
            b = a * b_s + b
            a = a * a_s
        h = b + a * hprev
        b_ref[pl.ds(r0, SUBLANES), :] = h
        return jnp.broadcast_to(h[SUBLANES - 1:SUBLANES, :], (SUBLANES, d))

    carry_ref[...] = lax.fori_loop(0, ts // SUBLANES, group, carry_ref[...])
    y_ref[0] = (b_ref[...] * _gelu_tanh(ug_ref[0])).astype(BF16)


def _rglru(u_rnn, u_gate, conv_w, conv_b, w_rg_a, b_rg_a, w_rg_x, b_rg_x, lam):
    bsz, s, d = u_rnn.shape
    ts = SCAN_TILE
    wax = jnp.concatenate([w_rg_a, w_rg_x], axis=-1).astype(BF16)
    vec = lambda v: v.reshape(1, d)
    c2 = lambda shape: pl.BlockSpec(shape, lambda b, i: (0,) * len(shape))
    row = pl.BlockSpec((1, ts, d), lambda b, i: (b, i, 0))
    return pl.pallas_call(
        _rglru_kernel,
        grid=(bsz, s // ts),
        in_specs=[row, row, c2((CONV_W, d)), c2((1, d)), c2(wax.shape), c2((1, d)), c2((1, d)),
                  c2((1, d))],
        out_specs=row,
        out_shape=jax.ShapeDtypeStruct((bsz, s, d), BF16),
        scratch_shapes=[pltpu.VMEM((ts + SUBLANES, d), F32), pltpu.VMEM((ts, d), F32),
                        pltpu.VMEM((ts, d), F32), pltpu.VMEM((SUBLANES, d), F32)],
        compiler_params=_cparams("arbitrary", "arbitrary"),
        name="rglru",
    )(u_rnn, u_gate, conv_w, vec(conv_b), wax, vec(b_rg_a), vec(b_rg_x), vec(lam))


def _t5_bucket_np(dist):
    max_exact = N_BUCKETS // 2
    dd = np.maximum(dist, 0)
    df = np.maximum(dd, 1).astype(np.float32)
    large = max_exact + (np.log(df / np.float32(max_exact)) / np.float32(math.log(MAX_DIST / max_exact))
                         * np.float32(N_BUCKETS - max_exact)).astype(np.int32)
    large = np.minimum(large, N_BUCKETS - 1)
    return np.where(dd < max_exact, dd, large)


def _near_bucket_ids():
    r = np.arange(KEY_GROUP)[None, :, None]
    c = np.arange(Q_TILE)[None, None, :]
    o = np.arange(2)[:, None, None]
    return _t5_bucket_np(c - r - (o - 1) * KEY_GROUP).astype(np.int32)


def _attn_kernel(rb_ref, bkt_ref, qt_ref, qit_ref, wit_ref, k_ref, vt_ref, ki_ref, y_ref,
                 s_ref, tab_ref, acc_ref, m_ref, lg_ref):
    tq = Q_TILE
    ks = KEY_SUB
    kg = KEY_GROUP
    jq = pl.program_id(1)
    t0 = jq * tq
    ngrp = jq + 1
    lane_t = t0 + lax.broadcasted_iota(jnp.int32, (1, tq), 1)

    @pl.when((pl.program_id(0) == 0) & (jq == 0))
    def _():
        for o in range(2):
            for h in range(N_HEADS):
                tab_ref[h, o] = jnp.zeros((kg, tq), F32)

            def fill(b, c):
                hit = bkt_ref[o] == b
                for h in range(N_HEADS):
                    val = (rb_ref[b, h] - rb_ref[N_BUCKETS - 1, h]) * LOG2E
                    tab_ref[h, o] = jnp.where(hit, val, tab_ref[h, o])
                return c

            lax.fori_loop(0, N_BUCKETS - 1, fill, 0)

    wi = wit_ref[0] * (IDX_HEADS ** -0.5 * IDX_DIM ** -0.5)

    def score_sub(i, mnmx, masked):
        r0 = pl.multiple_of(i * ks, ks)
        kic = ki_ref[0, pl.ds(r0, ks), :]
        acc = jnp.zeros((ks, tq), F32)
        for h in range(IDX_HEADS):
            dts = jnp.dot(kic, qit_ref[0, h * IDX_DIM:(h + 1) * IDX_DIM, :],
                          preferred_element_type=F32)
            acc = acc + jnp.maximum(dts, 0.0) * wi[h:h + 1, :]
        lo_src = acc
        if masked:
            key_s = r0 + lax.broadcasted_iota(jnp.int32, (ks, tq), 0)
            causal = key_s <= lane_t
            acc = jnp.where(causal, acc, NEG_INF)
            lo_src = jnp.where(causal, acc, jnp.inf)
        s_ref[pl.ds(r0, ks), :] = acc
        mn, mx = mnmx
        mn = jnp.minimum(mn, jnp.min(lo_src.reshape(ks // SUBLANES, SUBLANES, tq), axis=0))
        mx = jnp.maximum(mx, jnp.max(acc.reshape(ks // SUBLANES, SUBLANES, tq), axis=0))
        return mn, mx

    mnmx = (jnp.full((SUBLANES, tq), jnp.inf, F32), jnp.full((SUBLANES, tq), NEG_INF, F32))
    def _score_run(first, n, c):
        for u in range(n):
            c = score_sub(first + u, c, False)
        return c

    n_plain = 2 * ngrp - 2
    mnmx = lax.fori_loop(0, n_plain // 8, lambda q, c: _score_run(8 * q, 8, c), mnmx)
    mnmx = lax.cond(n_plain % 8 >= 4, lambda c: _score_run(n_plain // 8 * 8, 4, c), lambda c: c, mnmx)
    mnmx = lax.cond(n_plain % 4 == 2, lambda c: _score_run(n_plain - 2, 2, c), lambda c: c, mnmx)
    mnmx = score_sub(2 * ngrp - 2, mnmx, True)
    mn8, mx8 = score_sub(2 * ngrp - 1, mnmx, True)
    smin = jnp.min(mn8, axis=0, keepdims=True)
    smax = jnp.max(mx8, axis=0, keepdims=True)

    n_causal = (lane_t + 1).astype(F32)
    k_eff = jnp.minimum(n_causal, float(TOPK_MAX))

    def count_rows(pred):
        part = 2 * SUBLANES

        def block(start, rows, c):
            r0 = pl.multiple_of(start, rows)
            key_s = (r0 + lax.broadcasted_iota(jnp.int32, (rows, tq), 0)).astype(F32)
            ind = jnp.where(pred(s_ref[pl.ds(r0, rows), :], key_s), 1.0, 0.0)
            return c + jnp.sum(ind.reshape(rows // part, part, tq), axis=0)

        c = lax.fori_loop(0, ngrp // 2, lambda g, c: block(g * (2 * kg), 2 * kg, c),
                          jnp.zeros((part, tq), F32))
        c = lax.cond(ngrp % 2 == 1, lambda c: block((ngrp - 1) * kg, kg, c), lambda c: c, c)
        return jnp.sum(c, axis=0, keepdims=True)

    def bis_step(st):
        it, lo, hi, c_lo, done = st
        first = (jnp.zeros((1, tq), F32) + jnp.where(it == 0, 1.0, 0.0)) > 0.0
        probe = jnp.where(first, smax, 0.5 * lo + 0.5 * hi)
        collapsed = ~first & ((probe <= lo) | (probe >= hi))
        cnt = count_rows(lambda blk, _: blk >= probe)
        ge = cnt >= k_eff
        upd = (done == 0.0) & ~collapsed
        lo_n = jnp.where(upd & ge, probe, lo)
        c_lo_n = jnp.where(upd & ge, cnt, c_lo)
        hi_n = jnp.where(upd & ~ge, probe, hi)
        fin = collapsed | (cnt == k_eff) | (first & ge)
        done_n = jnp.where(fin, 1.0, done)
        return it + 1, lo_n, hi_n, c_lo_n, done_n

    def n_open(st):
        return jnp.sum(1.0 - st[4])

    done0 = jnp.where(n_causal <= k_eff, 1.0, 0.0)
    st = lax.fori_loop(0, UNCHECKED_BISECT_STEPS, lambda _, s: bis_step(s),
                       (jnp.int32(0), smin, smax, n_causal, done0))
    st = lax.while_loop(lambda s: s[5] > 0.0,
                        lambda s: (lambda nxt: nxt + (n_open(nxt),))(bis_step(s[:5])),
                        st + (n_open(st),))
    thr, c_thr = st[1], st[3]

    tie_all = jnp.zeros((1, tq), F32) + (t0 + tq).astype(F32)

    def tie_limit():
        need = k_eff - count_rows(lambda blk, _: blk > thr)

        def body(_, st):
            lo, hi = st
            mid = jnp.floor(0.5 * (lo + hi))
            ok = count_rows(lambda blk, key_s: (blk == thr) & (key_s < mid)) >= need
            return jnp.where(ok, lo, mid), jnp.where(ok, mid, hi)

        n_steps = int(math.ceil(math.log2(s_ref.shape[0]))) + 1
        return lax.fori_loop(0, n_steps, body, (jnp.zeros((1, tq), F32), tie_all))[1]

    excess = jnp.sum(jnp.where(c_thr > k_eff, 1.0, 0.0))
    tie_lim = lax.cond(excess > 0.0, tie_limit, lambda: tie_all)

    def mask_body(g, c):
        r0 = pl.multiple_of(g * kg, kg)
        blk = s_ref[pl.ds(r0, kg), :]
        key_s = (r0 + lax.broadcasted_iota(jnp.int32, (kg, tq), 0)).astype(F32)
        sel = (blk > thr) | ((blk == thr) & (key_s < tie_lim))
        s_ref[pl.ds(r0, kg), :] = jnp.where(sel, 0.0, NEG_INF)
        return c

    lax.fori_loop(0, ngrp, mask_body, 0)

    m_ref[...] = jnp.full_like(m_ref, NEG_INF)
    acc_ref[...] = jnp.zeros_like(acc_ref)
    ones_rows = jnp.ones((DEN_ROWS, kg), BF16)

    def attend(g, near):
        r0 = pl.multiple_of(g * kg, kg)
        msk = s_ref[pl.ds(r0, kg), :]
        col_max = []
        for h in range(N_HEADS):
            hs = slice(h * HEAD_DIM, (h + 1) * HEAD_DIM)
            kh = k_ref[0, pl.ds(r0, kg), hs]
            lg = jnp.dot(kh, qt_ref[0, hs, :], preferred_element_type=F32) + msk
            if near is not None:
                lg = lg + tab_ref[h, near]
            lg_ref[h] = lg
            col_max.append(jnp.max(lg, axis=0, keepdims=True))
        for h in range(N_HEADS):
            hs = slice(h * HEAD_DIM, (h + 1) * HEAD_DIM)
            m_old = m_ref[h:h + 1, :]
            m_new = jnp.maximum(m_old, col_max[h])
            m_safe = jnp.where(m_new == NEG_INF, 0.0, m_new)
            p = jnp.exp2(lg_ref[h] - m_safe).astype(BF16)
            alpha = jnp.exp2(m_old - m_safe)
            m_ref[h:h + 1, :] = m_new
            vh = jnp.concatenate([vt_ref[0, hs, pl.ds(r0, kg)], ones_rows], axis=0)
            acc_ref[h] = alpha * acc_ref[h] + jnp.dot(vh, p, preferred_element_type=F32)

    n_far = jnp.maximum(ngrp - 2, 0)

    def _attend_run(first, n):
        for u in range(n):
            attend(first + u, None)

    def _attend_oct(go, c):
        _attend_run(8 * go, 8)
        return c

    lax.fori_loop(0, n_far // 8, _attend_oct, 0)

    @pl.when(n_far % 8 >= 4)
    def _():
        _attend_run(n_far // 8 * 8, 4)

    @pl.when(n_far % 4 >= 2)
    def _():
        _attend_run(n_far // 4 * 4, 2)

    @pl.when(n_far % 2 == 1)
    def _():
        attend(n_far - 1, None)

    @pl.when(jq > 0)
    def _():
        attend(ngrp - 2, 0)

    attend(ngrp - 1, 1)

    for h in range(N_HEADS):
        o = acc_ref[h, 0:HEAD_DIM, :] / acc_ref[h, HEAD_DIM:HEAD_DIM + 1, :]
        y_ref[0, :, h * HEAD_DIM:(h + 1) * HEAD_DIM] = o.T.astype(BF16)


def _attention(qt, qit, wit, k, vt, ki, rel_bias):
    bsz, d, s = qt.shape
    tq = Q_TILE
    bkt = jnp.asarray(_near_bucket_ids())
    once = lambda shape: pl.BlockSpec(shape, lambda b, j: (b,) + (0,) * (len(shape) - 1),
                                      pipeline_mode=pl.Buffered(1))
    col = lambda r: pl.BlockSpec((1, r, tq), lambda b, j: (b, 0, j))
    return pl.pallas_call(
        _attn_kernel,
        grid=(bsz, s // tq),
        in_specs=[pl.BlockSpec(memory_space=pltpu.SMEM),
                  pl.BlockSpec(bkt.shape, lambda b, j: (0, 0, 0)),
                  col(d), col(d), col(IDX_HEADS),
                  once((1, s, d)), once((1, d, s)), once((1, s, IDX_DIM))],
        out_specs=pl.BlockSpec((1, tq, d), lambda b, j: (b, j, 0)),
        out_shape=jax.ShapeDtypeStruct((bsz, s, d), BF16),
        scratch_shapes=[pltpu.VMEM((s, tq), F32),
                        pltpu.VMEM((N_HEADS, 2, KEY_GROUP, tq), F32),
                        pltpu.VMEM((N_HEADS, HEAD_DIM + DEN_ROWS, tq), F32),
                        pltpu.VMEM((N_HEADS, tq), F32),
                        pltpu.VMEM((N_HEADS, KEY_GROUP, tq), F32)],
        compiler_params=_cparams("arbitrary", "arbitrary"),
        name="attn",
    )(rel_bias, bkt, qt, qit, wit, k, vt, ki)


def _pack_bf16_pair(x):
    c = x.shape[1] // 2
    lo = lax.bitcast_convert_type(x[:, :c].astype(BF16).astype(F32), jnp.uint32) >> 16
    hi = lax.bitcast_convert_type(x[:, c:].astype(BF16).astype(F32), jnp.uint32) & jnp.uint32(0xFFFF0000)
    return lo | hi


def _unpack_bf16_pair(p):
    a = lax.bitcast_convert_type(p << 16, F32)
    b = lax.bitcast_convert_type(p & jnp.uint32(0xFFFF0000), F32)
    return a, b


def _unpack_rows_bf16(p):
    a, b = _unpack_bf16_pair(p)
    return jnp.concatenate([a, b], axis=1).astype(BF16)


def _merge_kernel(yr_ref, ya_ref, glr_ref, gla_ref, x_ref, mod_ref, g_ref,
                  wr_ref, wa_ref, wo_ref, x1_ref, h2p_ref):
    merged = (jax.nn.sigmoid(glr_ref[0]) * jnp.dot(yr_ref[0], wr_ref[...], preferred_element_type=F32)
              + jax.nn.sigmoid(gla_ref[0]) * jnp.dot(ya_ref[0], wa_ref[...], preferred_element_type=F32))
    y = _bdot(merged, wo_ref[...])
    x1 = x_ref[0] + mod_ref[0, 2:3, :] * _rms(y, g_ref[1:2, :])
    x1_ref[0] = x1
    h2 = _rms(x1, g_ref[2:3, :]) * (1.0 + mod_ref[0, 4:5, :]) + mod_ref[0, 3:4, :]
    h2p_ref[0] = _pack_bf16_pair(h2)


def _merge(y_rnn, y_attn, gl_rnn, gl_attn, x, mod3, gains, w_br_rnn, w_br_attn, w_out):
    bsz, s, d = x.shape
    tm = ROW_TILE
    row = pl.BlockSpec((1, tm, d), lambda b, i: (b, i, 0))
    half = pl.BlockSpec((1, tm, d // 2), lambda b, i: (b, i, 0))
    c2 = lambda shape: pl.BlockSpec(shape, lambda b, i: (0,) * len(shape))
    return pl.pallas_call(
        _merge_kernel,
        grid=(bsz, s // tm),
        in_specs=[row, row, row, row, row,
                  pl.BlockSpec((1, N_MOD, d), lambda b, i: (b, 0, 0)),
                  c2(gains.shape), c2((d, d)), c2((d, d)), c2((d, d))],
        out_specs=(row, half),
        out_shape=(jax.ShapeDtypeStruct((bsz, s, d), F32),
                   jax.ShapeDtypeStruct((bsz, s, d // 2), jnp.uint32)),
        compiler_params=_cparams("arbitrary", "arbitrary"),
        name="merge",
    )(y_rnn, y_attn, gl_rnn, gl_attn, x, mod3, gains,
      w_br_rnn.astype(BF16), w_br_attn.astype(BF16), w_out.astype(BF16))


def _router_kernel(hp_ref, wr_ref, rb_ref, dest_ref, wk_ref, seg_ref):
    t = hp_ref.shape[0]
    gsz = N_EXPERTS // N_GROUPS
    h = _unpack_rows_bf16(hp_ref[...])
    s = jax.nn.sigmoid(_bdot_nt(wr_ref[...], h))
    s_sel = s + rb_ref[...]
    g3 = s_sel.reshape(N_GROUPS, gsz, t)
    e_in_g = lax.broadcasted_iota(jnp.int32, (N_GROUPS, gsz, t), 1)
    top1 = jnp.max(g3, axis=1, keepdims=True)
    first = jnp.min(jnp.where(g3 == top1, e_in_g, gsz), axis=1, keepdims=True)
    top2 = jnp.max(jnp.where(e_in_g == first, NEG_INF, g3), axis=1, keepdims=True)
    gscore = jnp.broadcast_to(top1 + top2, (N_GROUPS, gsz, t))
    gi = lax.broadcasted_iota(jnp.int32, (N_GROUPS, gsz, t), 0)
    gmask = jnp.zeros((N_GROUPS, gsz, t), F32)
    for _ in range(TOPK_GROUPS):
        mx = jnp.max(gscore, axis=0, keepdims=True)
        pick = jnp.min(jnp.where(gscore == mx, gi, N_GROUPS), axis=0, keepdims=True)
        hit = gi == pick
        gmask = jnp.where(hit, 1.0, gmask)
        gscore = jnp.where(hit, NEG_INF, gscore)
    cand = jnp.where(gmask.reshape(N_EXPERTS, t) > 0.0, s_sel, NEG_INF)
    ei = lax.broadcasted_iota(jnp.int32, (N_EXPERTS, t), 0)
    sel = jnp.zeros((N_EXPERTS, t), F32)
    picks = []
    for _ in range(TOP_K):
        mx = jnp.max(cand, axis=0, keepdims=True)
        pick = jnp.min(jnp.where(cand == mx, ei, N_EXPERTS), axis=0, keepdims=True)
        hit = ei == pick
        sel = jnp.where(hit, 1.0, sel)
        cand = jnp.where(hit, NEG_INF, cand)
        picks.append(pick)
    w = s * sel
    w = w / jnp.sum(w, axis=0, keepdims=True) * ROUTED_SCALE

    selb = sel.astype(BF16)
    tok_r = lax.broadcasted_iota(jnp.int32, (t, t), 0)
    tok_c = lax.broadcasted_iota(jnp.int32, (t, t), 1)
    rank = jnp.dot(selb, jnp.where(tok_r < tok_c, 1.0, 0.0).astype(BF16), preferred_element_type=F32)
    e_r = lax.broadcasted_iota(jnp.int32, (N_EXPERTS, N_EXPERTS), 0)
    e_c = lax.broadcasted_iota(jnp.int32, (N_EXPERTS, N_EXPERTS), 1)
    cnt_col = jnp.sum(sel, axis=1, keepdims=True)
    pad_col = jnp.floor((cnt_col + (SUBLANES - 1.0)) * (1.0 / SUBLANES))
    off_col = SUBLANES * jnp.dot(jnp.where(e_c < e_r, 1.0, 0.0).astype(BF16),
                                 jnp.broadcast_to(pad_col, (N_EXPERTS, LANES)).astype(BF16),
                                 preferred_element_type=F32)[:, 0:1]
    slot = rank + off_col
    dest_rows, w_rows = [], []
    for k in range(TOP_K):
        hit = ei == picks[k]
        dest_rows.append(jnp.sum(jnp.where(hit, slot, 0.0), axis=0, keepdims=True))
        w_rows.append(jnp.sum(jnp.where(hit, w, 0.0), axis=0, keepdims=True))
    dest_ref[0] = (jnp.concatenate(dest_rows, axis=0) * SLAB).astype(jnp.int32)
    wk_ref[...] = jnp.concatenate(w_rows, axis=0)
    cnt_row = _bdot_nt(jnp.ones((SUBLANES, t), BF16), selb)
    pad_row = jnp.floor((cnt_row + (SUBLANES - 1.0)) * (1.0 / SUBLANES))
    off_row = SUBLANES * jnp.dot(pad_row.astype(BF16), jnp.where(e_r < e_c, 1.0, 0.0).astype(BF16),
                                 preferred_element_type=F32)
    seg_ref[0] = jnp.concatenate([off_row[0:1], cnt_row[0:1]], axis=1).astype(jnp.int32)


def _router(h2p, w_router, router_bias):
    n, dh = h2p.shape
    t = MOE_TILE
    nt = n // t
    return pl.pallas_call(
        _router_kernel,
        grid=(nt,),
        in_specs=[pl.BlockSpec((t, dh), lambda i: (i, 0)),
                  pl.BlockSpec((N_EXPERTS, 2 * dh), lambda i: (0, 0)),
                  pl.BlockSpec((N_EXPERTS, 1), lambda i: (0, 0))],
        out_specs=(pl.BlockSpec((1, TOP_K, t), lambda i: (i, 0, 0)),
                   pl.BlockSpec((TOP_K, t), lambda i: (0, i)),
                   pl.BlockSpec((1, 1, 2 * N_EXPERTS), lambda i: (i, 0, 0))),
        out_shape=(jax.ShapeDtypeStruct((nt, TOP_K, t), jnp.int32),
                   jax.ShapeDtypeStruct((TOP_K, n), F32),
                   jax.ShapeDtypeStruct((nt, 1, 2 * N_EXPERTS), jnp.int32)),
        compiler_params=_cparams("arbitrary"),
        name="router",
    )(h2p, w_router.T.astype(BF16), router_bias.reshape(N_EXPERTS, 1))


def _swiglu(x, wgu, wd):
    gu = jnp.dot(x, wgu, preferred_element_type=F32)
    g = gu[:, :D_EXPERT]
    a = (g * jax.nn.sigmoid(g)) * gu[:, D_EXPERT:]
    return jnp.dot(a.astype(BF16), wd, preferred_element_type=F32)


def _expert_block(tile, step, n_blocks):
    sc = jnp.minimum(step, n_blocks - 1)
    return jnp.where(tile % 2 == 0, sc, n_blocks - 1 - sc)


def _moe_kernel(seg_ref, hp_ref, dest_ref, wk_ref, wgu_ref, wd_ref, wsgu_ref, wsd_ref, x1_ref,
                mod_ref, g_ref, o_ref, xs_ref, acca_ref, accb_ref, act_ref, dest_smem, w_smem, sem):
    i = pl.program_id(0)
    s = pl.program_id(1)
    n_s = pl.num_programs(1)
    t = MOE_TILE

    def slab(ref, first):
        return ref.at[pl.ds(pl.multiple_of(first, SLAB), SLAB), :]

    def load_rows(ref, r0, n):
        return jnp.concatenate([ref[pl.ds(SLAB * r0 + c, n, stride=SLAB), :] for c in range(SLAB)],
                               axis=1)

    @pl.when((i == 0) & (s == 0))
    def _():
        xs_ref[...] = jnp.zeros_like(xs_ref)

    @pl.when(s == 0)
    def _():
        copies = (pltpu.make_async_copy(dest_ref, dest_smem, sem.at[0]),
                  pltpu.make_async_copy(wk_ref, w_smem, sem.at[1]))
        for cp in copies:
            cp.start()
        for cp in copies:
            cp.wait()

        def body(nb, c):
            n0 = nb * MOVE_TOKENS
            base = n0 * TOP_K
            for j in range(MOVE_TOKENS):
                row = slab(hp_ref, (n0 + j) * SLAB)[...]
                for k in range(TOP_K):
                    slab(xs_ref, dest_smem[base + (j * TOP_K + k)])[...] = row
            return c

        lax.fori_loop(0, t // MOVE_TOKENS, body, 0)

    def put_rows(r0, end, y, old):
        new = _pack_bf16_pair(y)
        rows = r0 + lax.broadcasted_iota(jnp.int32, (FFN_ROWS, LANES), 0)
        for c in range(SLAB):
            cs = slice(c * LANES, (c + 1) * LANES)
            xs_ref[pl.ds(SLAB * r0 + c, FFN_ROWS, stride=SLAB), :] = jnp.where(rows < end, new[:, cs],
                                                                               old[:, cs])

    @pl.when(s < n_s - 1)
    def _():
        offs, ends = [], []
        for eb in range(EXPERTS_PER_STEP):
            e = _expert_block(i, s, n_s - 1) * EXPERTS_PER_STEP + eb
            offs.append(pl.multiple_of(seg_ref[i, e], SUBLANES))
            ends.append(offs[eb] + seg_ref[i, N_EXPERTS + e])
        olds = [load_rows(xs_ref, offs[eb], FFN_ROWS) for eb in range(EXPERTS_PER_STEP)]
        for eb in range(EXPERTS_PER_STEP):
            gu = jnp.dot(_unpack_rows_bf16(olds[eb]), wgu_ref[eb], preferred_element_type=F32)
            g = gu[:, :D_EXPERT]
            act_ref[eb] = ((g * jax.nn.sigmoid(g)) * gu[:, D_EXPERT:]).astype(BF16)
        ys = [jnp.dot(act_ref[eb], wd_ref[eb], preferred_element_type=F32)
              for eb in range(EXPERTS_PER_STEP)]
        for eb in range(EXPERTS_PER_STEP):
            put_rows(offs[eb], ends[eb], ys[eb], olds[eb])
        for eb in range(EXPERTS_PER_STEP):
            def chunk(c, carry, eb=eb):
                r0 = pl.multiple_of(offs[eb] + c * FFN_ROWS, SUBLANES)
                old = load_rows(xs_ref, r0, FFN_ROWS)
                put_rows(r0, ends[eb], _swiglu(_unpack_rows_bf16(old), wgu_ref[eb], wd_ref[eb]), old)
                return carry

            n_chunks = lax.div(ends[eb] - offs[eb] + (FFN_ROWS - 1), FFN_ROWS)
            lax.fori_loop(1, n_chunks, chunk, 0)

    @pl.when(s == n_s - 1)
    def _():
        o_ref[...] = _swiglu(_unpack_rows_bf16(load_rows(hp_ref, 0, t)), wsgu_ref[...], wsd_ref[...])

        def body(nb, c):
            n0 = pl.multiple_of(nb * SUBLANES, SUBLANES)
            base = n0 * TOP_K
            for j in range(SUBLANES):
                acc_a = jnp.zeros((SLAB, LANES), F32)
                acc_b = jnp.zeros((SLAB, LANES), F32)
                for k in range(TOP_K):
                    idx = base + (j * TOP_K + k)
                    a, b = _unpack_bf16_pair(slab(xs_ref, dest_smem[idx])[...])
                    w = w_smem[idx]
                    acc_a = acc_a + a * w
                    acc_b = acc_b + b * w
                acca_ref[j * SLAB:(j + 1) * SLAB, :] = acc_a
                accb_ref[j * SLAB:(j + 1) * SLAB, :] = acc_b
            lo = [acca_ref[pl.ds(c, SUBLANES, stride=SLAB), :] for c in range(SLAB)]
            hi = [accb_ref[pl.ds(c, SUBLANES, stride=SLAB), :] for c in range(SLAB)]
            o_ref[pl.ds(n0, SUBLANES), :] += jnp.concatenate(lo + hi, axis=1)
            return c

        lax.fori_loop(0, t // SUBLANES, body, 0)
        o_ref[...] = x1_ref[...] + mod_ref[0, 5:6, :] * _rms(o_ref[...], g_ref[...])


def _moe(h2p, dest, wk, seg, wgu, wd, wsgu, wsd, x1, mod3, g3, seq):
    n, dh = h2p.shape
    d = 2 * dh
    t = MOE_TILE
    n_steps = N_EXPERTS // EXPERTS_PER_STEP + 1
    tiles_per_seq = seq // t
    tok = lambda w: pl.BlockSpec((t, w), lambda i, s: (i, 0))
    table = pl.BlockSpec((TOP_K * t,), lambda i, s: (i,))
    wblk = lambda r, c: pl.BlockSpec((EXPERTS_PER_STEP, r, c),
                                     lambda i, s: (_expert_block(i, s, n_steps - 1), 0, 0))
    c2 = lambda shape: pl.BlockSpec(shape, lambda i, s: (0,) * len(shape))
    xs_rows = TOP_K * t + N_EXPERTS * SUBLANES + FFN_ROWS
    return pl.pallas_call(
        _moe_kernel,
        grid=(n // t, n_steps),
        in_specs=[pl.BlockSpec(memory_space=pltpu.SMEM),
                  pl.BlockSpec((SLAB * t, LANES), lambda i, s: (i, 0)),
                  table, table,
                  wblk(d, 2 * D_EXPERT), wblk(D_EXPERT, d),
                  c2((d, 2 * D_EXPERT)), c2((D_EXPERT, d)),
                  tok(d),
                  pl.BlockSpec((1, N_MOD, d), lambda i, s: (i // tiles_per_seq, 0, 0)),
                  c2((1, d))],
        out_specs=tok(d),
        out_shape=jax.ShapeDtypeStruct((n, d), F32),
        scratch_shapes=[pltpu.VMEM((SLAB * xs_rows, LANES), jnp.uint32),
                        pltpu.VMEM((SLAB * SUBLANES, LANES), F32),
                        pltpu.VMEM((SLAB * SUBLANES, LANES), F32),
                        pltpu.VMEM((EXPERTS_PER_STEP, FFN_ROWS, D_EXPERT), BF16),
                        pltpu.SMEM((TOP_K * t,), jnp.int32),
                        pltpu.SMEM((TOP_K * t,), F32),
                        pltpu.SemaphoreType.DMA((2,))],
        compiler_params=_cparams("arbitrary", "arbitrary"),
        name="moe",
    )(seg, h2p.reshape(n * SLAB, LANES), dest, wk, wgu, wd, wsgu, wsd, x1, mod3, g3)


def kernel(x, c, w_ada, b_ada, norm_gain, w_in, conv_w, conv_b, w_rg_a, b_rg_a, w_rg_x, b_rg_x,
           lru_lambda, w_br_rnn, w_br_attn, w_out, rel_bias, w_router, router_bias,
           w_exp_gate, w_exp_up, w_exp_down, w_sh_gate, w_sh_up, w_sh_down):
    bsz, s, d = x.shape
    depth = w_ada.shape[0]
    assert d == D_MODEL and all(s % t == 0 for t in (ROW_TILE, SCAN_TILE, Q_TILE, MOE_TILE))
    for l in range(depth):
        mod = _ada(c, w_ada[l], b_ada[l])
        mod3 = mod.reshape(bsz, N_MOD, d)
        gains = norm_gain[l]
        (u_rnn, u_gate, k, gl_rnn, gl_attn, ki, qt, vt, qit, wit) = _in_proj(x, mod3, gains[0:1], w_in[l])
        y_rnn = _rglru(u_rnn, u_gate, conv_w[l], conv_b[l], w_rg_a[l], b_rg_a[l], w_rg_x[l],
                       b_rg_x[l], lru_lambda[l])
        y_attn = _attention(qt, qit, wit, k, vt, ki, rel_bias)
        x1, h2p = _merge(y_rnn, y_attn, gl_rnn, gl_attn, x, mod3, gains, w_br_rnn[l], w_br_attn[l],
                         w_out[l])
        h2p = h2p.reshape(bsz * s, d // 2)
        dest, wk, seg = _router(h2p, w_router[l], router_bias[l])
        wgu = jnp.concatenate([w_exp_gate[l], w_exp_up[l]], axis=-1).astype(BF16)
        wsgu = jnp.concatenate([w_sh_gate[l], w_sh_up[l]], axis=-1).astype(BF16)
        dest = jnp.transpose(dest, (0, 2, 1)).reshape(-1)
        wk = jnp.transpose(wk).reshape(-1)
        x = _moe(h2p, dest, wk, seg.reshape(-1, 2 * N_EXPERTS), wgu, w_exp_down[l].astype(BF16),
                 wsgu, w_sh_down[l].astype(BF16), x1.reshape(bsz * s, d), mod3, gains[3:4],
                 s).reshape(bsz, s, d)
    return x
```
